```python
import math
import jax, jax.numpy as jnp
from jax import lax
import numpy as np

D_MODEL = 4096
BATCH = 2
SEQ = 8192
DEPTH = 2
DEC_BATCH = 4
DEC_SEQ = 4096
PAST_LEN = 128

HEAD_DIM = 128
HYENA_WIDTH = D_MODEL // 4
FNET_WIDTH = D_MODEL // 4
ATTN_WIDTH = D_MODEL // 2
FNET_HEADS = FNET_WIDTH // HEAD_DIM
N_HEADS = ATTN_WIDTH // HEAD_DIM
N_KV_HEADS = 4
KV_GROUP = N_HEADS // N_KV_HEADS
KV_WIDTH = N_KV_HEADS * HEAD_DIM
WINDOW = 128
BLOCK = 128

HYENA_ORDER = 2
HYENA_IN = (HYENA_ORDER + 1) * HYENA_WIDTH
SHORT_CONV = 3
FILTER_EMB = 33
FILTER_HIDDEN = 64
FILTER_OUT = 2 * HYENA_ORDER * HYENA_WIDTH
DECAY_FAST_PCT = 0.3
DECAY_SLOW_PCT = 1.5
DECAY_TARGET = 1e-2

OFF_FNET = HYENA_IN
OFF_Q = OFF_FNET + FNET_WIDTH
OFF_K = OFF_Q + ATTN_WIDTH
OFF_V = OFF_K + KV_WIDTH
IN_WIDTH = OFF_V + KV_WIDTH

D_FF = -(-8 * D_MODEL // (3 * 256)) * 256

ALPHA = (2 * DEPTH) ** 0.25
BETA = (8 * DEPTH) ** -0.25
LN_EPS = 1e-5

kernel_name = "hyena_fnet_swa_hybrid_encoder"

F32 = jnp.float32


def _layer_norm(x, g, b):
    xf = x.astype(F32)
    mu = jnp.mean(xf, axis=-1, keepdims=True)
    var = jnp.mean(jnp.square(xf - mu), axis=-1, keepdims=True)
    return ((xf - mu) * lax.rsqrt(var + LN_EPS) * g.astype(F32) + b.astype(F32)).astype(x.dtype)


def _short_conv(u, w, b):
    L = u.shape[1]
    up = jnp.pad(u, ((0, 0), (1, 1), (0, 0)))
    return up[:, :L] * w[0] + up[:, 1:L + 1] * w[1] + up[:, 2:] * w[2] + b


def _hyena_filters(L, w1, b1, w2, b2, w3, b3, freq, w4):
    t = jnp.linspace(0.0, 1.0, L, dtype=F32)[:, None]
    bands = (FILTER_EMB - 1) // 2
    f = jnp.linspace(1e-4, bands - 1, bands, dtype=F32)
    w = (2.0 * math.pi / L) * jnp.arange(L, dtype=F32)[:, None]
    z = jnp.concatenate([t, jnp.cos(f * w), -jnp.sin(f * w)], axis=-1)
    fr = freq.astype(F32)
    h = jnp.sin(fr[0] * (z @ w1.astype(F32) + b1.astype(F32)))
    h = jnp.sin(fr[1] * (h @ w2.astype(F32) + b2.astype(F32)))
    h = jnp.sin(fr[2] * (h @ w3.astype(F32) + b3.astype(F32)))
    h = h @ w4.astype(F32)
    max_decay = math.log(DECAY_TARGET) / DECAY_FAST_PCT
    min_decay = math.log(DECAY_TARGET) / DECAY_SLOW_PCT
    deltas = jnp.abs(jnp.linspace(min_decay, max_decay, HYENA_WIDTH, dtype=F32))
    decay = jnp.exp(-t * deltas)
    return h.reshape(L, 2, HYENA_ORDER, HYENA_WIDTH) * decay[:, None, None, :]


def _bidir_fftconv(v, h_fwd, h_bwd, skip):
    L = v.shape[1]
    kern = jnp.concatenate([h_fwd, jnp.zeros_like(h_fwd[:1]), h_bwd[:0:-1]], axis=0)
    k_f = jnp.fft.rfft(kern, axis=0)
    v_f = jnp.fft.rfft(v, n=2 * L, axis=1)
    y = jnp.fft.irfft(v_f * k_f, n=2 * L, axis=1)[:, :L]
    return y + v * skip


def _hyena(u, short_w, short_b, filters, skip):
    uc = _short_conv(u, short_w, short_b).astype(F32)
    x1, x2, v = jnp.split(uc, 3, axis=-1)
    skip = skip.astype(F32)
    z = x1 * _bidir_fftconv(v, filters[:, 0, 0], filters[:, 1, 0], skip[0])
    z = x2 * _bidir_fftconv(z, filters[:, 0, 1], filters[:, 1, 1], skip[1])
    return z


def _fourier(u, w, b):
    B, L, _ = u.shape
    uh = u.astype(F32).reshape(B, L, FNET_HEADS, HEAD_DIM)
    y = jnp.fft.fft2(uh, axes=(1, 3), norm="ortho").real.reshape(B, L, FNET_WIDTH)
    return y @ w.astype(F32) + b.astype(F32)


def _window_attention(q, k, v, sink):
    B, L = q.shape[0], q.shape[1]
    nb = L // BLOCK
    qb = q.reshape(B, nb, BLOCK, N_KV_HEADS, KV_GROUP, HEAD_DIM)

    def band(a):
        ap = jnp.pad(a, ((0, 0), (BLOCK, BLOCK), (0, 0), (0, 0)))
        ap = ap.reshape(B, nb + 2, BLOCK, N_KV_HEADS, HEAD_DIM)
        return jnp.concatenate([ap[:, :-2], ap[:, 1:-1], ap[:, 2:]], axis=2)

    kb, vb = band(k), band(v)
    s = jnp.einsum('bnqhgd,bnshd->bnhgqs', qb, kb, preferred_element_type=F32) * (HEAD_DIM ** -0.5)
    qi = jnp.arange(BLOCK)[:, None]
    ki = jnp.arange(3 * BLOCK)[None, :]
    dist = qi + BLOCK - ki
    kpos = jnp.arange(nb)[:, None, None] * BLOCK + ki[None] - BLOCK
    valid = (jnp.abs(dist)[None] <= WINDOW) & (kpos >= 0) & (kpos < L)
    slopes = (2.0 ** (-8.0 * jnp.arange(1, N_HEADS + 1, dtype=F32) / N_HEADS)).reshape(N_KV_HEADS, KV_GROUP)
    s = s - slopes[:, :, None, None] * jnp.abs(dist).astype(F32)
    s = jnp.where(valid[None, :, None, None], s, -jnp.inf)
    sk = sink.astype(F32).reshape(N_KV_HEADS, KV_GROUP)[:, :, None, None]
    m = jnp.maximum(jnp.max(s, axis=-1, keepdims=True), sk)
    p = jnp.exp(s - m)
    denom = jnp.sum(p, axis=-1, keepdims=True) + jnp.exp(sk - m)
    o = jnp.einsum('bnhgqs,bnshd->bnqhgd', p / denom, vb.astype(F32))
    return o.reshape(B, L, ATTN_WIDTH)


def _trunk(x, ln0_g, ln0_b, w_in, short_w, short_b, filt_w1, filt_b1, filt_w2, filt_b2, filt_w3, filt_b3,
           filt_freq, filt_w4, hyena_skip, w_fnet, b_fnet, attn_sink, w_out, ln1_g, ln1_b,
           w_gate, w_up, w_down, ln2_g, ln2_b):
    B, L, _ = x.shape
    x = _layer_norm(x, ln0_g, ln0_b)
    for l in range(DEPTH):
        filters = _hyena_filters(L, filt_w1[l], filt_b1[l], filt_w2[l], filt_b2[l], filt_w3[l], filt_b3[l],
                                 filt_freq[l], filt_w4[l])
        proj = x @ w_in[l]
        u_h, u_f, q, k, v = jnp.split(proj, [OFF_FNET, OFF_Q, OFF_K, OFF_V], axis=-1)
        y_h = _hyena(u_h, short_w[l], short_b[l], filters, hyena_skip[l])
        y_f = _fourier(u_f, w_fnet[l], b_fnet[l])
        y_a = _window_attention(q.reshape(B, L, N_HEADS, HEAD_DIM),
                                k.reshape(B, L, N_KV_HEADS, HEAD_DIM),
                                v.reshape(B, L, N_KV_HEADS, HEAD_DIM), attn_sink[l])
        mix = jnp.concatenate([y_h, y_f, y_a], axis=-1).astype(x.dtype) @ w_out[l]
        x = _layer_norm(ALPHA * x + mix, ln1_g[l], ln1_b[l])
        ff = (jax.nn.silu(x @ w_gate[l]) * (x @ w_up[l])) @ w_down[l]
        x = _layer_norm(ALPHA * x + ff, ln2_g[l], ln2_b[l])
    return x


def setup_inputs(seed: int = 0) -> dict:
    key = jax.random.key(seed)
    ks = jax.random.split(key, 32)
    nrm = lambda k, shape, scale: jax.random.normal(k, shape, F32) * scale
    col_scale = jnp.concatenate([jnp.ones((IN_WIDTH - KV_WIDTH,), F32), jnp.full((KV_WIDTH,), BETA, F32)])
    return {
        "x_prompt": nrm(ks[0], (BATCH, SEQ, D_MODEL), 1.0),
        "x_sample": nrm(ks[1], (DEC_BATCH, DEC_SEQ, D_MODEL), 1.0),
        "ln0_g": 1.0 + nrm(ks[2], (D_MODEL,), 0.02),
        "ln0_b": nrm(ks[3], (D_MODEL,), 0.02),
        "w_in": nrm(ks[4], (DEPTH, D_MODEL, IN_WIDTH), D_MODEL ** -0.5) * col_scale,
        "short_w": nrm(ks[5], (DEPTH, SHORT_CONV, HYENA_IN), SHORT_CONV ** -0.5),
        "short_b": nrm(ks[6], (DEPTH, HYENA_IN), 0.02),
        "filt_w1": nrm(ks[7], (DEPTH, FILTER_EMB, FILTER_HIDDEN), FILTER_EMB ** -0.5),
        "filt_b1": nrm(ks[8], (DEPTH, FILTER_HIDDEN), 0.1),
        "filt_w2": nrm(ks[9], (DEPTH, FILTER_HIDDEN, FILTER_HIDDEN), FILTER_HIDDEN ** -0.5),
        "filt_b2": nrm(ks[10], (DEPTH, FILTER_HIDDEN), 0.1),
        "filt_w3": nrm(ks[11], (DEPTH, FILTER_HIDDEN, FILTER_HIDDEN), FILTER_HIDDEN ** -0.5),
        "filt_b3": nrm(ks[12], (DEPTH, FILTER_HIDDEN), 0.1),
        "filt_freq": 1.0 + nrm(ks[13], (DEPTH, 3, FILTER_HIDDEN), 0.02),
        "filt_w4": nrm(ks[14], (DEPTH, FILTER_HIDDEN, FILTER_OUT), 0.05 * FILTER_HIDDEN ** -0.5),
        "hyena_skip": nrm(ks[15], (DEPTH, HYENA_ORDER, HYENA_WIDTH), 1.0),
        "w_fnet": nrm(ks[16], (DEPTH, FNET_WIDTH, FNET_WIDTH), FNET_WIDTH ** -0.5),
        "b_fnet": nrm(ks[17], (DEPTH, FNET_WIDTH), 0.02),
        "attn_sink": nrm(ks[18], (DEPTH, N_HEADS), 0.5),
        "w_out": nrm(ks[19], (DEPTH, D_MODEL, D_MODEL), BETA * D_MODEL ** -0.5),
        "ln1_g": 1.0 + nrm(ks[20], (DEPTH, D_MODEL), 0.02),
        "ln1_b": nrm(ks[21], (DEPTH, D_MODEL), 0.02),
        "w_gate": nrm(ks[22], (DEPTH, D_MODEL, D_FF), D_MODEL ** -0.5),
        "w_up": nrm(ks[23], (DEPTH, D_MODEL, D_FF), D_MODEL ** -0.5),
        "w_down": nrm(ks[24], (DEPTH, D_FF, D_MODEL), BETA * D_FF ** -0.5),
        "ln2_g": 1.0 + nrm(ks[25], (DEPTH, D_MODEL), 0.02),
        "ln2_b": nrm(ks[26], (DEPTH, D_MODEL), 0.02),
    }


def reference(x_prompt, x_sample, ln0_g, ln0_b, w_in, short_w, short_b, filt_w1, filt_b1, filt_w2, filt_b2,
              filt_w3, filt_b3, filt_freq, filt_w4, hyena_skip, w_fnet, b_fnet, attn_sink, w_out,
              ln1_g, ln1_b, w_gate, w_up, w_down, ln2_g, ln2_b):
    y_prompt = _trunk(x_prompt, ln0_g, ln0_b, w_in, short_w, short_b, filt_w1, filt_b1, filt_w2, filt_b2,
                      filt_w3, filt_b3, filt_freq, filt_w4, hyena_skip, w_fnet, b_fnet, attn_sink, w_out,
                      ln1_g, ln1_b, w_gate, w_up, w_down, ln2_g, ln2_b)
    y_sample = _trunk(x_sample, ln0_g, ln0_b, w_in, short_w, short_b, filt_w1, filt_b1, filt_w2, filt_b2,
                      filt_w3, filt_b3, filt_freq, filt_w4, hyena_skip, w_fnet, b_fnet, attn_sink, w_out,
                      ln1_g, ln1_b, w_gate, w_up, w_down, ln2_g, ln2_b)
    return (y_prompt, y_sample)
```

```python
import functools
import math

import jax
import jax.numpy as jnp
from jax import lax
from jax.experimental import pallas as pl
from jax.experimental.pallas import tpu as pltpu

F32 = jnp.float32
BF16 = jnp.bfloat16

D_MODEL = 4096
HEAD_DIM = 128
HYENA_WIDTH = 1024
FNET_WIDTH = 1024
ATTN_WIDTH = 2048
FNET_HEADS = 8
N_HEADS = 16
N_KV_HEADS = 4
KV_GROUP = 4
KV_WIDTH = 512
BLOCK = 128
HYENA_IN = 3 * HYENA_WIDTH
FILTER_EMB = 33
FILTER_HIDDEN = 64
FILTER_OUT = 4 * HYENA_WIDTH
OFF_FNET = HYENA_IN
OFF_Q = OFF_FNET + FNET_WIDTH
OFF_K = OFF_Q + ATTN_WIDTH
OFF_V = OFF_K + KV_WIDTH
IN_WIDTH = OFF_V + KV_WIDTH
D_FF = 11008
D_FF_PAD = 11264
DEPTH = 2
ALPHA = (2 * DEPTH) ** 0.25
LN_EPS = 1e-5
DECAY_FAST_PCT = 0.3
DECAY_SLOW_PCT = 1.5
DECAY_TARGET = 1e-2

DFT_N2 = 128
VMEM_LIMIT = 56 * 1024 * 1024


def _cparams(sem, vmem=VMEM_LIMIT):
    return pltpu.CompilerParams(dimension_semantics=sem, vmem_limit_bytes=vmem)


def _dot(a, b):
    return jnp.dot(a, b, preferred_element_type=F32)


def _split(x):
    hi = x.astype(BF16)
    lo = (x - hi.astype(F32)).astype(BF16)
    return hi, lo


def _dot3_left(m_hi, m_lo, x):
    x_hi, x_lo = _split(x)
    return _dot(m_hi, x_hi) + (_dot(m_hi, x_lo) + _dot(m_lo, x_hi))


def _dot3_right(x, m_hi, m_lo):
    x_hi, x_lo = _split(x)
    return _dot(x_hi, m_hi) + (_dot(x_lo, m_hi) + _dot(x_hi, m_lo))


def _split_const(m):
    m = m.astype(F32)
    hi = m.astype(BF16)
    lo = (m - hi.astype(F32)).astype(BF16)
    return hi, lo


def _cs(num, den):
    ang = (2.0 * math.pi / den) * (num % den).astype(F32)
    return jnp.cos(ang), jnp.sin(ang)


def _ln_math(x, g, b):
    mu = jnp.mean(x, axis=-1, keepdims=True)
    xc = x - mu
    var = jnp.mean(xc * xc, axis=-1, keepdims=True)
    return xc * lax.rsqrt(var + LN_EPS) * g + b


def _ln0_kernel(xp_ref, xs_ref, g_ref, b_ref, of_ref, ob_ref, *, n_p):
    i = pl.program_id(0)

    def emit(x):
        y = _ln_math(x, g_ref[...], b_ref[...])
        of_ref[...] = y
        ob_ref[...] = y.astype(BF16)

    @pl.when(i < n_p)
    def _():
        emit(xp_ref[...])

    @pl.when(i >= n_p)
    def _():
        emit(xs_ref[...])


def _ln0(xp, xs, g, b, tm):
    tp, ts = xp.shape[0], xs.shape[0]
    n_p, n_s = tp // tm, ts // tm
    t = tp + ts
    return pl.pallas_call(
        functools.partial(_ln0_kernel, n_p=n_p),
        grid=(n_p + n_s,),
        in_specs=[
            pl.BlockSpec((tm, D_MODEL), lambda i: (jnp.minimum(i, n_p - 1), 0)),
            pl.BlockSpec((tm, D_MODEL), lambda i: (jnp.maximum(i - n_p, 0), 0)),
            pl.BlockSpec((1, D_MODEL), lambda i: (0, 0)),
            pl.BlockSpec((1, D_MODEL), lambda i: (0, 0)),
        ],
        out_specs=[
            pl.BlockSpec((tm, D_MODEL), lambda i: (i, 0)),
            pl.BlockSpec((tm, D_MODEL), lambda i: (i, 0)),
        ],
        out_shape=[jax.ShapeDtypeStruct((t, D_MODEL), F32), jax.ShapeDtypeStruct((t, D_MODEL), BF16)],
        compiler_params=_cparams(("parallel",)),
        name="ln0",
    )(xp, xs, g.reshape(1, -1), b.reshape(1, -1))


def _res_ln_kernel(x_ref, a_ref, g_ref, b_ref, of_ref, ob_ref):
    y = _ln_math(ALPHA * x_ref[...] + a_ref[...], g_ref[...], b_ref[...])
    of_ref[...] = y
    ob_ref[...] = y.astype(BF16)


def _res_ln(x, add, g, b, tm):
    t = x.shape[0]
    row = pl.BlockSpec((tm, D_MODEL), lambda i: (i, 0))
    vec = pl.BlockSpec((1, D_MODEL), lambda i: (0, 0))
    return pl.pallas_call(
        _res_ln_kernel,
        grid=(t // tm,),
        in_specs=[row, row, vec, vec],
        out_specs=[row, row],
        out_shape=[jax.ShapeDtypeStruct((t, D_MODEL), F32), jax.ShapeDtypeStruct((t, D_MODEL), BF16)],
        compiler_params=_cparams(("parallel",)),
        name="res_ln",
    )(x, add, g.reshape(1, -1), b.reshape(1, -1))


def _res_ln_final_kernel(x_ref, a_ref, g_ref, b_ref, op_ref, os_ref, *, n_p):
    i = pl.program_id(0)
    y = _ln_math(ALPHA * x_ref[...] + a_ref[...], g_ref[...], b_ref[...])

    @pl.when(i < n_p)
    def _():
        op_ref[...] = y

    @pl.when(i >= n_p)
    def _():
        os_ref[...] = y


def _res_ln_final(x, add, g, b, tm, tp):
    t = x.shape[0]
    n_p = tp // tm
    row = pl.BlockSpec((tm, D_MODEL), lambda i: (i, 0))
    vec = pl.BlockSpec((1, D_MODEL), lambda i: (0, 0))
    return pl.pallas_call(
        functools.partial(_res_ln_final_kernel, n_p=n_p),
        grid=(t // tm,),
        in_specs=[row, row, vec, vec],
        out_specs=[
            pl.BlockSpec((tm, D_MODEL), lambda i: (jnp.minimum(i, n_p - 1), 0)),
            pl.BlockSpec((tm, D_MODEL), lambda i: (jnp.maximum(i - n_p, 0), 0)),
        ],
        out_shape=[jax.ShapeDtypeStruct((tp, D_MODEL), F32), jax.ShapeDtypeStruct((t - tp, D_MODEL), F32)],
        compiler_params=_cparams(("arbitrary",)),
        name="res_ln_final",
    )(x, add, g.reshape(1, -1), b.reshape(1, -1))


def _mm_kernel(*refs, widths, has_bias):
    n_a = len(widths)
    a_refs = refs[:n_a]
    w_ref = refs[n_a]
    o_ref = refs[-1]
    acc = None
    off = 0
    for a_ref, wd in zip(a_refs, widths):
        part = _dot(a_ref[...], w_ref[off:off + wd, :])
        acc = part if acc is None else acc + part
        off += wd
    if has_bias:
        acc = acc + refs[n_a + 1][...]
    o_ref[...] = acc.astype(o_ref.dtype)


def _matmul(a_list, w, bias, out_dtype, tm, tn, name):
    t = a_list[0].shape[0]
    k, n = w.shape
    widths = tuple(a.shape[1] for a in a_list)
    assert sum(widths) == k and t % tm == 0 and n % tn == 0
    in_specs = [pl.BlockSpec((tm, wd), lambda i, j: (i, 0)) for wd in widths]
    in_specs.append(pl.BlockSpec((k, tn), lambda i, j: (0, j)))
    args = list(a_list) + [w]
    if bias is not None:
        in_specs.append(pl.BlockSpec((1, tn), lambda i, j: (0, j)))
        args.append(bias.reshape(1, n).astype(F32))
    return pl.pallas_call(
        functools.partial(_mm_kernel, widths=widths, has_bias=bias is not None),
        grid=(t // tm, n // tn),
        in_specs=in_specs,
        out_specs=pl.BlockSpec((tm, tn), lambda i, j: (i, j)),
        out_shape=jax.ShapeDtypeStruct((t, n), out_dtype),
        compiler_params=_cparams(("parallel", "parallel")),
        name=name,
    )(*args)


def _gate_up_kernel(x_ref, wg_ref, wu_ref, o_ref):
    x = x_ref[...]
    g = _dot(x, wg_ref[...])
    u = _dot(x, wu_ref[...])
    o_ref[...] = (g * (1.0 / (1.0 + jnp.exp(-g))) * u).astype(o_ref.dtype)


def _gate_up(x, wg, wu, tm, tn):
    t, k = x.shape
    n = wg.shape[1]
    return pl.pallas_call(
        _gate_up_kernel,
        grid=(t // tm, n // tn),
        in_specs=[
            pl.BlockSpec((tm, k), lambda i, j: (i, 0)),
            pl.BlockSpec((k, tn), lambda i, j: (0, j)),
            pl.BlockSpec((k, tn), lambda i, j: (0, j)),
        ],
        out_specs=pl.BlockSpec((tm, tn), lambda i, j: (i, j)),
        out_shape=jax.ShapeDtypeStruct((t, n), BF16),
        compiler_params=_cparams(("parallel", "parallel")),
        name="ffn_gate_up",
    )(x, wg, wu)


def _attn_kernel(q_ref, kp_ref, kc_ref, kn_ref, vp_ref, vc_ref, vn_ref, slope_ref, sink_ref, o_ref,
                 *, gp, nbp, nbs):
    g = pl.program_id(0)
    h = pl.program_id(1)
    in_p = g < gp
    n_loc = jnp.where(in_p, g % nbp, (g - gp) % nbs)
    nb = jnp.where(in_p, nbp, nbs)
    has_prev = n_loc > 0
    has_next = n_loc < nb - 1

    qi = lax.broadcasted_iota(jnp.int32, (BLOCK, BLOCK), 0)
    ki = lax.broadcasted_iota(jnp.int32, (BLOCK, BLOCK), 1)
    d_prev = qi + BLOCK - ki
    d_cur = qi - ki
    d_next = qi - BLOCK - ki
    ok_prev = jnp.logical_and(d_prev <= BLOCK, has_prev)
    ok_next = jnp.logical_and(-d_next <= BLOCK, has_next)
    a_prev = d_prev.astype(F32)
    a_cur = jnp.abs(d_cur).astype(F32)
    a_next = (-d_next).astype(F32)

    kp = kp_ref[...].astype(BF16)
    kc = kc_ref[...].astype(BF16)
    kn = kn_ref[...].astype(BF16)
    vp = vp_ref[...].astype(BF16)
    vc = vc_ref[...].astype(BF16)
    vn = vn_ref[...].astype(BF16)
    scale = HEAD_DIM ** -0.5
    neg_inf = -jnp.inf
    dn = (((1,), (1,)), ((), ()))
    for gi in range(KV_GROUP):
        head = h * KV_GROUP + gi
        slope = slope_ref[head]
        sink = sink_ref[head]
        q = q_ref[:, gi * HEAD_DIM:(gi + 1) * HEAD_DIM].astype(BF16)
        s_p = lax.dot_general(q, kp, dn, preferred_element_type=F32) * scale - slope * a_prev
        s_c = lax.dot_general(q, kc, dn, preferred_element_type=F32) * scale - slope * a_cur
        s_n = lax.dot_general(q, kn, dn, preferred_element_type=F32) * scale - slope * a_next
        s_p = jnp.where(ok_prev, s_p, neg_inf)
        s_n = jnp.where(ok_next, s_n, neg_inf)
        m = jnp.maximum(jnp.maximum(jnp.max(s_p, axis=-1, keepdims=True), jnp.max(s_c, axis=-1, keepdims=True)),
                        jnp.max(s_n, axis=-1, keepdims=True))
        m = jnp.maximum(m, sink)
        p_p = jnp.exp(s_p - m)
        p_c = jnp.exp(s_c - m)
        p_n = jnp.exp(s_n - m)
        denom = (jnp.sum(p_p, axis=-1, keepdims=True) + jnp.sum(p_c, axis=-1, keepdims=True)
                 + jnp.sum(p_n, axis=-1, keepdims=True) + jnp.exp(sink - m))
        inv = 1.0 / denom
        o = (_dot((p_p * inv).astype(BF16), vp) + _dot((p_c * inv).astype(BF16), vc)
             + _dot((p_n * inv).astype(BF16), vn))
        o_ref[:, gi * HEAD_DIM:(gi + 1) * HEAD_DIM] = o.astype(o_ref.dtype)


def _attention(proj, sink, tp, lp, ls):
    t = proj.shape[0]
    ng = t // BLOCK
    gp, nbp, nbs = tp // BLOCK, lp // BLOCK, ls // BLOCK
    slopes = 2.0 ** (-8.0 * jnp.arange(1, N_HEADS + 1, dtype=F32) / N_HEADS)
    qc = OFF_Q // (KV_GROUP * HEAD_DIM)
    kc = OFF_K // HEAD_DIM
    vc = OFF_V // HEAD_DIM
    prev = lambda g: jnp.maximum(g - 1, 0)
    nxt = lambda g: jnp.minimum(g + 1, ng - 1)
    blk = (BLOCK, HEAD_DIM)
    smem = pl.BlockSpec(memory_space=pltpu.SMEM)
    return pl.pallas_call(
        functools.partial(_attn_kernel, gp=gp, nbp=nbp, nbs=nbs),
        grid=(ng, N_KV_HEADS),
        in_specs=[
            pl.BlockSpec((BLOCK, KV_GROUP * HEAD_DIM), lambda g, h: (g, qc + h)),
            pl.BlockSpec(blk, lambda g, h: (prev(g), kc + h)),
            pl.BlockSpec(blk, lambda g, h: (g, kc + h)),
            pl.BlockSpec(blk, lambda g, h: (nxt(g), kc + h)),
            pl.BlockSpec(blk, lambda g, h: (prev(g), vc + h)),
            pl.BlockSpec(blk, lambda g, h: (g, vc + h)),
            pl.BlockSpec(blk, lambda g, h: (nxt(g), vc + h)),
            smem, smem,
        ],
        out_specs=pl.BlockSpec((BLOCK, KV_GROUP * HEAD_DIM), lambda g, h: (g, h)),
        out_shape=jax.ShapeDtypeStruct((t, ATTN_WIDTH), BF16),
        compiler_params=_cparams(("parallel", "parallel")),
        name="band_attention",
    )(proj, proj, proj, proj, proj, proj, proj, slopes, sink.astype(F32))


def _filter_kernel(fvec_ref, w1_ref, b1_ref, w2_ref, b2_ref, w3_ref, b3_ref, fr_ref, w4_ref, delta_ref, o_ref,
                   *, seq, tl):
    r0 = pl.program_id(0) * tl
    row = (lax.broadcasted_iota(jnp.int32, (tl, 128), 0) + r0).astype(F32)
    lane = lax.broadcasted_iota(jnp.int32, (tl, 128), 1)
    t = row * (1.0 / (seq - 1))
    ang = (row * (2.0 * math.pi / seq)) * fvec_ref[...]
    z = jnp.where(lane == 0, t,
                  jnp.where(lane <= 16, jnp.cos(ang), jnp.where(lane <= 32, -jnp.sin(ang), 0.0)))
    hp = lax.Precision.HIGHEST
    h = jnp.sin(fr_ref[0:1, :] * (jnp.dot(z, w1_ref[...], precision=hp, preferred_element_type=F32) + b1_ref[...]))
    h = jnp.sin(fr_ref[1:2, :] * (jnp.dot(h, w2_ref[...], precision=hp, preferred_element_type=F32) + b2_ref[...]))
    h = jnp.sin(fr_ref[2:3, :] * (jnp.dot(h, w3_ref[...], precision=hp, preferred_element_type=F32) + b3_ref[...]))
    trow = (lax.broadcasted_iota(jnp.int32, (tl, HYENA_WIDTH), 0) + r0)
    decay = jnp.exp(-(trow.astype(F32) * (1.0 / (seq - 1))) * delta_ref[...])
    first = trow == 0
    for part in range(4):
        sl = slice(part * HYENA_WIDTH, (part + 1) * HYENA_WIDTH)
        v = jnp.dot(h, w4_ref[:, sl], precision=hp, preferred_element_type=F32) * decay
        if part >= 2:
            v = jnp.where(first, 0.0, v)
        o_ref[:, sl] = v


def _hyena_filters(seq, w1, b1, w2, b2, w3, b3, freq, w4):
    tl = min(seq, 512)
    bands = (FILTER_EMB - 1) // 2
    f = jnp.linspace(1e-4, bands - 1, bands, dtype=F32)
    fvec = jnp.zeros((1, 128), F32).at[0, 1:1 + bands].set(f).at[0, 1 + bands:1 + 2 * bands].set(f)
    w1p = jnp.zeros((128, FILTER_HIDDEN), F32).at[:FILTER_EMB].set(w1.astype(F32))
    max_decay = math.log(DECAY_TARGET) / DECAY_FAST_PCT
    min_decay = math.log(DECAY_TARGET) / DECAY_SLOW_PCT
    deltas = jnp.abs(jnp.linspace(min_decay, max_decay, HYENA_WIDTH, dtype=F32)).reshape(1, -1)
    full = lambda a: pl.BlockSpec(a.shape, lambda i: (0,) * a.ndim)
    args = [fvec, w1p, b1.reshape(1, -1).astype(F32), w2.astype(F32), b2.reshape(1, -1).astype(F32),
            w3.astype(F32), b3.reshape(1, -1).astype(F32), freq.astype(F32), w4.astype(F32), deltas]
    return pl.pallas_call(
        functools.partial(_filter_kernel, seq=seq, tl=tl),
        grid=(seq // tl,),
        in_specs=[full(a) for a in args],
        out_specs=pl.BlockSpec((tl, FILTER_OUT), lambda i: (i, 0)),
        out_shape=jax.ShapeDtypeStruct((seq, FILTER_OUT), F32),
        compiler_params=_cparams(("parallel",)),
        name="hyena_filter",
    )(*args)


def _short_conv_kernel(u_ref, w_ref, b_ref, o_ref, *, seq, rc):
    w0 = w_ref[0:1, :]
    w1 = w_ref[1:2, :]
    w2 = w_ref[2:3, :]
    b = b_ref[...]
    rid = lax.broadcasted_iota(jnp.int32, (rc, 128), 0)
    for r0 in range(0, seq, rc):
        cur = u_ref[r0:r0 + rc, :]
        prev = pltpu.roll(cur, 1, 0)
        if r0 > 0:
            edge = u_ref[r0 - 8:r0, :][7:8, :]
            prev = jnp.where(rid == 0, edge, prev)
        else:
            prev = jnp.where(rid == 0, 0.0, prev)
        nxt = pltpu.roll(cur, rc - 1, 0)
        if r0 + rc < seq:
            edge = u_ref[r0 + rc:r0 + rc + 8, :][0:1, :]
            nxt = jnp.where(rid == rc - 1, edge, nxt)
        else:
            nxt = jnp.where(rid == rc - 1, 0.0, nxt)
        o_ref[r0:r0 + rc, :] = prev * w0 + cur * w1 + nxt * w2 + b


def _short_conv(proj, short_w, short_b, row0, nbatch, seq):
    assert row0 % seq == 0
    b0 = row0 // seq
    rc = min(seq, 256)
    return pl.pallas_call(
        functools.partial(_short_conv_kernel, seq=seq, rc=rc),
        grid=(nbatch, HYENA_IN // 128),
        in_specs=[
            pl.BlockSpec((seq, 128), lambda b, j: (b0 + b, j)),
            pl.BlockSpec((3, 128), lambda b, j: (0, j)),
            pl.BlockSpec((1, 128), lambda b, j: (0, j)),
        ],
        out_specs=pl.BlockSpec((seq, 128), lambda b, j: (b, j)),
        out_shape=jax.ShapeDtypeStruct((nbatch * seq, HYENA_IN), F32),
        compiler_params=_cparams(("parallel", "parallel")),
        name="hyena_short_conv",
    )(proj, short_w.astype(F32), short_b.reshape(1, -1).astype(F32))


def _wide_kernel(*refs, mode):
    mh_ref, ml_ref, x_ref = refs[0], refs[1], refs[2]
    o_ref = refs[-1]
    y = _dot3_left(mh_ref[...], ml_ref[...], x_ref[...])
    if mode == "gate":
        v_ref, gate_ref, skip_ref = refs[3], refs[4], refs[5]
        y = gate_ref[...] * (y + v_ref[...] * skip_ref[...])
    o_ref[...] = y.astype(o_ref.dtype)


def _wide_matmul(mat, x, extra=None, out_dtype=F32, name="dft_wide"):
    bsz, kd, wdt = x.shape
    r = mat.shape[0]
    wt = min(wdt, 4096)
    mh, ml = _split_const(mat)
    mspec = pl.BlockSpec((r, kd), lambda b, j: (0, 0))
    in_specs = [mspec, mspec, pl.BlockSpec((None, kd, wt), lambda b, j: (b, 0, j))]
    args = [mh, ml, x]
    mode = "plain"
    if extra is not None:
        v, gate, skip_row = extra
        mode = "gate"
        in_specs += [pl.BlockSpec((None, r, wt), lambda b, j: (b, 0, j)),
                     pl.BlockSpec((None, r, wt), lambda b, j: (b, 0, j)),
                     pl.BlockSpec((1, wt), lambda b, j: (0, j))]
        args += [v, gate, skip_row]
    return pl.pallas_call(
        functools.partial(_wide_kernel, mode=mode),
        grid=(bsz, wdt // wt),
        in_specs=in_specs,
        out_specs=pl.BlockSpec((None, r, wt), lambda b, j: (b, 0, j)),
        out_shape=jax.ShapeDtypeStruct((bsz, r, wdt), out_dtype),
        compiler_params=_cparams(("parallel", "parallel")),
        name=name,
    )(*args)


def _conv_mid_kernel(a_ref, k_ref, gfh_ref, gfl_ref, gbh_ref, gbl_ref, o_ref):
    n2 = DFT_N2
    x = _dot3_left(gfh_ref[...], gfl_ref[...], a_ref[...])
    xr, xi = x[:n2], x[n2:]
    kr, ki = k_ref[:n2, :], k_ref[n2:, :]
    p = jnp.concatenate([xr * kr - xi * ki, xr * ki + xi * kr], axis=0)
    o_ref[...] = _dot3_left(gbh_ref[...], gbl_ref[...], p)


def _conv_mid(a, kf, gf, gb, ct):
    bsz, k1p, n2x2, c = a.shape
    gfh, gfl = _split_const(gf)
    gbh, gbl = _split_const(gb)
    gspec = pl.BlockSpec((None, n2x2, n2x2), lambda k, j, b: (k, 0, 0))
    return pl.pallas_call(
        _conv_mid_kernel,
        grid=(k1p, c // ct, bsz),
        in_specs=[
            pl.BlockSpec((None, None, n2x2, ct), lambda k, j, b: (b, k, 0, j)),
            pl.BlockSpec((None, n2x2, ct), lambda k, j, b: (k, 0, j)),
            gspec, gspec, gspec, gspec,
        ],
        out_specs=pl.BlockSpec((None, None, n2x2, ct), lambda k, j, b: (b, k, 0, j)),
        out_shape=jax.ShapeDtypeStruct(a.shape, F32),
        compiler_params=_cparams(("parallel", "parallel", "parallel")),
        name="hyena_conv_mid",
    )(a, kf, gfh, gfl, gbh, gbl)


def _filt_mid_kernel(a_ref, gfh_ref, gfl_ref, o_ref):
    n2 = DFT_N2
    c = HYENA_WIDTH
    for order in range(2):
        fwd = _dot3_left(gfh_ref[...], gfl_ref[...], a_ref[:, order * c:(order + 1) * c])
        bwd = _dot3_left(gfh_ref[...], gfl_ref[...], a_ref[:, (2 + order) * c:(3 + order) * c])
        o_ref[order, :n2, :] = fwd[:n2] + bwd[:n2]
        o_ref[order, n2:, :] = fwd[n2:] - bwd[n2:]


def _filt_mid(a, gf):
    k1p, n2x2, c4 = a.shape
    gfh, gfl = _split_const(gf)
    gspec = pl.BlockSpec((None, n2x2, n2x2), lambda k: (k, 0, 0))
    return pl.pallas_call(
        _filt_mid_kernel,
        grid=(k1p,),
        in_specs=[pl.BlockSpec((None, n2x2, c4), lambda k: (k, 0, 0)), gspec, gspec],
        out_specs=pl.BlockSpec((2, None, n2x2, HYENA_WIDTH), lambda k: (0, k, 0, 0)),
        out_shape=jax.ShapeDtypeStruct((2, k1p, n2x2, HYENA_WIDTH), F32),
        compiler_params=_cparams(("parallel",)),
        name="hyena_filter_spectrum",
    )(a, gfh, gfl)


def _conv_tables(seq):
    n = 2 * seq
    n2 = DFT_N2
    n1 = n // n2
    k1 = n1 // 2 + 1
    k1p = -(-k1 // 8) * 8
    half = n1 // 2
    kk = jnp.arange(k1p, dtype=jnp.int32)
    valid = (kk < k1)
    t1 = jnp.arange(half, dtype=jnp.int32)
    c, s = _cs(kk[:, None] * t1[None, :], n1)
    vm = valid[:, None].astype(F32)
    f1 = jnp.stack([c * vm, -s * vm], axis=1).reshape(2 * k1p, half)
    wgt = jnp.where((kk == 0) | (kk == n1 // 2), 1.0, 2.0) * valid.astype(F32) / n
    g3 = jnp.stack([c * vm * wgt[:, None], -s * vm * wgt[:, None]], axis=1).reshape(2 * k1p, half).T
    t2 = jnp.arange(n2, dtype=jnp.int32)
    k2 = jnp.arange(n2, dtype=jnp.int32)
    idx = t2[None, None, :] * (kk[:, None, None] + n1 * k2[None, :, None])
    gr, gs = _cs(idx, n)
    gi = -gs
    v3 = valid[:, None, None].astype(F32)
    gr, gi = gr * v3, gi * v3
    gf = jnp.concatenate([jnp.concatenate([gr, -gi], axis=2), jnp.concatenate([gi, gr], axis=2)], axis=1)
    hr = jnp.swapaxes(gr, 1, 2)
    hi = -jnp.swapaxes(gi, 1, 2)
    gb = jnp.concatenate([jnp.concatenate([hr, -hi], axis=2), jnp.concatenate([hi, hr], axis=2)], axis=1)
    return dict(n1=n1, k1p=k1p, half=half, f1=f1, g3=g3, gf=gf, gb=gb)


def _hyena_batch(proj, row0, nbatch, seq, tabs, kf, short_w, short_b, skip):
    c = HYENA_WIDTH
    n2 = DFT_N2
    half, k1p = tabs["half"], tabs["k1p"]
    uc = _short_conv(proj, short_w, short_b, row0, nbatch, seq)
    view = lambda a: a.reshape(nbatch, half, n2 * c)
    x1 = view(uc[:, :c])
    x2 = view(uc[:, c:2 * c])
    v = view(uc[:, 2 * c:])
    z = v
    gates = (x1, x2)
    for order in range(2):
        a = _wide_matmul(tabs["f1"], z, name="hyena_fwd_outer")
        a = a.reshape(nbatch, k1p, 2 * n2, c)
        bm = _conv_mid(a, kf[order], tabs["gf"], tabs["gb"], ct=512)
        bm = bm.reshape(nbatch, 2 * k1p, n2 * c)
        skip_row = jnp.tile(skip[order].astype(F32), n2).reshape(1, n2 * c)
        z = _wide_matmul(tabs["g3"], bm, extra=(z, gates[order], skip_row),
                         out_dtype=F32 if order == 0 else BF16, name="hyena_inv_outer")
    return z.reshape(nbatch * seq, c)


def _filter_spectrum(seq, tabs, fw):
    filt = _hyena_filters(seq, *fw)
    n2 = DFT_N2
    a = _wide_matmul(tabs["f1"], filt.reshape(1, tabs["half"], n2 * FILTER_OUT), name="hyena_filter_outer")
    a = a.reshape(tabs["k1p"], 2 * n2, FILTER_OUT)
    return _filt_mid(a, tabs["gf"])


def _fnet_chan_kernel(u_ref, ch_ref, cl_ref, sh_ref, sl_ref, o_ref):
    for hd in range(FNET_HEADS):
        sl = slice(hd * HEAD_DIM, (hd + 1) * HEAD_DIM)
        x_hi, x_lo = _split(u_ref[:, sl])
        o_ref[0, :, sl] = _dot(x_hi, ch_ref[...]) + (_dot(x_lo, ch_ref[...]) + _dot(x_hi, cl_ref[...]))
        o_ref[1, :, sl] = _dot(x_hi, sh_ref[...]) + (_dot(x_lo, sh_ref[...]) + _dot(x_hi, sl_ref[...]))


def _fnet_chan(proj, row0, nbatch, seq):
    tr = min(seq, 512)
    assert row0 % tr == 0
    rb0 = row0 // tr
    nrb = seq // tr
    j = jnp.arange(HEAD_DIM, dtype=jnp.int32)
    c, s = _cs(j[:, None] * j[None, :], HEAD_DIM)
    scale = (seq * HEAD_DIM) ** -0.5
    ch, cl = _split_const(c * scale)
    sh, sl = _split_const(-s * scale)
    mspec = pl.BlockSpec((HEAD_DIM, HEAD_DIM), lambda b, i: (0, 0))
    return pl.pallas_call(
        _fnet_chan_kernel,
        grid=(nbatch, nrb),
        in_specs=[pl.BlockSpec((tr, FNET_WIDTH), lambda b, i: (rb0 + b * nrb + i, OFF_FNET // FNET_WIDTH)),
                  mspec, mspec, mspec, mspec],
        out_specs=pl.BlockSpec((None, 2, tr, FNET_WIDTH), lambda b, i: (b, 0, i, 0)),
        out_shape=jax.ShapeDtypeStruct((nbatch, 2, seq, FNET_WIDTH), F32),
        compiler_params=_cparams(("parallel", "parallel")),
        name="fnet_channel_dft",
    )(proj, ch, cl, sh, sl)


def _fnet_mid_kernel(a_ref, gh_ref, gl_ref, o_ref):
    o_ref[...] = _dot3_left(gh_ref[...], gl_ref[...], a_ref[...]).astype(o_ref.dtype)


def _fnet_mid(a, g, ct):
    bsz, n1, n2x2, c = a.shape
    n2 = n2x2 // 2
    gh, gl = _split_const(g)
    gspec = pl.BlockSpec((None, n2, n2x2), lambda k, j, b: (k, 0, 0))
    return pl.pallas_call(
        _fnet_mid_kernel,
        grid=(n1, c // ct, bsz),
        in_specs=[pl.BlockSpec((None, None, n2x2, ct), lambda k, j, b: (b, k, 0, j)), gspec, gspec],
        out_specs=pl.BlockSpec((None, None, n2, ct), lambda k, j, b: (b, k, 0, j)),
        out_shape=jax.ShapeDtypeStruct((bsz, n1, n2, c), BF16),
        compiler_params=_cparams(("parallel", "parallel", "parallel")),
        name="fnet_inner_dft",
    )(a, gh, gl)


def _fnet_tables(seq):
    n2 = DFT_N2
    n1 = seq // n2
    kk = jnp.arange(n1, dtype=jnp.int32)
    c, s = _cs(kk[:, None] * kk[None, :], n1)
    top = jnp.concatenate([c, s], axis=1)
    bot = jnp.concatenate([-s, c], axis=1)
    m1 = jnp.stack([top, bot], axis=1).reshape(2 * n1, 2 * n1)
    t2 = jnp.arange(n2, dtype=jnp.int32)
    k2 = jnp.arange(n2, dtype=jnp.int32)
    idx = t2[None, None, :] * (kk[:, None, None] + n1 * k2[None, :, None])
    gr, gs = _cs(idx, seq)
    g = jnp.concatenate([gr, gs], axis=2)
    return dict(n1=n1, m1=m1, g=g)


def _fnet_batch(proj, row0, nbatch, seq, tabs):
    n2 = DFT_N2
    n1 = tabs["n1"]
    c = FNET_WIDTH
    z = _fnet_chan(proj, row0, nbatch, seq)
    a = _wide_matmul(tabs["m1"], z.reshape(nbatch, 2 * n1, n2 * c), name="fnet_outer_dft")
    y = _fnet_mid(a.reshape(nbatch, n1, 2 * n2, c), tabs["g"], ct=512)
    return jnp.swapaxes(y, 1, 2).reshape(nbatch * seq, c)


def _pick_tile(t, pref):
    while t % pref:
        pref //= 2
    return pref


def kernel(x_prompt, x_sample, ln0_g, ln0_b, w_in, short_w, short_b, filt_w1, filt_b1, filt_w2, filt_b2, filt_w3, filt_b3, filt_freq, filt_w4, hyena_skip, w_fnet, b_fnet, attn_sink, w_out, ln1_g, ln1_b, w_gate, w_up, w_down, ln2_g, ln2_b):
    bp, lp, _ = x_prompt.shape
    bs, ls, _ = x_sample.shape
    tp, ts = bp * lp, bs * ls
    t = tp + ts
    batches = ((0, bp, lp), (tp, bs, ls))
    tm = _pick_tile(math.gcd(tp, ts), 1024)
    tln = _pick_tile(math.gcd(tp, ts), 256)

    conv_tabs = {seq: _conv_tables(seq) for seq in {lp, ls}}
    fnet_tabs = {seq: _fnet_tables(seq) for seq in {lp, ls}}

    xf, xb = _ln0(x_prompt.reshape(tp, D_MODEL), x_sample.reshape(ts, D_MODEL), ln0_g, ln0_b, tln)
    pad_ff = D_FF_PAD - D_FF
    for l in range(DEPTH):
        w_in_b = w_in[l].astype(BF16)
        w_out_b = w_out[l].astype(BF16)
        w_fnet_b = w_fnet[l].astype(BF16)
        wg_b = jnp.pad(w_gate[l].astype(BF16), ((0, 0), (0, pad_ff)))
        wu_b = jnp.pad(w_up[l].astype(BF16), ((0, 0), (0, pad_ff)))
        wd_b = jnp.pad(w_down[l].astype(BF16), ((0, pad_ff), (0, 0)))
        fw = (filt_w1[l], filt_b1[l], filt_w2[l], filt_b2[l], filt_w3[l], filt_b3[l], filt_freq[l], filt_w4[l])

        proj = _matmul([xb], w_in_b, None, F32, tm, _pick_tile(IN_WIDTH, 1024), "in_proj")

        kf = {seq: _filter_spectrum(seq, conv_tabs[seq], fw) for seq in {lp, ls}}
        y_h = jnp.concatenate(
            [_hyena_batch(proj, r0, nb, seq, conv_tabs[seq], kf[seq], short_w[l], short_b[l], hyena_skip[l])
             for r0, nb, seq in batches], axis=0)
        y_f = jnp.concatenate(
            [_fnet_batch(proj, r0, nb, seq, fnet_tabs[seq]) for r0, nb, seq in batches], axis=0)
        y_f = _matmul([y_f], w_fnet_b, b_fnet[l], BF16, tm, 1024, "fnet_linear")
        y_a = _attention(proj, attn_sink[l], tp, lp, ls)

        mix = _matmul([y_h, y_f, y_a], w_out_b, None, F32, tm, 1024, "out_proj")
        xf, xb = _res_ln(xf, mix, ln1_g[l], ln1_b[l], tln)

        hid = _gate_up(xb, wg_b, wu_b, tm, 512)
        ff = _matmul([hid], wd_b, None, F32, _pick_tile(tm, 512), 512, "ffn_down")
        if l + 1 < DEPTH:
            xf, xb = _res_ln(xf, ff, ln2_g[l], ln2_b[l], tln)
        else:
            y_p, y_s = _res_ln_final(xf, ff, ln2_g[l], ln2_b[l], tln, tp)
    return y_p.reshape(bp, lp, D_MODEL), y_s.reshape(bs, ls, D_MODEL)
```

```python
import functools
import math

import jax
import jax.numpy as jnp
from jax import lax
from jax.experimental import pallas as pl
from jax.experimental.pallas import tpu as pltpu

F32 = jnp.float32
BF16 = jnp.bfloat16

D_MODEL = 4096
HEAD_DIM = 128
HYENA_WIDTH = 1024
FNET_WIDTH = 1024
ATTN_WIDTH = 2048
FNET_HEADS = 8
N_HEADS = 16
N_KV_HEADS = 4
KV_GROUP = 4
KV_WIDTH = 512
BLOCK = 128
HYENA_IN = 3 * HYENA_WIDTH
FILTER_EMB = 33
FILTER_HIDDEN = 64
FILTER_OUT = 4 * HYENA_WIDTH
OFF_FNET = HYENA_IN
OFF_Q = OFF_FNET + FNET_WIDTH
OFF_K = OFF_Q + ATTN_WIDTH
OFF_V = OFF_K + KV_WIDTH
IN_WIDTH = OFF_V + KV_WIDTH
D_FF = 11008
D_FF_PAD = 11264
DEPTH = 2
ALPHA = (2 * DEPTH) ** 0.25
LN_EPS = 1e-5
DECAY_FAST_PCT = 0.3
DECAY_SLOW_PCT = 1.5
DECAY_TARGET = 1e-2

DFT_N2 = 128
LANES = 128
T2_GROUP = 8
PITCH = 136
CONV_PASSES = 1
SPEC_PASSES = 1
INNER_GROUP = 4
ATTN_TQ = 512
VMEM_LIMIT = 56 * 1024 * 1024


def _cparams(sem, vmem=VMEM_LIMIT):
    return pltpu.CompilerParams(dimension_semantics=sem, vmem_limit_bytes=vmem)


def _dot(a, b):
    return jnp.dot(a, b, preferred_element_type=F32)


def _split(x):
    hi = x.astype(BF16)
    lo = (x - hi.astype(F32)).astype(BF16)
    return hi, lo


def _dot3_left(m_hi, m_lo, x):
    x_hi, x_lo = _split(x)
    return _dot(m_hi, x_hi) + (_dot(m_hi, x_lo) + _dot(m_lo, x_hi))


def _split_const(m):
    m = m.astype(F32)
    hi = m.astype(BF16)
    lo = (m - hi.astype(F32)).astype(BF16)
    return hi, lo


def _cs(num, den):
    ang = (2.0 * math.pi / den) * (num % den).astype(F32)
    return jnp.cos(ang), jnp.sin(ang)


def _full_spec(a):
    return pl.BlockSpec(a.shape, lambda *_: (0,) * a.ndim)


def _ln_math(x, g, b):
    mu = jnp.mean(x, axis=-1, keepdims=True)
    xc = x - mu
    var = jnp.mean(xc * xc, axis=-1, keepdims=True)
    return xc * lax.rsqrt(var + LN_EPS) * g + b


def _ln0_kernel(xp_ref, xs_ref, g_ref, b_ref, of_ref, ob_ref, *, n_p):
    i = pl.program_id(0)

    def emit(x):
        y = _ln_math(x, g_ref[...], b_ref[...])
        of_ref[...] = y
        ob_ref[...] = y.astype(BF16)

    @pl.when(i < n_p)
    def _():
        emit(xp_ref[...])

    @pl.when(i >= n_p)
    def _():
        emit(xs_ref[...])


def _ln0(xp, xs, g, b, tm):
    tp, ts = xp.shape[0], xs.shape[0]
    n_p, n_s = tp // tm, ts // tm
    t = tp + ts
    return pl.pallas_call(
        functools.partial(_ln0_kernel, n_p=n_p),
        grid=(n_p + n_s,),
        in_specs=[
            pl.BlockSpec((tm, D_MODEL), lambda i: (jnp.minimum(i, n_p - 1), 0)),
            pl.BlockSpec((tm, D_MODEL), lambda i: (jnp.maximum(i - n_p, 0), 0)),
            pl.BlockSpec((1, D_MODEL), lambda i: (0, 0)),
            pl.BlockSpec((1, D_MODEL), lambda i: (0, 0)),
        ],
        out_specs=[
            pl.BlockSpec((tm, D_MODEL), lambda i: (i, 0)),
            pl.BlockSpec((tm, D_MODEL), lambda i: (i, 0)),
        ],
        out_shape=[jax.ShapeDtypeStruct((t, D_MODEL), F32), jax.ShapeDtypeStruct((t, D_MODEL), BF16)],
        compiler_params=_cparams(("parallel",)),
        name="ln0",
    )(xp, xs, g.reshape(1, -1), b.reshape(1, -1))


def _ln_kernel(y_ref, g_ref, b_ref, of_ref, ob_ref):
    y = _ln_math(y_ref[...], g_ref[...], b_ref[...])
    of_ref[...] = y
    ob_ref[...] = y.astype(BF16)


def _ln(y, g, b, tm):
    t = y.shape[0]
    row = pl.BlockSpec((tm, D_MODEL), lambda i: (i, 0))
    vec = pl.BlockSpec((1, D_MODEL), lambda i: (0, 0))
    return pl.pallas_call(
        _ln_kernel,
        grid=(t // tm,),
        in_specs=[row, vec, vec],
        out_specs=[row, row],
        out_shape=[jax.ShapeDtypeStruct((t, D_MODEL), F32), jax.ShapeDtypeStruct((t, D_MODEL), BF16)],
        compiler_params=_cparams(("parallel",)),
        name="layer_norm",
    )(y, g.reshape(1, -1), b.reshape(1, -1))


def _ln_final_kernel(y_ref, g_ref, b_ref, op_ref, os_ref, *, n_p):
    i = pl.program_id(0)
    y = _ln_math(y_ref[...], g_ref[...], b_ref[...])

    @pl.when(i < n_p)
    def _():
        op_ref[...] = y

    @pl.when(i >= n_p)
    def _():
        os_ref[...] = y


def _ln_final(y, g, b, tm, tp):
    t = y.shape[0]
    n_p = tp // tm
    row = pl.BlockSpec((tm, D_MODEL), lambda i: (i, 0))
    vec = pl.BlockSpec((1, D_MODEL), lambda i: (0, 0))
    return pl.pallas_call(
        functools.partial(_ln_final_kernel, n_p=n_p),
        grid=(t // tm,),
        in_specs=[row, vec, vec],
        out_specs=[
            pl.BlockSpec((tm, D_MODEL), lambda i: (jnp.minimum(i, n_p - 1), 0)),
            pl.BlockSpec((tm, D_MODEL), lambda i: (jnp.maximum(i - n_p, 0), 0)),
        ],
        out_shape=[jax.ShapeDtypeStruct((tp, D_MODEL), F32), jax.ShapeDtypeStruct((t - tp, D_MODEL), F32)],
        compiler_params=_cparams(("arbitrary",)),
        name="layer_norm_final",
    )(y, g.reshape(1, -1), b.reshape(1, -1))


def _mm_kernel(*refs, widths, has_bias, has_resid):
    n_a = len(widths)
    a_refs = refs[:n_a]
    w_ref = refs[n_a]
    o_ref = refs[-1]
    nxt = n_a + 1
    acc = None
    off = 0
    for a_ref, wd in zip(a_refs, widths):
        part = _dot(a_ref[...].astype(BF16), w_ref[off:off + wd, :])
        acc = part if acc is None else acc + part
        off += wd
    if has_bias:
        acc = acc + refs[nxt][...]
        nxt += 1
    if has_resid:
        acc = ALPHA * refs[nxt][...] + acc
    o_ref[...] = acc.astype(o_ref.dtype)


def _matmul(a_list, w, bias, out_dtype, tm, tn, name, resid=None):
    t = a_list[0].shape[0]
    k, n = w.shape
    widths = tuple(a.shape[1] for a in a_list)
    assert sum(widths) == k and t % tm == 0 and n % tn == 0
    in_specs = [pl.BlockSpec((tm, wd), lambda i, j: (i, 0)) for wd in widths]
    in_specs.append(pl.BlockSpec((k, tn), lambda i, j: (0, j)))
    args = list(a_list) + [w]
    if bias is not None:
        in_specs.append(pl.BlockSpec((1, tn), lambda i, j: (0, j)))
        args.append(bias.reshape(1, n).astype(F32))
    if resid is not None:
        in_specs.append(pl.BlockSpec((tm, tn), lambda i, j: (i, j)))
        args.append(resid)
    return pl.pallas_call(
        functools.partial(_mm_kernel, widths=widths, has_bias=bias is not None, has_resid=resid is not None),
        grid=(t // tm, n // tn),
        in_specs=in_specs,
        out_specs=pl.BlockSpec((tm, tn), lambda i, j: (i, j)),
        out_shape=jax.ShapeDtypeStruct((t, n), out_dtype),
        compiler_params=_cparams(("parallel", "parallel")),
        name=name,
    )(*args)


def _gate_up_kernel(x_ref, wg_ref, wu_ref, o_ref):
    x = x_ref[...]
    g = _dot(x, wg_ref[...])
    u = _dot(x, wu_ref[...])
    o_ref[...] = (g * (1.0 / (1.0 + jnp.exp(-g))) * u).astype(o_ref.dtype)


def _gate_up(x, wg, wu, tm, tn):
    t, k = x.shape
    n = wg.shape[1]
    return pl.pallas_call(
        _gate_up_kernel,
        grid=(t // tm, n // tn),
        in_specs=[
            pl.BlockSpec((tm, k), lambda i, j: (i, 0)),
            pl.BlockSpec((k, tn), lambda i, j: (0, j)),
            pl.BlockSpec((k, tn), lambda i, j: (0, j)),
        ],
        out_specs=pl.BlockSpec((tm, tn), lambda i, j: (i, j)),
        out_shape=jax.ShapeDtypeStruct((t, n), BF16),
        compiler_params=_cparams(("parallel", "parallel")),
        name="ffn_gate_up",
    )(x, wg, wu)


def _attn_kernel(q_ref, kp_ref, kc_ref, kn_ref, vp_ref, vc_ref, vn_ref, bias_ref, sink_ref, o_ref, kbuf, vbuf,
                 *, tq, tiles_p, per_p, per_s):
    g = pl.program_id(0)
    h = pl.program_id(1)
    in_p = g < tiles_p
    n_loc = jnp.where(in_p, g % per_p, (g - tiles_p) % per_s)
    n_seq = jnp.where(in_p, per_p, per_s)
    pen_prev = jnp.where(n_loc > 0, 0.0, -jnp.inf)
    pen_next = jnp.where(n_loc < n_seq - 1, 0.0, -jnp.inf)

    kbuf[0:BLOCK, :] = kp_ref[...].astype(BF16)
    kbuf[BLOCK:BLOCK + tq, :] = kc_ref[...].astype(BF16)
    kbuf[BLOCK + tq:, :] = kn_ref[...].astype(BF16)
    vbuf[0:BLOCK, :] = vp_ref[...].astype(BF16)
    vbuf[BLOCK:BLOCK + tq, :] = vc_ref[...].astype(BF16)
    vbuf[BLOCK + tq:, :] = vn_ref[...].astype(BF16)

    lane = lax.broadcasted_iota(jnp.int32, (1, 3 * BLOCK), 1)
    row_prev = jnp.where(lane < BLOCK, pen_prev, 0.0)
    row_next = jnp.where(lane >= 2 * BLOCK, pen_next, 0.0)
    sink_col = jnp.concatenate(
        [jnp.full((BLOCK, 1), sink_ref[h * KV_GROUP + gi], F32) for gi in range(KV_GROUP)], axis=0)
    scale = HEAD_DIM ** -0.5
    dn = (((1,), (1,)), ((), ()))
    nsb = tq // BLOCK
    for sb in range(nsb):
        r0 = sb * BLOCK
        q4 = jnp.concatenate(
            [q_ref[r0:r0 + BLOCK, gi * HEAD_DIM:(gi + 1) * HEAD_DIM].astype(BF16) for gi in range(KV_GROUP)], axis=0)
        s = lax.dot_general(q4, kbuf[r0:r0 + 3 * BLOCK, :], dn, preferred_element_type=F32) * scale + bias_ref[...]
        if sb == 0:
            s = s + row_prev
        if sb == nsb - 1:
            s = s + row_next
        m = jnp.maximum(jnp.max(s, axis=-1, keepdims=True), sink_col)
        p = jnp.exp(s - m)
        denom = jnp.sum(p, axis=-1, keepdims=True) + jnp.exp(sink_col - m)
        o = _dot(p.astype(BF16), vbuf[r0:r0 + 3 * BLOCK, :]) * (1.0 / denom)
        for gi in range(KV_GROUP):
            o_ref[r0:r0 + BLOCK, gi * HEAD_DIM:(gi + 1) * HEAD_DIM] = (
                o[gi * BLOCK:(gi + 1) * BLOCK, :].astype(o_ref.dtype))


def _attention(proj, sink, tp, lp, ls):
    t = proj.shape[0]
    tq = math.gcd(ATTN_TQ, math.gcd(lp, ls))
    nblk = t // BLOCK
    bpt = tq // BLOCK
    slopes = 2.0 ** (-8.0 * jnp.arange(1, N_HEADS + 1, dtype=F32) / N_HEADS)
    qi = jnp.arange(BLOCK)[:, None]
    ki = jnp.arange(3 * BLOCK)[None, :]
    dist = jnp.abs(qi + BLOCK - ki)
    bias = jnp.where(dist[None] <= BLOCK, -slopes[:, None, None] * dist[None].astype(F32), -jnp.inf)
    bias = bias.reshape(N_HEADS * BLOCK, 3 * BLOCK)
    qc = OFF_Q // (KV_GROUP * HEAD_DIM)
    kc = OFF_K // HEAD_DIM
    vc = OFF_V // HEAD_DIM
    prev = lambda g: jnp.maximum(g * bpt - 1, 0)
    nxt = lambda g: jnp.minimum((g + 1) * bpt, nblk - 1)
    halo = (BLOCK, HEAD_DIM)
    cur = (tq, HEAD_DIM)
    return pl.pallas_call(
        functools.partial(_attn_kernel, tq=tq, tiles_p=tp // tq, per_p=lp // tq, per_s=ls // tq),
        grid=(t // tq, N_KV_HEADS),
        in_specs=[
            pl.BlockSpec((tq, KV_GROUP * HEAD_DIM), lambda g, h: (g, qc + h)),
            pl.BlockSpec(halo, lambda g, h: (prev(g), kc + h)),
            pl.BlockSpec(cur, lambda g, h: (g, kc + h)),
            pl.BlockSpec(halo, lambda g, h: (nxt(g), kc + h)),
            pl.BlockSpec(halo, lambda g, h: (prev(g), vc + h)),
            pl.BlockSpec(cur, lambda g, h: (g, vc + h)),
            pl.BlockSpec(halo, lambda g, h: (nxt(g), vc + h)),
            pl.BlockSpec((KV_GROUP * BLOCK, 3 * BLOCK), lambda g, h: (h, 0)),
            pl.BlockSpec(memory_space=pltpu.SMEM),
        ],
        out_specs=pl.BlockSpec((tq, KV_GROUP * HEAD_DIM), lambda g, h: (g, h)),
        out_shape=jax.ShapeDtypeStruct((t, ATTN_WIDTH), BF16),
        scratch_shapes=[pltpu.VMEM((tq + 2 * BLOCK, HEAD_DIM), BF16), pltpu.VMEM((tq + 2 * BLOCK, HEAD_DIM), BF16)],
        compiler_params=_cparams(("parallel", "parallel")),
        name="band_attention",
    )(proj, proj, proj, proj, proj, proj, proj, bias, sink.astype(F32))


def _filter_kernel(fvec_ref, w1_ref, b1_ref, w2_ref, b2_ref, w3_ref, b3_ref, fr_ref, w4_ref, delta_ref, o_ref,
                   *, seq, tl):
    r0 = pl.program_id(0) * tl
    row = (lax.broadcasted_iota(jnp.int32, (tl, 128), 0) + r0).astype(F32)
    lane = lax.broadcasted_iota(jnp.int32, (tl, 128), 1)
    t = row * (1.0 / (seq - 1))
    ang = (row * (2.0 * math.pi / seq)) * fvec_ref[...]
    z = jnp.where(lane == 0, t,
                  jnp.where(lane <= 16, jnp.cos(ang), jnp.where(lane <= 32, -jnp.sin(ang), 0.0)))
    hp = lax.Precision.HIGHEST
    h = jnp.sin(fr_ref[0:1, :] * (jnp.dot(z, w1_ref[...], precision=hp, preferred_element_type=F32) + b1_ref[...]))
    h = jnp.sin(fr_ref[1:2, :] * (jnp.dot(h, w2_ref[...], precision=hp, preferred_element_type=F32) + b2_ref[...]))
    h = jnp.sin(fr_ref[2:3, :] * (jnp.dot(h, w3_ref[...], precision=hp, preferred_element_type=F32) + b3_ref[...]))
    trow = (lax.broadcasted_iota(jnp.int32, (tl, HYENA_WIDTH), 0) + r0)
    decay = jnp.exp(-(trow.astype(F32) * (1.0 / (seq - 1))) * delta_ref[...])
    first = trow == 0
    for part in range(4):
        sl = slice(part * HYENA_WIDTH, (part + 1) * HYENA_WIDTH)
        v = jnp.dot(h, w4_ref[:, sl], precision=hp, preferred_element_type=F32) * decay
        if part >= 2:
            v = jnp.where(first, 0.0, v)
        o_ref[:, sl] = v


def _hyena_filters(seq, w1, b1, w2, b2, w3, b3, freq, w4):
    tl = min(seq, 512)
    bands = (FILTER_EMB - 1) // 2
    f = jnp.linspace(1e-4, bands - 1, bands, dtype=F32)
    fvec = jnp.zeros((1, 128), F32).at[0, 1:1 + bands].set(f).at[0, 1 + bands:1 + 2 * bands].set(f)
    w1p = jnp.zeros((128, FILTER_HIDDEN), F32).at[:FILTER_EMB].set(w1.astype(F32))
    max_decay = math.log(DECAY_TARGET) / DECAY_FAST_PCT
    min_decay = math.log(DECAY_TARGET) / DECAY_SLOW_PCT
    deltas = jnp.abs(jnp.linspace(min_decay, max_decay, HYENA_WIDTH, dtype=F32)).reshape(1, -1)
    args = [fvec, w1p, b1.reshape(1, -1).astype(F32), w2.astype(F32), b2.reshape(1, -1).astype(F32),
            w3.astype(F32), b3.reshape(1, -1).astype(F32), freq.astype(F32), w4.astype(F32), deltas]
    return pl.pallas_call(
        functools.partial(_filter_kernel, seq=seq, tl=tl),
        grid=(seq // tl,),
        in_specs=[_full_spec(a) for a in args],
        out_specs=pl.BlockSpec((tl, FILTER_OUT), lambda i: (i, 0)),
        out_shape=jax.ShapeDtypeStruct((seq, FILTER_OUT), F32),
        compiler_params=_cparams(("parallel",)),
        name="hyena_filter",
    )(*args)


def _dotp(mh_ref, ml_ref, x, passes):
    if passes == 1:
        return _dot(mh_ref[...], x.astype(BF16))
    return _dot3_left(mh_ref[...], ml_ref[...], x)


def _gather(ref, t2, n):
    return ref[pl.ds(t2, n, stride=PITCH), :]


def _fill_pitched(u_ref, p_ref, half, w_ref=None, b_ref=None):
    n2 = DFT_N2
    seq = half * n2
    rid = lax.broadcasted_iota(jnp.int32, (n2, LANES), 0)

    def step(t1, carry):
        r0 = pl.multiple_of(t1 * n2, n2)
        cur = u_ref[pl.ds(r0, n2), :]
        if w_ref is not None:
            before = u_ref[pl.ds(pl.multiple_of(jnp.maximum(r0 - 8, 0), 8), 8), :][7:8, :]
            after = u_ref[pl.ds(pl.multiple_of(jnp.minimum(r0 + n2, seq - 8), 8), 8), :][0:1, :]
            before = jnp.where(t1 > 0, before, 0.0)
            after = jnp.where(t1 < half - 1, after, 0.0)
            prev = jnp.where(rid == 0, before, pltpu.roll(cur, 1, 0))
            nxt = jnp.where(rid == n2 - 1, after, pltpu.roll(cur, n2 - 1, 0))
            cur = prev * w_ref[0:1, :] + cur * w_ref[1:2, :] + nxt * w_ref[2:3, :] + b_ref[...]
        p_ref[pl.ds(pl.multiple_of(t1 * PITCH, 8), n2), :] = cur
        return carry

    lax.fori_loop(0, half, step, 0)


def _for_t2_groups(body):
    def step(g, carry):
        body(g * T2_GROUP)
        return carry

    lax.fori_loop(0, DFT_N2 // T2_GROUP, step, 0)


def _outer_stage(p_ref, half, mh_ref, ml_ref, a_ref, nrows, passes):
    def body(base):
        xs = jnp.concatenate([_gather(p_ref, base + j, half) for j in range(T2_GROUP)], axis=1)
        y = _dotp(mh_ref, ml_ref, xs, passes)
        for j in range(T2_GROUP):
            a_ref[pl.ds(base + j, nrows, stride=PITCH), :] = y[:, j * LANES:(j + 1) * LANES]

    _for_t2_groups(body)


def _inner_blocks(a_ref, tw_ref, wstep_ref, nblocks, compute_fn, store_fn):
    n2 = DFT_N2
    tw_ref[0:n2, :] = jnp.ones((n2, LANES), F32)
    tw_ref[n2:, :] = jnp.zeros((n2, LANES), F32)

    def run(k0, count):
        twr = tw_ref[0:n2, :]
        twi = tw_ref[n2:, :]
        wr = wstep_ref[0:n2, :]
        wi = wstep_ref[n2:, :]
        done = []
        for j in range(count):
            k = k0 + j
            rr = pl.multiple_of(k * 2 * PITCH, 8)
            ri = pl.multiple_of(k * 2 * PITCH + PITCH, 8)
            ar = a_ref[pl.ds(rr, n2), :]
            ai = a_ref[pl.ds(ri, n2), :]
            done.append((k, rr, ri, compute_fn(k, ar * twr - ai * twi, ar * twi + ai * twr, twr, twi)))
            twr, twi = twr * wr - twi * wi, twr * wi + twi * wr
        for k, rr, ri, vals in done:
            store_fn(k, rr, ri, vals)
        tw_ref[0:n2, :] = twr
        tw_ref[n2:, :] = twi

    def step(g, carry):
        run(g * INNER_GROUP, INNER_GROUP)
        return carry

    lax.fori_loop(0, nblocks // INNER_GROUP, step, 0)
    if nblocks % INNER_GROUP:
        run(nblocks - nblocks % INNER_GROUP, nblocks % INNER_GROUP)


def _hyena_conv_kernel(sig_ref, gate_ref, wsig_ref, bsig_ref, wgate_ref, bgate_ref, skip_ref, kf_ref,
                       f1h_ref, f1l_ref, g3h_ref, g3l_ref, gch_ref, gcl_ref, gbh_ref, gbl_ref, wstep_ref,
                       o_ref, a_ref, tw_ref, pv_ref, pg_ref, *, half, k1, k1p, sig_conv):
    n2 = DFT_N2
    if sig_conv:
        _fill_pitched(sig_ref, pv_ref, half, wsig_ref, bsig_ref)
    else:
        _fill_pitched(sig_ref, pv_ref, half)
    _fill_pitched(gate_ref, pg_ref, half, wgate_ref, bgate_ref)
    _outer_stage(pv_ref, half, f1h_ref, f1l_ref, a_ref, 2 * k1p, CONV_PASSES)

    def block(k, apr, api, twr, twi):
        x = _dotp(gch_ref, gcl_ref, jnp.concatenate([apr, api], axis=0), CONV_PASSES)
        xr, xi = x[:n2], x[n2:]
        k0 = pl.multiple_of(k * 2 * n2, 2 * n2)
        kr = kf_ref[pl.ds(k0, n2), :]
        ki = kf_ref[pl.ds(k0 + n2, n2), :]
        p = jnp.concatenate([xr * kr - xi * ki, xr * ki + xi * kr], axis=0)
        bb = _dotp(gbh_ref, gbl_ref, p, CONV_PASSES)
        br, bi = bb[:n2], bb[n2:]
        return br * twr + bi * twi, bi * twr - br * twi

    def put(k, rr, ri, vals):
        a_ref[pl.ds(rr, n2), :] = vals[0]
        a_ref[pl.ds(ri, n2), :] = vals[1]

    _inner_blocks(a_ref, tw_ref, wstep_ref, k1, block, put)

    def finish(base):
        bcat = jnp.concatenate([_gather(a_ref, base + j, 2 * k1p) for j in range(T2_GROUP)], axis=1)
        y = _dotp(g3h_ref, g3l_ref, bcat, CONV_PASSES)
        for j in range(T2_GROUP):
            v = _gather(pv_ref, base + j, half)
            gate = _gather(pg_ref, base + j, half)
            pg_ref[pl.ds(base + j, half, stride=PITCH), :] = gate * (y[:, j * LANES:(j + 1) * LANES]
                                                                    + v * skip_ref[...])

    _for_t2_groups(finish)

    def emit(t1, carry):
        o_ref[pl.ds(pl.multiple_of(t1 * n2, n2), n2), :] = pg_ref[pl.ds(pl.multiple_of(t1 * PITCH, 8), n2), :]
        return carry

    lax.fori_loop(0, half, emit, 0)


def _hyena_spec_kernel(ff_ref, fb_ref, f1h_ref, f1l_ref, gch_ref, gcl_ref, wstep_ref, o_ref, a_ref, tw_ref, pf_ref,
                       *, half, k1, k1p):
    n2 = DFT_N2
    _fill_pitched(ff_ref, pf_ref, half)
    _outer_stage(pf_ref, half, f1h_ref, f1l_ref, a_ref, 2 * k1p, SPEC_PASSES)

    def spectrum(k, apr, api, twr, twi):
        return _dotp(gch_ref, gcl_ref, jnp.concatenate([apr, api], axis=0), SPEC_PASSES)

    def put_fwd(k, rr, ri, x):
        o_ref[pl.ds(pl.multiple_of(k * 2 * n2, 2 * n2), 2 * n2), :] = x

    _inner_blocks(a_ref, tw_ref, wstep_ref, k1, spectrum, put_fwd)
    _fill_pitched(fb_ref, pf_ref, half)
    _outer_stage(pf_ref, half, f1h_ref, f1l_ref, a_ref, 2 * k1p, SPEC_PASSES)

    def add_bwd(k, rr, ri, x):
        k0 = pl.multiple_of(k * 2 * n2, 2 * n2)
        o_ref[pl.ds(k0, n2), :] = o_ref[pl.ds(k0, n2), :] + x[:n2]
        o_ref[pl.ds(k0 + n2, n2), :] = o_ref[pl.ds(k0 + n2, n2), :] - x[n2:]

    _inner_blocks(a_ref, tw_ref, wstep_ref, k1, spectrum, add_bwd)


def _conv_tables(seq):
    n = 2 * seq
    n2 = DFT_N2
    n1 = n // n2
    k1 = n1 // 2 + 1
    k1p = -(-k1 // 8) * 8
    half = n1 // 2
    kk = jnp.arange(k1p, dtype=jnp.int32)
    valid = (kk < k1)
    t1 = jnp.arange(half, dtype=jnp.int32)
    c, s = _cs(kk[:, None] * t1[None, :], n1)
    vm = valid[:, None].astype(F32)
    f1 = jnp.stack([c * vm, -s * vm], axis=1).reshape(2 * k1p, half)
    wgt = jnp.where((kk == 0) | (kk == n1 // 2), 1.0, 2.0) * valid.astype(F32) / n
    g3 = jnp.stack([c * vm * wgt[:, None], -s * vm * wgt[:, None]], axis=1).reshape(2 * k1p, half).T
    j = jnp.arange(n2, dtype=jnp.int32)
    cr, cs_ = _cs(j[:, None] * j[None, :], n2)
    gc = jnp.concatenate([jnp.concatenate([cr, cs_], axis=1), jnp.concatenate([-cs_, cr], axis=1)], axis=0)
    gb = jnp.concatenate([jnp.concatenate([cr, -cs_], axis=1), jnp.concatenate([cs_, cr], axis=1)], axis=0)
    wr, ws = _cs(j, n)
    wstep = jnp.concatenate([jnp.broadcast_to(wr[:, None], (n2, LANES)),
                             jnp.broadcast_to(-ws[:, None], (n2, LANES))], axis=0)
    tabs = dict(n1=n1, k1=k1, k1p=k1p, half=half, wstep=wstep)
    for name, m in (("f1", f1), ("g3", g3), ("gc", gc), ("gb", gb)):
        tabs[name + "h"], tabs[name + "l"] = _split_const(m)
    return tabs


def _filter_spectrum(seq, tabs, fw):
    filt = _hyena_filters(seq, *fw)
    half, k1, k1p = tabs["half"], tabs["k1"], tabs["k1p"]
    n2 = DFT_N2
    nct = HYENA_WIDTH // LANES
    consts = [tabs[n] for n in ("f1h", "f1l", "gch", "gcl", "wstep")]
    return pl.pallas_call(
        functools.partial(_hyena_spec_kernel, half=half, k1=k1, k1p=k1p),
        grid=(2, nct),
        in_specs=[pl.BlockSpec((seq, LANES), lambda o, j: (0, o * nct + j)),
                  pl.BlockSpec((seq, LANES), lambda o, j: (0, (2 + o) * nct + j))] + [_full_spec(a) for a in consts],
        out_specs=pl.BlockSpec((None, k1 * 2 * n2, LANES), lambda o, j: (o, 0, j)),
        out_shape=jax.ShapeDtypeStruct((2, k1 * 2 * n2, HYENA_WIDTH), F32),
        scratch_shapes=[pltpu.VMEM((k1p * 2 * PITCH, LANES), F32), pltpu.VMEM((2 * n2, LANES), F32),
                        pltpu.VMEM((half * PITCH, LANES), F32)],
        compiler_params=_cparams(("parallel", "parallel")),
        name="hyena_filter_spectrum",
    )(filt, filt, *consts)


def _hyena_conv(sig, sig_cols, gate_cols, proj, row0, nbatch, seq, tabs, kf, order, short_w, short_b, skip):
    assert row0 % seq == 0
    b0 = row0 // seq
    half, k1, k1p = tabs["half"], tabs["k1"], tabs["k1p"]
    n2 = DFT_N2
    nct = HYENA_WIDTH // LANES
    sig_conv = sig is None
    if sig_conv:
        sig_arr = proj
        sig_spec = pl.BlockSpec((seq, LANES), lambda j, b: (b0 + b, sig_cols + j))
    else:
        sig_arr = sig
        sig_spec = pl.BlockSpec((seq, LANES), lambda j, b: (b, j))
    sw = short_w.astype(F32)
    sb = short_b.reshape(1, -1).astype(F32)
    consts = [tabs[n] for n in ("f1h", "f1l", "g3h", "g3l", "gch", "gcl", "gbh", "gbl", "wstep")]
    return pl.pallas_call(
        functools.partial(_hyena_conv_kernel, half=half, k1=k1, k1p=k1p, sig_conv=sig_conv),
        grid=(nct, nbatch),
        in_specs=[
            sig_spec,
            pl.BlockSpec((seq, LANES), lambda j, b: (b0 + b, gate_cols + j), pipeline_mode=pl.Buffered(1)),
            pl.BlockSpec((3, LANES), lambda j, b: (0, sig_cols + j)),
            pl.BlockSpec((1, LANES), lambda j, b: (0, sig_cols + j)),
            pl.BlockSpec((3, LANES), lambda j, b: (0, gate_cols + j)),
            pl.BlockSpec((1, LANES), lambda j, b: (0, gate_cols + j)),
            pl.BlockSpec((1, LANES), lambda j, b: (0, j)),
            pl.BlockSpec((None, k1 * 2 * n2, LANES), lambda j, b: (order, 0, j), pipeline_mode=pl.Buffered(1)),
        ] + [_full_spec(a) for a in consts],
        out_specs=pl.BlockSpec((seq, LANES), lambda j, b: (b, j)),
        out_shape=jax.ShapeDtypeStruct((nbatch * seq, HYENA_WIDTH), F32),
        scratch_shapes=[pltpu.VMEM((k1p * 2 * PITCH, LANES), F32), pltpu.VMEM((2 * n2, LANES), F32),
                        pltpu.VMEM((half * PITCH, LANES), F32), pltpu.VMEM((half * PITCH, LANES), F32)],
        compiler_params=_cparams(("parallel", "arbitrary")),
        name="hyena_conv",
    )(sig_arr, proj, sw, sb, sw, sb, skip.reshape(1, -1).astype(F32), kf, *consts)


def _hyena_batch(proj, row0, nbatch, seq, tabs, kf, short_w, short_b, skip):
    nct = HYENA_WIDTH // LANES
    z = _hyena_conv(None, 2 * nct, 0, proj, row0, nbatch, seq, tabs, kf, 0, short_w, short_b, skip[0])
    return _hyena_conv(z, 2 * nct, nct, proj, row0, nbatch, seq, tabs, kf, 1, short_w, short_b, skip[1])


def _fnet_chan_kernel(u_ref, ch_ref, cl_ref, sh_ref, sl_ref, o_ref):
    for hd in range(FNET_HEADS):
        sl = slice(hd * HEAD_DIM, (hd + 1) * HEAD_DIM)
        x_hi, x_lo = _split(u_ref[:, sl])
        o_ref[0, :, sl] = _dot(x_hi, ch_ref[...]) + (_dot(x_lo, ch_ref[...]) + _dot(x_hi, cl_ref[...]))
        o_ref[1, :, sl] = _dot(x_hi, sh_ref[...]) + (_dot(x_lo, sh_ref[...]) + _dot(x_hi, sl_ref[...]))


def _fnet_chan(proj, row0, nbatch, seq):
    tr = min(seq, 512)
    assert row0 % tr == 0
    rb0 = row0 // tr
    nrb = seq // tr
    j = jnp.arange(HEAD_DIM, dtype=jnp.int32)
    c, s = _cs(j[:, None] * j[None, :], HEAD_DIM)
    scale = (seq * HEAD_DIM) ** -0.5
    ch, cl = _split_const(c * scale)
    sh, sl = _split_const(-s * scale)
    mspec = pl.BlockSpec((HEAD_DIM, HEAD_DIM), lambda b, i: (0, 0))
    return pl.pallas_call(
        _fnet_chan_kernel,
        grid=(nbatch, nrb),
        in_specs=[pl.BlockSpec((tr, FNET_WIDTH), lambda b, i: (rb0 + b * nrb + i, OFF_FNET // FNET_WIDTH)),
                  mspec, mspec, mspec, mspec],
        out_specs=pl.BlockSpec((None, 2, tr, FNET_WIDTH), lambda b, i: (b, 0, i, 0)),
        out_shape=jax.ShapeDtypeStruct((nbatch, 2, seq, FNET_WIDTH), F32),
        compiler_params=_cparams(("parallel", "parallel")),
        name="fnet_channel_dft",
    )(proj, ch, cl, sh, sl)


def _wide_kernel(mh_ref, ml_ref, x_ref, o_ref):
    o_ref[...] = _dot3_left(mh_ref[...], ml_ref[...], x_ref[...]).astype(o_ref.dtype)


def _wide_matmul(mat, x, name):
    bsz, kd, wdt = x.shape
    r = mat.shape[0]
    wt = min(wdt, 4096)
    mh, ml = _split_const(mat)
    mspec = pl.BlockSpec((r, kd), lambda b, j: (0, 0))
    return pl.pallas_call(
        _wide_kernel,
        grid=(bsz, wdt // wt),
        in_specs=[mspec, mspec, pl.BlockSpec((None, kd, wt), lambda b, j: (b, 0, j))],
        out_specs=pl.BlockSpec((None, r, wt), lambda b, j: (b, 0, j)),
        out_shape=jax.ShapeDtypeStruct((bsz, r, wdt), F32),
        compiler_params=_cparams(("parallel", "parallel")),
        name=name,
    )(mh, ml, x)


def _fnet_mid_kernel(a_ref, gh_ref, gl_ref, o_ref):
    o_ref[...] = _dot3_left(gh_ref[...], gl_ref[...], a_ref[...]).astype(o_ref.dtype)


def _fnet_mid(a, g, ct):
    bsz, n1, n2x2, c = a.shape
    n2 = n2x2 // 2
    gh, gl = _split_const(g)
    gspec = pl.BlockSpec((None, n2, n2x2), lambda k, j, b: (k, 0, 0))
    return pl.pallas_call(
        _fnet_mid_kernel,
        grid=(n1, c // ct, bsz),
        in_specs=[pl.BlockSpec((None, None, n2x2, ct), lambda k, j, b: (b, k, 0, j)), gspec, gspec],
        out_specs=pl.BlockSpec((None, None, n2, ct), lambda k, j, b: (b, k, 0, j)),
        out_shape=jax.ShapeDtypeStruct((bsz, n1, n2, c), BF16),
        compiler_params=_cparams(("parallel", "parallel", "parallel")),
        name="fnet_inner_dft",
    )(a, gh, gl)


def _fnet_tables(seq):
    n2 = DFT_N2
    n1 = seq // n2
    kk = jnp.arange(n1, dtype=jnp.int32)
    c, s = _cs(kk[:, None] * kk[None, :], n1)
    top = jnp.concatenate([c, s], axis=1)
    bot = jnp.concatenate([-s, c], axis=1)
    m1 = jnp.stack([top, bot], axis=1).reshape(2 * n1, 2 * n1)
    t2 = jnp.arange(n2, dtype=jnp.int32)
    k2 = jnp.arange(n2, dtype=jnp.int32)
    idx = t2[None, None, :] * (kk[:, None, None] + n1 * k2[None, :, None])
    gr, gs = _cs(idx, seq)
    g = jnp.concatenate([gr, gs], axis=2)
    return dict(n1=n1, m1=m1, g=g)


def _fnet_batch(proj, row0, nbatch, seq, tabs):
    n2 = DFT_N2
    n1 = tabs["n1"]
    c = FNET_WIDTH
    z = _fnet_chan(proj, row0, nbatch, seq)
    a = _wide_matmul(tabs["m1"], z.reshape(nbatch, 2 * n1, n2 * c), name="fnet_outer_dft")
    y = _fnet_mid(a.reshape(nbatch, n1, 2 * n2, c), tabs["g"], ct=512)
    return jnp.swapaxes(y, 1, 2).reshape(nbatch * seq, c)


def _pick_tile(t, pref):
    while t % pref:
        pref //= 2
    return pref


def kernel(x_prompt, x_sample, ln0_g, ln0_b, w_in, short_w, short_b, filt_w1, filt_b1, filt_w2, filt_b2, filt_w3, filt_b3, filt_freq, filt_w4, hyena_skip, w_fnet, b_fnet, attn_sink, w_out, ln1_g, ln1_b, w_gate, w_up, w_down, ln2_g, ln2_b):
    bp, lp, _ = x_prompt.shape
    bs, ls, _ = x_sample.shape
    tp, ts = bp * lp, bs * ls
    batches = ((0, bp, lp), (tp, bs, ls))
    tm = _pick_tile(math.gcd(tp, ts), 1024)
    tln = _pick_tile(math.gcd(tp, ts), 256)

    conv_tabs = {seq: _conv_tables(seq) for seq in {lp, ls}}
    fnet_tabs = {seq: _fnet_tables(seq) for seq in {lp, ls}}

    xf, xb = _ln0(x_prompt.reshape(tp, D_MODEL), x_sample.reshape(ts, D_MODEL), ln0_g, ln0_b, tln)
    pad_ff = D_FF_PAD - D_FF
    for l in range(DEPTH):
        w_in_b = w_in[l].astype(BF16)
        w_out_b = w_out[l].astype(BF16)
        w_fnet_b = w_fnet[l].astype(BF16)
        wg_b = jnp.pad(w_gate[l].astype(BF16), ((0, 0), (0, pad_ff)))
        wu_b = jnp.pad(w_up[l].astype(BF16), ((0, 0), (0, pad_ff)))
        wd_b = jnp.pad(w_down[l].astype(BF16), ((0, pad_ff), (0, 0)))
        fw = (filt_w1[l], filt_b1[l], filt_w2[l], filt_b2[l], filt_w3[l], filt_b3[l], filt_freq[l], filt_w4[l])

        proj = _matmul([xb], w_in_b, None, F32, tm, _pick_tile(IN_WIDTH, 1024), "in_proj")

        kf = {seq: _filter_spectrum(seq, conv_tabs[seq], fw) for seq in {lp, ls}}
        y_h = jnp.concatenate(
            [_hyena_batch(proj, r0, nb, seq, conv_tabs[seq], kf[seq], short_w[l], short_b[l], hyena_skip[l])
             for r0, nb, seq in batches], axis=0)
        y_f = jnp.concatenate(
            [_fnet_batch(proj, r0, nb, seq, fnet_tabs[seq]) for r0, nb, seq in batches], axis=0)
        y_f = _matmul([y_f], w_fnet_b, b_fnet[l], BF16, tm, 1024, "fnet_linear")
        y_a = _attention(proj, attn_sink[l], tp, lp, ls)

        y = _matmul([y_h, y_f, y_a], w_out_b, None, F32, tm, 512, "out_proj", resid=xf)
        xf, xb = _ln(y, ln1_g[l], ln1_b[l], tln)

        hid = _gate_up(xb, wg_b, wu_b, tm, 512)
        y = _matmul([hid], wd_b, None, F32, _pick_tile(tm, 512), 512, "ffn_down", resid=xf)
        if l + 1 < DEPTH:
            xf, xb = _ln(y, ln2_g[l], ln2_b[l], tln)
        else:
            y_p, y_s = _ln_final(y, ln2_g[l], ln2_b[l], tln, tp)
    return y_p.reshape(bp, lp, D_MODEL), y_s.reshape(bs, ls, D_MODEL)
```

```python
import functools
import math

import jax
import jax.numpy as jnp
from jax import lax
from jax.experimental import pallas as pl
from jax.experimental.pallas import tpu as pltpu

F32 = jnp.float32
BF16 = jnp.bfloat16

D_MODEL = 4096
HEAD_DIM = 128
HYENA_WIDTH = 1024
FNET_WIDTH = 1024
ATTN_WIDTH = 2048
FNET_HEADS = 8
N_HEADS = 16
N_KV_HEADS = 4
KV_GROUP = 4
KV_WIDTH = 512
BLOCK = 128
HYENA_IN = 3 * HYENA_WIDTH
FILTER_EMB = 33
FILTER_HIDDEN = 64
FILTER_OUT = 4 * HYENA_WIDTH
OFF_FNET = HYENA_IN
OFF_Q = OFF_FNET + FNET_WIDTH
OFF_K = OFF_Q + ATTN_WIDTH
OFF_V = OFF_K + KV_WIDTH
IN_WIDTH = OFF_V + KV_WIDTH
D_FF = 11008
FF_TILE = 256
DEPTH = 2
ALPHA = (2 * DEPTH) ** 0.25
LN_EPS = 1e-5
DECAY_FAST_PCT = 0.3
DECAY_SLOW_PCT = 1.5
DECAY_TARGET = 1e-2

DFT_N2 = 128
LANES = 128
T2_GROUP = 8
PITCH = 136
CONV_PASSES = 1
SPEC_PASSES = 1
INNER_GROUP = 4
ATTN_TQ = 512
VMEM_LIMIT = 56 * 1024 * 1024


def _cparams(sem, vmem=VMEM_LIMIT):
    return pltpu.CompilerParams(dimension_semantics=sem, vmem_limit_bytes=vmem)


def _dot(a, b):
    return jnp.dot(a, b, preferred_element_type=F32)


def _split(x):
    hi = x.astype(BF16)
    lo = (x - hi.astype(F32)).astype(BF16)
    return hi, lo


def _dot3_left(m_hi, m_lo, x):
    x_hi, x_lo = _split(x)
    return _dot(m_hi, x_hi) + (_dot(m_hi, x_lo) + _dot(m_lo, x_hi))


def _split_const(m):
    m = m.astype(F32)
    hi = m.astype(BF16)
    lo = (m - hi.astype(F32)).astype(BF16)
    return hi, lo


def _cs(num, den):
    ang = (2.0 * math.pi / den) * (num % den).astype(F32)
    return jnp.cos(ang), jnp.sin(ang)


def _full_spec(a):
    return pl.BlockSpec(a.shape, lambda *_: (0,) * a.ndim)


def _ln_math(x, g, b):
    mu = jnp.mean(x, axis=-1, keepdims=True)
    xc = x - mu
    var = jnp.mean(xc * xc, axis=-1, keepdims=True)
    return xc * lax.rsqrt(var + LN_EPS) * g + b


def _ln0_kernel(xp_ref, xs_ref, g_ref, b_ref, of_ref, ob_ref, *, n_p):
    i = pl.program_id(0)

    def emit(x):
        y = _ln_math(x, g_ref[...], b_ref[...])
        of_ref[...] = y
        ob_ref[...] = y.astype(BF16)

    @pl.when(i < n_p)
    def _():
        emit(xp_ref[...])

    @pl.when(i >= n_p)
    def _():
        emit(xs_ref[...])


def _ln0(xp, xs, g, b, tm):
    tp, ts = xp.shape[0], xs.shape[0]
    n_p, n_s = tp // tm, ts // tm
    t = tp + ts
    return pl.pallas_call(
        functools.partial(_ln0_kernel, n_p=n_p),
        grid=(n_p + n_s,),
        in_specs=[
            pl.BlockSpec((tm, D_MODEL), lambda i: (jnp.minimum(i, n_p - 1), 0)),
            pl.BlockSpec((tm, D_MODEL), lambda i: (jnp.maximum(i - n_p, 0), 0)),
            pl.BlockSpec((1, D_MODEL), lambda i: (0, 0)),
            pl.BlockSpec((1, D_MODEL), lambda i: (0, 0)),
        ],
        out_specs=[
            pl.BlockSpec((tm, D_MODEL), lambda i: (i, 0)),
            pl.BlockSpec((tm, D_MODEL), lambda i: (i, 0)),
        ],
        out_shape=[jax.ShapeDtypeStruct((t, D_MODEL), F32), jax.ShapeDtypeStruct((t, D_MODEL), BF16)],
        compiler_params=_cparams(("parallel",)),
        name="ln0",
    )(xp, xs, g.reshape(1, -1), b.reshape(1, -1))


def _ln_kernel(y_ref, g_ref, b_ref, of_ref, ob_ref):
    y = _ln_math(y_ref[...], g_ref[...], b_ref[...])
    of_ref[...] = y
    ob_ref[...] = y.astype(BF16)


def _ln(y, g, b, tm):
    t = y.shape[0]
    row = pl.BlockSpec((tm, D_MODEL), lambda i: (i, 0))
    vec = pl.BlockSpec((1, D_MODEL), lambda i: (0, 0))
    return pl.pallas_call(
        _ln_kernel,
        grid=(t // tm,),
        in_specs=[row, vec, vec],
        out_specs=[row, row],
        out_shape=[jax.ShapeDtypeStruct((t, D_MODEL), F32), jax.ShapeDtypeStruct((t, D_MODEL), BF16)],
        compiler_params=_cparams(("parallel",)),
        name="layer_norm",
    )(y, g.reshape(1, -1), b.reshape(1, -1))


def _ln_final_kernel(y_ref, g_ref, b_ref, op_ref, os_ref, *, n_p):
    i = pl.program_id(0)
    y = _ln_math(y_ref[...], g_ref[...], b_ref[...])

    @pl.when(i < n_p)
    def _():
        op_ref[...] = y

    @pl.when(i >= n_p)
    def _():
        os_ref[...] = y


def _ln_final(y, g, b, tm, tp):
    t = y.shape[0]
    n_p = tp // tm
    row = pl.BlockSpec((tm, D_MODEL), lambda i: (i, 0))
    vec = pl.BlockSpec((1, D_MODEL), lambda i: (0, 0))
    return pl.pallas_call(
        functools.partial(_ln_final_kernel, n_p=n_p),
        grid=(t // tm,),
        in_specs=[row, vec, vec],
        out_specs=[
            pl.BlockSpec((tm, D_MODEL), lambda i: (jnp.minimum(i, n_p - 1), 0)),
            pl.BlockSpec((tm, D_MODEL), lambda i: (jnp.maximum(i - n_p, 0), 0)),
        ],
        out_shape=[jax.ShapeDtypeStruct((tp, D_MODEL), F32), jax.ShapeDtypeStruct((t - tp, D_MODEL), F32)],
        compiler_params=_cparams(("arbitrary",)),
        name="layer_norm_final",
    )(y, g.reshape(1, -1), b.reshape(1, -1))


def _mm_kernel(*refs, widths, has_bias, has_resid):
    n_a = len(widths)
    a_refs = refs[:n_a]
    w_ref = refs[n_a]
    o_ref = refs[-1]
    nxt = n_a + 1
    acc = None
    off = 0
    for a_ref, wd in zip(a_refs, widths):
        part = _dot(a_ref[...].astype(BF16), w_ref[off:off + wd, :])
        acc = part if acc is None else acc + part
        off += wd
    if has_bias:
        acc = acc + refs[nxt][...]
        nxt += 1
    if has_resid:
        acc = ALPHA * refs[nxt][...] + acc
    o_ref[...] = acc.astype(o_ref.dtype)


def _matmul(a_list, w, bias, out_dtype, tm, tn, name, resid=None):
    t = a_list[0].shape[0]
    k, n = w.shape
    widths = tuple(a.shape[1] for a in a_list)
    assert sum(widths) == k and t % tm == 0 and n % tn == 0
    in_specs = [pl.BlockSpec((tm, wd), lambda i, j: (i, 0)) for wd in widths]
    in_specs.append(pl.BlockSpec((k, tn), lambda i, j: (0, j)))
    args = list(a_list) + [w]
    if bias is not None:
        in_specs.append(pl.BlockSpec((1, tn), lambda i, j: (0, j)))
        args.append(bias.reshape(1, n).astype(F32))
    if resid is not None:
        in_specs.append(pl.BlockSpec((tm, tn), lambda i, j: (i, j)))
        args.append(resid)
    return pl.pallas_call(
        functools.partial(_mm_kernel, widths=widths, has_bias=bias is not None, has_resid=resid is not None),
        grid=(t // tm, n // tn),
        in_specs=in_specs,
        out_specs=pl.BlockSpec((tm, tn), lambda i, j: (i, j)),
        out_shape=jax.ShapeDtypeStruct((t, n), out_dtype),
        compiler_params=_cparams(("parallel", "parallel")),
        name=name,
    )(*args)


def _gate_up_kernel(x_ref, wg_ref, wu_ref, o_ref):
    x = x_ref[...]
    g = _dot(x, wg_ref[...])
    u = _dot(x, wu_ref[...])
    o_ref[...] = (g * (1.0 / (1.0 + jnp.exp(-g))) * u).astype(o_ref.dtype)


def _gate_up(x, wg, wu, tm, tn):
    t, k = x.shape
    n = wg.shape[1]
    return pl.pallas_call(
        _gate_up_kernel,
        grid=(t // tm, n // tn),
        in_specs=[
            pl.BlockSpec((tm, k), lambda i, j: (i, 0)),
            pl.BlockSpec((k, tn), lambda i, j: (0, j)),
            pl.BlockSpec((k, tn), lambda i, j: (0, j)),
        ],
        out_specs=pl.BlockSpec((tm, tn), lambda i, j: (i, j)),
        out_shape=jax.ShapeDtypeStruct((t, n), BF16),
        compiler_params=_cparams(("parallel", "parallel")),
        name="ffn_gate_up",
    )(x, wg, wu)


def _attn_kernel(q_ref, kp_ref, kc_ref, kn_ref, vp_ref, vc_ref, vn_ref, bias_ref, sink_ref, o_ref, kbuf, vbuf,
                 *, tq, tiles_p, per_p, per_s):
    g = pl.program_id(0)
    h = pl.program_id(1)
    in_p = g < tiles_p
    n_loc = jnp.where(in_p, g % per_p, (g - tiles_p) % per_s)
    n_seq = jnp.where(in_p, per_p, per_s)
    pen_prev = jnp.where(n_loc > 0, 0.0, -jnp.inf)
    pen_next = jnp.where(n_loc < n_seq - 1, 0.0, -jnp.inf)

    kbuf[0:BLOCK, :] = kp_ref[...].astype(BF16)
    kbuf[BLOCK:BLOCK + tq, :] = kc_ref[...].astype(BF16)
    kbuf[BLOCK + tq:, :] = kn_ref[...].astype(BF16)
    vbuf[:, 0:BLOCK] = vp_ref[...].T.astype(BF16)
    for c0 in range(0, tq, BLOCK):
        vbuf[:, BLOCK + c0:2 * BLOCK + c0] = vc_ref[c0:c0 + BLOCK, :].T.astype(BF16)
    vbuf[:, BLOCK + tq:] = vn_ref[...].T.astype(BF16)

    key = lax.broadcasted_iota(jnp.int32, (3 * BLOCK, BLOCK), 0)
    col_prev = jnp.where(key < BLOCK, pen_prev, 0.0)
    col_next = jnp.where(key >= 2 * BLOCK, pen_next, 0.0)
    scale = HEAD_DIM ** -0.5
    dn = (((1,), (1,)), ((), ()))
    nsb = tq // BLOCK
    for sb in range(nsb):
        r0 = sb * BLOCK
        k3 = kbuf[r0:r0 + 3 * BLOCK, :]
        v3 = vbuf[:, r0:r0 + 3 * BLOCK]
        for gi in range(KV_GROUP):
            sink = sink_ref[h * KV_GROUP + gi]
            q = q_ref[r0:r0 + BLOCK, gi * HEAD_DIM:(gi + 1) * HEAD_DIM].astype(BF16)
            s = lax.dot_general(k3, q, dn, preferred_element_type=F32) * scale
            s = s + bias_ref[gi * 3 * BLOCK:(gi + 1) * 3 * BLOCK, :]
            if sb == 0:
                s = s + col_prev
            if sb == nsb - 1:
                s = s + col_next
            m = jnp.maximum(jnp.max(s, axis=0, keepdims=True), sink)
            p = jnp.exp(s - m)
            denom = jnp.sum(p, axis=0, keepdims=True) + jnp.exp(sink - m)
            o_t = _dot(v3, p.astype(BF16)) * (1.0 / denom)
            o_ref[r0:r0 + BLOCK, gi * HEAD_DIM:(gi + 1) * HEAD_DIM] = o_t.T.astype(o_ref.dtype)


def _attention(proj, sink, tp, lp, ls):
    t = proj.shape[0]
    tq = math.gcd(ATTN_TQ, math.gcd(lp, ls))
    nblk = t // BLOCK
    bpt = tq // BLOCK
    slopes = 2.0 ** (-8.0 * jnp.arange(1, N_HEADS + 1, dtype=F32) / N_HEADS)
    qi = jnp.arange(BLOCK)[:, None]
    ki = jnp.arange(3 * BLOCK)[None, :]
    dist = jnp.abs(qi + BLOCK - ki)
    bias = jnp.where(dist[None] <= BLOCK, -slopes[:, None, None] * dist[None].astype(F32), -jnp.inf)
    bias = jnp.swapaxes(bias, 1, 2).reshape(N_HEADS * 3 * BLOCK, BLOCK)
    qc = OFF_Q // (KV_GROUP * HEAD_DIM)
    kc = OFF_K // HEAD_DIM
    vc = OFF_V // HEAD_DIM
    prev = lambda g: jnp.maximum(g * bpt - 1, 0)
    nxt = lambda g: jnp.minimum((g + 1) * bpt, nblk - 1)
    halo = (BLOCK, HEAD_DIM)
    cur = (tq, HEAD_DIM)
    return pl.pallas_call(
        functools.partial(_attn_kernel, tq=tq, tiles_p=tp // tq, per_p=lp // tq, per_s=ls // tq),
        grid=(t // tq, N_KV_HEADS),
        in_specs=[
            pl.BlockSpec((tq, KV_GROUP * HEAD_DIM), lambda g, h: (g, qc + h)),
            pl.BlockSpec(halo, lambda g, h: (prev(g), kc + h)),
            pl.BlockSpec(cur, lambda g, h: (g, kc + h)),
            pl.BlockSpec(halo, lambda g, h: (nxt(g), kc + h)),
            pl.BlockSpec(halo, lambda g, h: (prev(g), vc + h)),
            pl.BlockSpec(cur, lambda g, h: (g, vc + h)),
            pl.BlockSpec(halo, lambda g, h: (nxt(g), vc + h)),
            pl.BlockSpec((KV_GROUP * 3 * BLOCK, BLOCK), lambda g, h: (h, 0)),
            pl.BlockSpec(memory_space=pltpu.SMEM),
        ],
        out_specs=pl.BlockSpec((tq, KV_GROUP * HEAD_DIM), lambda g, h: (g, h)),
        out_shape=jax.ShapeDtypeStruct((t, ATTN_WIDTH), BF16),
        scratch_shapes=[pltpu.VMEM((tq + 2 * BLOCK, HEAD_DIM), BF16), pltpu.VMEM((HEAD_DIM, tq + 2 * BLOCK), BF16)],
        compiler_params=_cparams(("parallel", "parallel")),
        name="band_attention",
    )(proj, proj, proj, proj, proj, proj, proj, bias, sink.astype(F32))


def _filter_kernel(fvec_ref, w1_ref, b1_ref, w2_ref, b2_ref, w3_ref, b3_ref, fr_ref, w4_ref, delta_ref, o_ref,
                   *, seq, tl):
    r0 = pl.program_id(0) * tl
    row = (lax.broadcasted_iota(jnp.int32, (tl, 128), 0) + r0).astype(F32)
    lane = lax.broadcasted_iota(jnp.int32, (tl, 128), 1)
    t = row * (1.0 / (seq - 1))
    ang = (row * (2.0 * math.pi / seq)) * fvec_ref[...]
    z = jnp.where(lane == 0, t,
                  jnp.where(lane <= 16, jnp.cos(ang), jnp.where(lane <= 32, -jnp.sin(ang), 0.0)))
    hp = lax.Precision.HIGHEST
    h = jnp.sin(fr_ref[0:1, :] * (jnp.dot(z, w1_ref[...], precision=hp, preferred_element_type=F32) + b1_ref[...]))
    h = jnp.sin(fr_ref[1:2, :] * (jnp.dot(h, w2_ref[...], precision=hp, preferred_element_type=F32) + b2_ref[...]))
    h = jnp.sin(fr_ref[2:3, :] * (jnp.dot(h, w3_ref[...], precision=hp, preferred_element_type=F32) + b3_ref[...]))
    trow = (lax.broadcasted_iota(jnp.int32, (tl, HYENA_WIDTH), 0) + r0)
    decay = jnp.exp(-(trow.astype(F32) * (1.0 / (seq - 1))) * delta_ref[...])
    first = trow == 0
    for part in range(4):
        sl = slice(part * HYENA_WIDTH, (part + 1) * HYENA_WIDTH)
        v = jnp.dot(h, w4_ref[:, sl], precision=hp, preferred_element_type=F32) * decay
        if part >= 2:
            v = jnp.where(first, 0.0, v)
        o_ref[:, sl] = v


def _hyena_filters(seq, w1, b1, w2, b2, w3, b3, freq, w4):
    tl = min(seq, 512)
    bands = (FILTER_EMB - 1) // 2
    f = jnp.linspace(1e-4, bands - 1, bands, dtype=F32)
    fvec = jnp.zeros((1, 128), F32).at[0, 1:1 + bands].set(f).at[0, 1 + bands:1 + 2 * bands].set(f)
    w1p = jnp.zeros((128, FILTER_HIDDEN), F32).at[:FILTER_EMB].set(w1.astype(F32))
    max_decay = math.log(DECAY_TARGET) / DECAY_FAST_PCT
    min_decay = math.log(DECAY_TARGET) / DECAY_SLOW_PCT
    deltas = jnp.abs(jnp.linspace(min_decay, max_decay, HYENA_WIDTH, dtype=F32)).reshape(1, -1)
    args = [fvec, w1p, b1.reshape(1, -1).astype(F32), w2.astype(F32), b2.reshape(1, -1).astype(F32),
            w3.astype(F32), b3.reshape(1, -1).astype(F32), freq.astype(F32), w4.astype(F32), deltas]
    return pl.pallas_call(
        functools.partial(_filter_kernel, seq=seq, tl=tl),
        grid=(seq // tl,),
        in_specs=[_full_spec(a) for a in args],
        out_specs=pl.BlockSpec((tl, FILTER_OUT), lambda i: (i, 0)),
        out_shape=jax.ShapeDtypeStruct((seq, FILTER_OUT), F32),
        compiler_params=_cparams(("parallel",)),
        name="hyena_filter",
    )(*args)


def _dotp(mh_ref, ml_ref, x, passes):
    if passes == 1:
        return _dot(mh_ref[...], x.astype(BF16))
    return _dot3_left(mh_ref[...], ml_ref[...], x)


def _gather(ref, t2, n):
    return ref[pl.ds(t2, n, stride=PITCH), :]


def _fill_pitched(u_ref, p_ref, half, w_ref=None, b_ref=None):
    n2 = DFT_N2
    seq = half * n2
    rid = lax.broadcasted_iota(jnp.int32, (n2, LANES), 0)

    def step(t1, carry):
        r0 = pl.multiple_of(t1 * n2, n2)
        cur = u_ref[pl.ds(r0, n2), :]
        if w_ref is not None:
            before = u_ref[pl.ds(pl.multiple_of(jnp.maximum(r0 - 8, 0), 8), 8), :][7:8, :]
            after = u_ref[pl.ds(pl.multiple_of(jnp.minimum(r0 + n2, seq - 8), 8), 8), :][0:1, :]
            before = jnp.where(t1 > 0, before, 0.0)
            after = jnp.where(t1 < half - 1, after, 0.0)
            prev = jnp.where(rid == 0, before, pltpu.roll(cur, 1, 0))
            nxt = jnp.where(rid == n2 - 1, after, pltpu.roll(cur, n2 - 1, 0))
            cur = prev * w_ref[0:1, :] + cur * w_ref[1:2, :] + nxt * w_ref[2:3, :] + b_ref[...]
        p_ref[pl.ds(pl.multiple_of(t1 * PITCH, 8), n2), :] = cur
        return carry

    lax.fori_loop(0, half, step, 0)


def _for_t2_groups(body):
    def step(g, carry):
        body(g * T2_GROUP)
        return carry

    lax.fori_loop(0, DFT_N2 // T2_GROUP, step, 0)


def _outer_stage(p_ref, half, mh_ref, ml_ref, a_ref, nrows, passes):
    def body(base):
        xs = jnp.concatenate([_gather(p_ref, base + j, half) for j in range(T2_GROUP)], axis=1)
        y = _dotp(mh_ref, ml_ref, xs, passes)
        for j in range(T2_GROUP):
            a_ref[pl.ds(base + j, nrows, stride=PITCH), :] = y[:, j * LANES:(j + 1) * LANES]

    _for_t2_groups(body)


def _inner_blocks(a_ref, tw_ref, wstep_ref, nblocks, compute_fn, store_fn):
    n2 = DFT_N2
    tw_ref[0:n2, :] = jnp.ones((n2, LANES), F32)
    tw_ref[n2:, :] = jnp.zeros((n2, LANES), F32)

    def run(k0, count):
        twr = tw_ref[0:n2, :]
        twi = tw_ref[n2:, :]
        wr = wstep_ref[0:n2, :]
        wi = wstep_ref[n2:, :]
        done = []
        for j in range(count):
            k = k0 + j
            rr = pl.multiple_of(k * 2 * PITCH, 8)
            ri = pl.multiple_of(k * 2 * PITCH + PITCH, 8)
            ar = a_ref[pl.ds(rr, n2), :]
            ai = a_ref[pl.ds(ri, n2), :]
            done.append((k, rr, ri, compute_fn(k, ar * twr - ai * twi, ar * twi + ai * twr, twr, twi)))
            twr, twi = twr * wr - twi * wi, twr * wi + twi * wr
        for k, rr, ri, vals in done:
            store_fn(k, rr, ri, vals)
        tw_ref[0:n2, :] = twr
        tw_ref[n2:, :] = twi

    def step(g, carry):
        run(g * INNER_GROUP, INNER_GROUP)
        return carry

    lax.fori_loop(0, nblocks // INNER_GROUP, step, 0)
    if nblocks % INNER_GROUP:
        run(nblocks - nblocks % INNER_GROUP, nblocks % INNER_GROUP)


def _hyena_conv_kernel(sig_ref, gate_ref, wsig_ref, bsig_ref, wgate_ref, bgate_ref, skip_ref, kf_ref,
                       f1h_ref, f1l_ref, g3h_ref, g3l_ref, gch_ref, gcl_ref, gbh_ref, gbl_ref, wstep_ref,
                       o_ref, a_ref, tw_ref, pv_ref, pg_ref, *, half, k1, k1p, sig_conv):
    n2 = DFT_N2
    if sig_conv:
        _fill_pitched(sig_ref, pv_ref, half, wsig_ref, bsig_ref)
    else:
        _fill_pitched(sig_ref, pv_ref, half)
    _fill_pitched(gate_ref, pg_ref, half, wgate_ref, bgate_ref)
    _outer_stage(pv_ref, half, f1h_ref, f1l_ref, a_ref, 2 * k1p, CONV_PASSES)

    def block(k, apr, api, twr, twi):
        x = _dotp(gch_ref, gcl_ref, jnp.concatenate([apr, api], axis=0), CONV_PASSES)
        xr, xi = x[:n2], x[n2:]
        k0 = pl.multiple_of(k * 2 * n2, 2 * n2)
        kr = kf_ref[pl.ds(k0, n2), :]
        ki = kf_ref[pl.ds(k0 + n2, n2), :]
        p = jnp.concatenate([xr * kr - xi * ki, xr * ki + xi * kr], axis=0)
        bb = _dotp(gbh_ref, gbl_ref, p, CONV_PASSES)
        br, bi = bb[:n2], bb[n2:]
        return br * twr + bi * twi, bi * twr - br * twi

    def put(k, rr, ri, vals):
        a_ref[pl.ds(rr, n2), :] = vals[0]
        a_ref[pl.ds(ri, n2), :] = vals[1]

    _inner_blocks(a_ref, tw_ref, wstep_ref, k1, block, put)

    def finish(base):
        bcat = jnp.concatenate([_gather(a_ref, base + j, 2 * k1p) for j in range(T2_GROUP)], axis=1)
        y = _dotp(g3h_ref, g3l_ref, bcat, CONV_PASSES)
        for j in range(T2_GROUP):
            v = _gather(pv_ref, base + j, half)
            gate = _gather(pg_ref, base + j, half)
            pg_ref[pl.ds(base + j, half, stride=PITCH), :] = gate * (y[:, j * LANES:(j + 1) * LANES]
                                                                    + v * skip_ref[...])

    _for_t2_groups(finish)

    def emit(t1, carry):
        o_ref[pl.ds(pl.multiple_of(t1 * n2, n2), n2), :] = pg_ref[pl.ds(pl.multiple_of(t1 * PITCH, 8), n2), :]
        return carry

    lax.fori_loop(0, half, emit, 0)


def _hyena_spec_kernel(ff_ref, fb_ref, f1h_ref, f1l_ref, gch_ref, gcl_ref, wstep_ref, o_ref, a_ref, tw_ref, pf_ref,
                       *, half, k1, k1p):
    n2 = DFT_N2
    _fill_pitched(ff_ref, pf_ref, half)
    _outer_stage(pf_ref, half, f1h_ref, f1l_ref, a_ref, 2 * k1p, SPEC_PASSES)

    def spectrum(k, apr, api, twr, twi):
        return _dotp(gch_ref, gcl_ref, jnp.concatenate([apr, api], axis=0), SPEC_PASSES)

    def put_fwd(k, rr, ri, x):
        o_ref[pl.ds(pl.multiple_of(k * 2 * n2, 2 * n2), 2 * n2), :] = x

    _inner_blocks(a_ref, tw_ref, wstep_ref, k1, spectrum, put_fwd)
    _fill_pitched(fb_ref, pf_ref, half)
    _outer_stage(pf_ref, half, f1h_ref, f1l_ref, a_ref, 2 * k1p, SPEC_PASSES)

    def add_bwd(k, rr, ri, x):
        k0 = pl.multiple_of(k * 2 * n2, 2 * n2)
        o_ref[pl.ds(k0, n2), :] = o_ref[pl.ds(k0, n2), :] + x[:n2]
        o_ref[pl.ds(k0 + n2, n2), :] = o_ref[pl.ds(k0 + n2, n2), :] - x[n2:]

    _inner_blocks(a_ref, tw_ref, wstep_ref, k1, spectrum, add_bwd)


def _conv_tables(seq):
    n = 2 * seq
    n2 = DFT_N2
    n1 = n // n2
    k1 = n1 // 2 + 1
    k1p = -(-k1 // 8) * 8
    half = n1 // 2
    kk = jnp.arange(k1p, dtype=jnp.int32)
    valid = (kk < k1)
    t1 = jnp.arange(half, dtype=jnp.int32)
    c, s = _cs(kk[:, None] * t1[None, :], n1)
    vm = valid[:, None].astype(F32)
    f1 = jnp.stack([c * vm, -s * vm], axis=1).reshape(2 * k1p, half)
    wgt = jnp.where((kk == 0) | (kk == n1 // 2), 1.0, 2.0) * valid.astype(F32) / n
    g3 = jnp.stack([c * vm * wgt[:, None], -s * vm * wgt[:, None]], axis=1).reshape(2 * k1p, half).T
    j = jnp.arange(n2, dtype=jnp.int32)
    cr, cs_ = _cs(j[:, None] * j[None, :], n2)
    gc = jnp.concatenate([jnp.concatenate([cr, cs_], axis=1), jnp.concatenate([-cs_, cr], axis=1)], axis=0)
    gb = jnp.concatenate([jnp.concatenate([cr, -cs_], axis=1), jnp.concatenate([cs_, cr], axis=1)], axis=0)
    wr, ws = _cs(j, n)
    wstep = jnp.concatenate([jnp.broadcast_to(wr[:, None], (n2, LANES)),
                             jnp.broadcast_to(-ws[:, None], (n2, LANES))], axis=0)
    tabs = dict(n1=n1, k1=k1, k1p=k1p, half=half, wstep=wstep)
    for name, m in (("f1", f1), ("g3", g3), ("gc", gc), ("gb", gb)):
        tabs[name + "h"], tabs[name + "l"] = _split_const(m)
    return tabs


def _filter_spectrum(seq, tabs, fw):
    filt = _hyena_filters(seq, *fw)
    half, k1, k1p = tabs["half"], tabs["k1"], tabs["k1p"]
    n2 = DFT_N2
    nct = HYENA_WIDTH // LANES
    consts = [tabs[n] for n in ("f1h", "f1l", "gch", "gcl", "wstep")]
    return pl.pallas_call(
        functools.partial(_hyena_spec_kernel, half=half, k1=k1, k1p=k1p),
        grid=(2, nct),
        in_specs=[pl.BlockSpec((seq, LANES), lambda o, j: (0, o * nct + j)),
                  pl.BlockSpec((seq, LANES), lambda o, j: (0, (2 + o) * nct + j))] + [_full_spec(a) for a in consts],
        out_specs=pl.BlockSpec((None, k1 * 2 * n2, LANES), lambda o, j: (o, 0, j)),
        out_shape=jax.ShapeDtypeStruct((2, k1 * 2 * n2, HYENA_WIDTH), F32),
        scratch_shapes=[pltpu.VMEM((k1p * 2 * PITCH, LANES), F32), pltpu.VMEM((2 * n2, LANES), F32),
                        pltpu.VMEM((half * PITCH, LANES), F32)],
        compiler_params=_cparams(("parallel", "parallel")),
        name="hyena_filter_spectrum",
    )(filt, filt, *consts)


def _hyena_conv(sig, sig_cols, gate_cols, proj, row0, nbatch, seq, tabs, kf, order, short_w, short_b, skip):
    assert row0 % seq == 0
    b0 = row0 // seq
    half, k1, k1p = tabs["half"], tabs["k1"], tabs["k1p"]
    n2 = DFT_N2
    nct = HYENA_WIDTH // LANES
    sig_conv = sig is None
    if sig_conv:
        sig_arr = proj
        sig_spec = pl.BlockSpec((seq, LANES), lambda j, b: (b0 + b, sig_cols + j))
    else:
        sig_arr = sig
        sig_spec = pl.BlockSpec((seq, LANES), lambda j, b: (b, j))
    sw = short_w.astype(F32)
    sb = short_b.reshape(1, -1).astype(F32)
    consts = [tabs[n] for n in ("f1h", "f1l", "g3h", "g3l", "gch", "gcl", "gbh", "gbl", "wstep")]
    return pl.pallas_call(
        functools.partial(_hyena_conv_kernel, half=half, k1=k1, k1p=k1p, sig_conv=sig_conv),
        grid=(nct, nbatch),
        in_specs=[
            sig_spec,
            pl.BlockSpec((seq, LANES), lambda j, b: (b0 + b, gate_cols + j), pipeline_mode=pl.Buffered(1)),
            pl.BlockSpec((3, LANES), lambda j, b: (0, sig_cols + j)),
            pl.BlockSpec((1, LANES), lambda j, b: (0, sig_cols + j)),
            pl.BlockSpec((3, LANES), lambda j, b: (0, gate_cols + j)),
            pl.BlockSpec((1, LANES), lambda j, b: (0, gate_cols + j)),
            pl.BlockSpec((1, LANES), lambda j, b: (0, j)),
            pl.BlockSpec((None, k1 * 2 * n2, LANES), lambda j, b: (order, 0, j), pipeline_mode=pl.Buffered(1)),
        ] + [_full_spec(a) for a in consts],
        out_specs=pl.BlockSpec((seq, LANES), lambda j, b: (b, j)),
        out_shape=jax.ShapeDtypeStruct((nbatch * seq, HYENA_WIDTH), F32),
        scratch_shapes=[pltpu.VMEM((k1p * 2 * PITCH, LANES), F32), pltpu.VMEM((2 * n2, LANES), F32),
                        pltpu.VMEM((half * PITCH, LANES), F32), pltpu.VMEM((half * PITCH, LANES), F32)],
        compiler_params=_cparams(("parallel", "arbitrary")),
        name="hyena_conv",
    )(sig_arr, proj, sw, sb, sw, sb, skip.reshape(1, -1).astype(F32), kf, *consts)


def _hyena_batch(proj, row0, nbatch, seq, tabs, kf, short_w, short_b, skip):
    nct = HYENA_WIDTH // LANES
    z = _hyena_conv(None, 2 * nct, 0, proj, row0, nbatch, seq, tabs, kf, 0, short_w, short_b, skip[0])
    return _hyena_conv(z, 2 * nct, nct, proj, row0, nbatch, seq, tabs, kf, 1, short_w, short_b, skip[1])


def _fnet_kernel(u_ref, chan_ref, m1_ref, gri_ref, wstep_ref, o_ref, zr_ref, zi_ref, a_ref, tw_ref, *, n1):
    n2 = DFT_N2
    slabs = 4

    def chan(g, carry):
        x = u_ref[pl.ds(pl.multiple_of(g * slabs * n2, slabs * n2), slabs * n2), :].astype(BF16)
        z = _dot(x, chan_ref[...])
        for i in range(slabs):
            r = pl.multiple_of((g * slabs + i) * PITCH, 8)
            zr_ref[pl.ds(r, n2), :] = z[i * n2:(i + 1) * n2, :LANES]
            zi_ref[pl.ds(r, n2), :] = z[i * n2:(i + 1) * n2, LANES:]
        return carry

    lax.fori_loop(0, n1 // slabs, chan, 0)

    def outer(base):
        xs = jnp.concatenate(
            [jnp.concatenate([_gather(zr_ref, base + j, n1), _gather(zi_ref, base + j, n1)], axis=0)
             for j in range(T2_GROUP)], axis=1)
        y = _dot(m1_ref[...], xs.astype(BF16))
        for j in range(T2_GROUP):
            a_ref[pl.ds(base + j, 2 * n1, stride=PITCH), :] = y[:, j * LANES:(j + 1) * LANES]

    _for_t2_groups(outer)

    def real_part(k, apr, api, twr, twi):
        return _dot(gri_ref[...], jnp.concatenate([apr, api], axis=0).astype(BF16))

    def put(k, rr, ri, y):
        zr_ref[pl.ds(pl.multiple_of(k * PITCH, 8), n2), :] = y

    _inner_blocks(a_ref, tw_ref, wstep_ref, n1, real_part, put)

    def emit(k2, carry):
        o_ref[pl.ds(pl.multiple_of(k2 * n1, n1), n1), :] = _gather(zr_ref, k2, n1).astype(o_ref.dtype)
        return carry

    lax.fori_loop(0, n2, emit, 0)


def _fnet_tables(seq):
    n2 = DFT_N2
    n1 = seq // n2
    j = jnp.arange(HEAD_DIM, dtype=jnp.int32)
    c, s = _cs(j[:, None] * j[None, :], HEAD_DIM)
    scale = (seq * HEAD_DIM) ** -0.5
    chan = jnp.concatenate([c * scale, -s * scale], axis=1).astype(BF16)
    kk = jnp.arange(n1, dtype=jnp.int32)
    c1, s1 = _cs(kk[:, None] * kk[None, :], n1)
    m1 = jnp.stack([jnp.concatenate([c1, s1], axis=1), jnp.concatenate([-s1, c1], axis=1)], axis=1)
    m1 = m1.reshape(2 * n1, 2 * n1).astype(BF16)
    t2 = jnp.arange(n2, dtype=jnp.int32)
    cr, cs_ = _cs(t2[:, None] * t2[None, :], n2)
    gri = jnp.concatenate([cr, cs_], axis=1).astype(BF16)
    wr, ws = _cs(t2, seq)
    wstep = jnp.concatenate([jnp.broadcast_to(wr[:, None], (n2, LANES)),
                             jnp.broadcast_to(-ws[:, None], (n2, LANES))], axis=0)
    return dict(n1=n1, chan=chan, m1=m1, gri=gri, wstep=wstep)


def _fnet_batch(proj, row0, nbatch, seq, tabs):
    assert row0 % seq == 0
    b0 = row0 // seq
    n1 = tabs["n1"]
    consts = [tabs[n] for n in ("chan", "m1", "gri", "wstep")]
    return pl.pallas_call(
        functools.partial(_fnet_kernel, n1=n1),
        grid=(FNET_HEADS, nbatch),
        in_specs=[pl.BlockSpec((seq, LANES), lambda j, b: (b0 + b, OFF_FNET // LANES + j))]
        + [_full_spec(a) for a in consts],
        out_specs=pl.BlockSpec((seq, LANES), lambda j, b: (b, j)),
        out_shape=jax.ShapeDtypeStruct((nbatch * seq, FNET_WIDTH), BF16),
        scratch_shapes=[pltpu.VMEM((n1 * PITCH, LANES), F32), pltpu.VMEM((n1 * PITCH, LANES), F32),
                        pltpu.VMEM((2 * n1 * PITCH, LANES), F32), pltpu.VMEM((2 * DFT_N2, LANES), F32)],
        compiler_params=_cparams(("parallel", "parallel")),
        name="fnet_mixer",
    )(proj, *consts)


def _pick_tile(t, pref):
    while t % pref:
        pref //= 2
    return pref


def kernel(x_prompt, x_sample, ln0_g, ln0_b, w_in, short_w, short_b, filt_w1, filt_b1, filt_w2, filt_b2, filt_w3, filt_b3, filt_freq, filt_w4, hyena_skip, w_fnet, b_fnet, attn_sink, w_out, ln1_g, ln1_b, w_gate, w_up, w_down, ln2_g, ln2_b):
    bp, lp, _ = x_prompt.shape
    bs, ls, _ = x_sample.shape
    tp, ts = bp * lp, bs * ls
    batches = ((0, bp, lp), (tp, bs, ls))
    tm = _pick_tile(math.gcd(tp, ts), 1024)
    tln = _pick_tile(math.gcd(tp, ts), 256)

    conv_tabs = {seq: _conv_tables(seq) for seq in {lp, ls}}
    fnet_tabs = {seq: _fnet_tables(seq) for seq in {lp, ls}}

    xf, xb = _ln0(x_prompt.reshape(tp, D_MODEL), x_sample.reshape(ts, D_MODEL), ln0_g, ln0_b, tln)
    for l in range(DEPTH):
        w_in_b = w_in[l].astype(BF16)
        w_out_b = w_out[l].astype(BF16)
        w_fnet_b = w_fnet[l].astype(BF16)
        wg_b = w_gate[l].astype(BF16)
        wu_b = w_up[l].astype(BF16)
        wd_b = w_down[l].astype(BF16)
        fw = (filt_w1[l], filt_b1[l], filt_w2[l], filt_b2[l], filt_w3[l], filt_b3[l], filt_freq[l], filt_w4[l])

        proj = _matmul([xb], w_in_b, None, F32, tm, _pick_tile(IN_WIDTH, 1024), "in_proj")

        kf = {seq: _filter_spectrum(seq, conv_tabs[seq], fw) for seq in {lp, ls}}
        y_h = jnp.concatenate(
            [_hyena_batch(proj, r0, nb, seq, conv_tabs[seq], kf[seq], short_w[l], short_b[l], hyena_skip[l])
             for r0, nb, seq in batches], axis=0)
        y_f = jnp.concatenate(
            [_fnet_batch(proj, r0, nb, seq, fnet_tabs[seq]) for r0, nb, seq in batches], axis=0)
        y_f = _matmul([y_f], w_fnet_b, b_fnet[l], BF16, tm, 1024, "fnet_linear")
        y_a = _attention(proj, attn_sink[l], tp, lp, ls)

        y = _matmul([y_h, y_f, y_a], w_out_b, None, F32, tm, 512, "out_proj", resid=xf)
        xf, xb = _ln(y, ln1_g[l], ln1_b[l], tln)

        hid = _gate_up(xb, wg_b, wu_b, _pick_tile(math.gcd(tp, ts), 2048), FF_TILE)
        y = _matmul([hid], wd_b, None, F32, _pick_tile(tm, 512), 512, "ffn_down", resid=xf)
        if l + 1 < DEPTH:
            xf, xb = _ln(y, ln2_g[l], ln2_b[l], tln)
        else:
            y_p, y_s = _ln_final(y, ln2_g[l], ln2_b[l], tln, tp)
    return y_p.reshape(bp, lp, D_MODEL), y_s.reshape(bs, ls, D_MODEL)
```

```python
import functools
import math

import jax
import jax.numpy as jnp
from jax import lax
from jax.experimental import pallas as pl
from jax.experimental.pallas import tpu as pltpu

F32 = jnp.float32
BF16 = jnp.bfloat16

D_MODEL = 4096
HEAD_DIM = 128
HYENA_WIDTH = 1024
FNET_WIDTH = 1024
ATTN_WIDTH = 2048
FNET_HEADS = 8
N_HEADS = 16
N_KV_HEADS = 4
KV_GROUP = 4
KV_WIDTH = 512
BLOCK = 128
HYENA_IN = 3 * HYENA_WIDTH
FILTER_EMB = 33
FILTER_HIDDEN = 64
FILTER_OUT = 4 * HYENA_WIDTH
OFF_FNET = HYENA_IN
OFF_Q = OFF_FNET + FNET_WIDTH
OFF_K = OFF_Q + ATTN_WIDTH
OFF_V = OFF_K + KV_WIDTH
IN_WIDTH = OFF_V + KV_WIDTH
D_FF = 11008
FF_TILE = 256
DEPTH = 2
ALPHA = (2 * DEPTH) ** 0.25
LN_EPS = 1e-5
DECAY_FAST_PCT = 0.3
DECAY_SLOW_PCT = 1.5
DECAY_TARGET = 1e-2

DFT_N2 = 128
LANES = 128
T2_GROUP = 8
PITCH = 136
CONV_PASSES = 1
SPEC_PASSES = 1
INNER_GROUP = 4
ATTN_TQ = 512
VMEM_LIMIT = 56 * 1024 * 1024


def _cparams(sem, vmem=VMEM_LIMIT):
    return pltpu.CompilerParams(dimension_semantics=sem, vmem_limit_bytes=vmem)


def _dot(a, b):
    return jnp.dot(a, b, preferred_element_type=F32)


def _split(x):
    hi = x.astype(BF16)
    lo = (x - hi.astype(F32)).astype(BF16)
    return hi, lo


def _dot3_left(m_hi, m_lo, x):
    x_hi, x_lo = _split(x)
    return _dot(m_hi, x_hi) + (_dot(m_hi, x_lo) + _dot(m_lo, x_hi))


def _split_const(m):
    m = m.astype(F32)
    hi = m.astype(BF16)
    lo = (m - hi.astype(F32)).astype(BF16)
    return hi, lo


def _cs(num, den):
    ang = (2.0 * math.pi / den) * (num % den).astype(F32)
    return jnp.cos(ang), jnp.sin(ang)


def _full_spec(a):
    return pl.BlockSpec(a.shape, lambda *_: (0,) * a.ndim)


def _ln_math(x, g, b):
    mu = jnp.mean(x, axis=-1, keepdims=True)
    xc = x - mu
    var = jnp.mean(xc * xc, axis=-1, keepdims=True)
    return xc * lax.rsqrt(var + LN_EPS) * g + b


def _ln0_kernel(xp_ref, xs_ref, g_ref, b_ref, of_ref, ob_ref, *, n_p):
    i = pl.program_id(0)

    def emit(x):
        y = _ln_math(x, g_ref[...], b_ref[...])
        of_ref[...] = y
        ob_ref[...] = y.astype(BF16)

    @pl.when(i < n_p)
    def _():
        emit(xp_ref[...])

    @pl.when(i >= n_p)
    def _():
        emit(xs_ref[...])


def _ln0(xp, xs, g, b, tm):
    tp, ts = xp.shape[0], xs.shape[0]
    n_p, n_s = tp // tm, ts // tm
    t = tp + ts
    return pl.pallas_call(
        functools.partial(_ln0_kernel, n_p=n_p),
        grid=(n_p + n_s,),
        in_specs=[
            pl.BlockSpec((tm, D_MODEL), lambda i: (jnp.minimum(i, n_p - 1), 0)),
            pl.BlockSpec((tm, D_MODEL), lambda i: (jnp.maximum(i - n_p, 0), 0)),
            pl.BlockSpec((1, D_MODEL), lambda i: (0, 0)),
            pl.BlockSpec((1, D_MODEL), lambda i: (0, 0)),
        ],
        out_specs=[
            pl.BlockSpec((tm, D_MODEL), lambda i: (i, 0)),
            pl.BlockSpec((tm, D_MODEL), lambda i: (i, 0)),
        ],
        out_shape=[jax.ShapeDtypeStruct((t, D_MODEL), F32), jax.ShapeDtypeStruct((t, D_MODEL), BF16)],
        compiler_params=_cparams(("parallel",)),
        name="ln0",
    )(xp, xs, g.reshape(1, -1), b.reshape(1, -1))


def _ln_kernel(y_ref, g_ref, b_ref, of_ref, ob_ref):
    y = _ln_math(y_ref[...], g_ref[...], b_ref[...])
    of_ref[...] = y
    ob_ref[...] = y.astype(BF16)


def _ln(y, g, b, tm):
    t = y.shape[0]
    row = pl.BlockSpec((tm, D_MODEL), lambda i: (i, 0))
    vec = pl.BlockSpec((1, D_MODEL), lambda i: (0, 0))
    return pl.pallas_call(
        _ln_kernel,
        grid=(t // tm,),
        in_specs=[row, vec, vec],
        out_specs=[row, row],
        out_shape=[jax.ShapeDtypeStruct((t, D_MODEL), F32), jax.ShapeDtypeStruct((t, D_MODEL), BF16)],
        compiler_params=_cparams(("parallel",)),
        name="layer_norm",
    )(y, g.reshape(1, -1), b.reshape(1, -1))


def _ln_final_kernel(y_ref, g_ref, b_ref, op_ref, os_ref, *, n_p):
    i = pl.program_id(0)
    y = _ln_math(y_ref[...], g_ref[...], b_ref[...])

    @pl.when(i < n_p)
    def _():
        op_ref[...] = y

    @pl.when(i >= n_p)
    def _():
        os_ref[...] = y


def _ln_final(y, g, b, tm, tp):
    t = y.shape[0]
    n_p = tp // tm
    row = pl.BlockSpec((tm, D_MODEL), lambda i: (i, 0))
    vec = pl.BlockSpec((1, D_MODEL), lambda i: (0, 0))
    return pl.pallas_call(
        functools.partial(_ln_final_kernel, n_p=n_p),
        grid=(t // tm,),
        in_specs=[row, vec, vec],
        out_specs=[
            pl.BlockSpec((tm, D_MODEL), lambda i: (jnp.minimum(i, n_p - 1), 0)),
            pl.BlockSpec((tm, D_MODEL), lambda i: (jnp.maximum(i - n_p, 0), 0)),
        ],
        out_shape=[jax.ShapeDtypeStruct((tp, D_MODEL), F32), jax.ShapeDtypeStruct((t - tp, D_MODEL), F32)],
        compiler_params=_cparams(("arbitrary",)),
        name="layer_norm_final",
    )(y, g.reshape(1, -1), b.reshape(1, -1))


def _mm_kernel(*refs, widths, has_bias, has_resid):
    n_a = len(widths)
    a_refs = refs[:n_a]
    w_ref = refs[n_a]
    o_ref = refs[-1]
    nxt = n_a + 1
    acc = None
    off = 0
    for a_ref, wd in zip(a_refs, widths):
        part = _dot(a_ref[...].astype(BF16), w_ref[off:off + wd, :].astype(BF16))
        acc = part if acc is None else acc + part
        off += wd
    if has_bias:
        acc = acc + refs[nxt][...]
        nxt += 1
    if has_resid:
        acc = ALPHA * refs[nxt][...] + acc
    o_ref[...] = acc.astype(o_ref.dtype)


def _matmul(a_list, w, bias, out_dtype, tm, tn, name, resid=None, layer=None):
    t = a_list[0].shape[0]
    k, n = w.shape[-2:]
    widths = tuple(a.shape[1] for a in a_list)
    assert sum(widths) == k and t % tm == 0 and n % tn == 0
    in_specs = [pl.BlockSpec((tm, wd), lambda i, j: (i, 0)) for wd in widths]
    if layer is None:
        in_specs.append(pl.BlockSpec((k, tn), lambda i, j: (0, j)))
    else:
        in_specs.append(pl.BlockSpec((None, k, tn), lambda i, j: (layer, 0, j)))
    args = list(a_list) + [w]
    if bias is not None:
        in_specs.append(pl.BlockSpec((1, tn), lambda i, j: (0, j)))
        args.append(bias.reshape(1, n).astype(F32))
    if resid is not None:
        in_specs.append(pl.BlockSpec((tm, tn), lambda i, j: (i, j)))
        args.append(resid)
    return pl.pallas_call(
        functools.partial(_mm_kernel, widths=widths, has_bias=bias is not None, has_resid=resid is not None),
        grid=(t // tm, n // tn),
        in_specs=in_specs,
        out_specs=pl.BlockSpec((tm, tn), lambda i, j: (i, j)),
        out_shape=jax.ShapeDtypeStruct((t, n), out_dtype),
        compiler_params=_cparams(("parallel", "parallel")),
        name=name,
    )(*args)


def _gate_up_kernel(x_ref, wg_ref, wu_ref, o_ref):
    x = x_ref[...]
    g = _dot(x, wg_ref[...].astype(BF16))
    u = _dot(x, wu_ref[...].astype(BF16))
    o_ref[...] = (g * (1.0 / (1.0 + jnp.exp(-g))) * u).astype(o_ref.dtype)


def _gate_up(x, wg, wu, layer, tm, tn):
    t, k = x.shape
    n = wg.shape[-1]
    return pl.pallas_call(
        _gate_up_kernel,
        grid=(t // tm, n // tn),
        in_specs=[
            pl.BlockSpec((tm, k), lambda i, j: (i, 0)),
            pl.BlockSpec((None, k, tn), lambda i, j: (layer, 0, j)),
            pl.BlockSpec((None, k, tn), lambda i, j: (layer, 0, j)),
        ],
        out_specs=pl.BlockSpec((tm, tn), lambda i, j: (i, j)),
        out_shape=jax.ShapeDtypeStruct((t, n), BF16),
        compiler_params=_cparams(("parallel", "parallel")),
        name="ffn_gate_up",
    )(x, wg, wu)


def _attn_kernel(q_ref, kp_ref, kc_ref, kn_ref, vp_ref, vc_ref, vn_ref, bias_ref, sink_ref, o_ref, kbuf, vbuf,
                 *, tq, tiles_p, per_p, per_s):
    g = pl.program_id(0)
    h = pl.program_id(1)
    in_p = g < tiles_p
    n_loc = jnp.where(in_p, g % per_p, (g - tiles_p) % per_s)
    n_seq = jnp.where(in_p, per_p, per_s)
    pen_prev = jnp.where(n_loc > 0, 0.0, -jnp.inf)
    pen_next = jnp.where(n_loc < n_seq - 1, 0.0, -jnp.inf)

    kbuf[0:BLOCK, :] = kp_ref[...].astype(BF16)
    kbuf[BLOCK:BLOCK + tq, :] = kc_ref[...].astype(BF16)
    kbuf[BLOCK + tq:, :] = kn_ref[...].astype(BF16)
    vbuf[:, 0:BLOCK] = vp_ref[...].T.astype(BF16)
    for c0 in range(0, tq, BLOCK):
        vbuf[:, BLOCK + c0:2 * BLOCK + c0] = vc_ref[c0:c0 + BLOCK, :].T.astype(BF16)
    vbuf[:, BLOCK + tq:] = vn_ref[...].T.astype(BF16)

    key = lax.broadcasted_iota(jnp.int32, (3 * BLOCK, BLOCK), 0)
    col_prev = jnp.where(key < BLOCK, pen_prev, 0.0)
    col_next = jnp.where(key >= 2 * BLOCK, pen_next, 0.0)
    scale = HEAD_DIM ** -0.5
    dn = (((1,), (1,)), ((), ()))
    nsb = tq // BLOCK
    for sb in range(nsb):
        r0 = sb * BLOCK
        k3 = kbuf[r0:r0 + 3 * BLOCK, :]
        v3 = vbuf[:, r0:r0 + 3 * BLOCK]
        for gi in range(KV_GROUP):
            sink = sink_ref[h * KV_GROUP + gi]
            q = q_ref[r0:r0 + BLOCK, gi * HEAD_DIM:(gi + 1) * HEAD_DIM].astype(BF16)
            s = lax.dot_general(k3, q, dn, preferred_element_type=F32) * scale
            s = s + bias_ref[gi * 3 * BLOCK:(gi + 1) * 3 * BLOCK, :]
            if sb == 0:
                s = s + col_prev
            if sb == nsb - 1:
                s = s + col_next
            m = jnp.maximum(jnp.max(s, axis=0, keepdims=True), sink)
            p = jnp.exp(s - m)
            denom = jnp.sum(p, axis=0, keepdims=True) + jnp.exp(sink - m)
            o_t = _dot(v3, p.astype(BF16)) * (1.0 / denom)
            o_ref[r0:r0 + BLOCK, gi * HEAD_DIM:(gi + 1) * HEAD_DIM] = o_t.T.astype(o_ref.dtype)


def _attention(proj, sink, tp, lp, ls):
    t = proj.shape[0]
    tq = math.gcd(ATTN_TQ, math.gcd(lp, ls))
    nblk = t // BLOCK
    bpt = tq // BLOCK
    slopes = 2.0 ** (-8.0 * jnp.arange(1, N_HEADS + 1, dtype=F32) / N_HEADS)
    qi = jnp.arange(BLOCK)[:, None]
    ki = jnp.arange(3 * BLOCK)[None, :]
    dist = jnp.abs(qi + BLOCK - ki)
    bias = jnp.where(dist[None] <= BLOCK, -slopes[:, None, None] * dist[None].astype(F32), -jnp.inf)
    bias = jnp.swapaxes(bias, 1, 2).reshape(N_HEADS * 3 * BLOCK, BLOCK)
    qc = OFF_Q // (KV_GROUP * HEAD_DIM)
    kc = OFF_K // HEAD_DIM
    vc = OFF_V // HEAD_DIM
    prev = lambda g: jnp.maximum(g * bpt - 1, 0)
    nxt = lambda g: jnp.minimum((g + 1) * bpt, nblk - 1)
    halo = (BLOCK, HEAD_DIM)
    cur = (tq, HEAD_DIM)
    return pl.pallas_call(
        functools.partial(_attn_kernel, tq=tq, tiles_p=tp // tq, per_p=lp // tq, per_s=ls // tq),
        grid=(t // tq, N_KV_HEADS),
        in_specs=[
            pl.BlockSpec((tq, KV_GROUP * HEAD_DIM), lambda g, h: (g, qc + h)),
            pl.BlockSpec(halo, lambda g, h: (prev(g), kc + h)),
            pl.BlockSpec(cur, lambda g, h: (g, kc + h)),
            pl.BlockSpec(halo, lambda g, h: (nxt(g), kc + h)),
            pl.BlockSpec(halo, lambda g, h: (prev(g), vc + h)),
            pl.BlockSpec(cur, lambda g, h: (g, vc + h)),
            pl.BlockSpec(halo, lambda g, h: (nxt(g), vc + h)),
            pl.BlockSpec((KV_GROUP * 3 * BLOCK, BLOCK), lambda g, h: (h, 0)),
            pl.BlockSpec(memory_space=pltpu.SMEM),
        ],
        out_specs=pl.BlockSpec((tq, KV_GROUP * HEAD_DIM), lambda g, h: (g, h)),
        out_shape=jax.ShapeDtypeStruct((t, ATTN_WIDTH), BF16),
        scratch_shapes=[pltpu.VMEM((tq + 2 * BLOCK, HEAD_DIM), BF16), pltpu.VMEM((HEAD_DIM, tq + 2 * BLOCK), BF16)],
        compiler_params=_cparams(("parallel", "parallel")),
        name="band_attention",
    )(proj, proj, proj, proj, proj, proj, proj, bias, sink.astype(F32))


def _filter_kernel(fvec_ref, w1_ref, b1_ref, w2_ref, b2_ref, w3_ref, b3_ref, fr_ref, w4_ref, delta_ref, o_ref,
                   *, seq, tl):
    r0 = pl.program_id(0) * tl
    row = (lax.broadcasted_iota(jnp.int32, (tl, 128), 0) + r0).astype(F32)
    lane = lax.broadcasted_iota(jnp.int32, (tl, 128), 1)
    t = row * (1.0 / (seq - 1))
    ang = (row * (2.0 * math.pi / seq)) * fvec_ref[...]
    z = jnp.where(lane == 0, t,
                  jnp.where(lane <= 16, jnp.cos(ang), jnp.where(lane <= 32, -jnp.sin(ang), 0.0)))
    hp = lax.Precision.HIGHEST
    h = jnp.sin(fr_ref[0:1, :] * (jnp.dot(z, w1_ref[...], precision=hp, preferred_element_type=F32) + b1_ref[...]))
    h = jnp.sin(fr_ref[1:2, :] * (jnp.dot(h, w2_ref[...], precision=hp, preferred_element_type=F32) + b2_ref[...]))
    h = jnp.sin(fr_ref[2:3, :] * (jnp.dot(h, w3_ref[...], precision=hp, preferred_element_type=F32) + b3_ref[...]))
    trow = (lax.broadcasted_iota(jnp.int32, (tl, HYENA_WIDTH), 0) + r0)
    decay = jnp.exp(-(trow.astype(F32) * (1.0 / (seq - 1))) * delta_ref[...])
    first = trow == 0
    for part in range(4):
        sl = slice(part * HYENA_WIDTH, (part + 1) * HYENA_WIDTH)
        v = jnp.dot(h, w4_ref[:, sl], precision=hp, preferred_element_type=F32) * decay
        if part >= 2:
            v = jnp.where(first, 0.0, v)
        o_ref[:, sl] = v


def _hyena_filters(seq, w1, b1, w2, b2, w3, b3, freq, w4):
    tl = min(seq, 512)
    bands = (FILTER_EMB - 1) // 2
    f = jnp.linspace(1e-4, bands - 1, bands, dtype=F32)
    fvec = jnp.zeros((1, 128), F32).at[0, 1:1 + bands].set(f).at[0, 1 + bands:1 + 2 * bands].set(f)
    w1p = jnp.zeros((128, FILTER_HIDDEN), F32).at[:FILTER_EMB].set(w1.astype(F32))
    max_decay = math.log(DECAY_TARGET) / DECAY_FAST_PCT
    min_decay = math.log(DECAY_TARGET) / DECAY_SLOW_PCT
    deltas = jnp.abs(jnp.linspace(min_decay, max_decay, HYENA_WIDTH, dtype=F32)).reshape(1, -1)
    args = [fvec, w1p, b1.reshape(1, -1).astype(F32), w2.astype(F32), b2.reshape(1, -1).astype(F32),
            w3.astype(F32), b3.reshape(1, -1).astype(F32), freq.astype(F32), w4.astype(F32), deltas]
    return pl.pallas_call(
        functools.partial(_filter_kernel, seq=seq, tl=tl),
        grid=(seq // tl,),
        in_specs=[_full_spec(a) for a in args],
        out_specs=pl.BlockSpec((tl, FILTER_OUT), lambda i: (i, 0)),
        out_shape=jax.ShapeDtypeStruct((seq, FILTER_OUT), F32),
        compiler_params=_cparams(("parallel",)),
        name="hyena_filter",
    )(*args)


def _dotp(mh_ref, ml_ref, x, passes):
    if passes == 1:
        return _dot(mh_ref[...], x.astype(BF16))
    return _dot3_left(mh_ref[...], ml_ref[...], x)


def _gather(ref, t2, n):
    return ref[pl.ds(t2, n, stride=PITCH), :]


def _fill_pitched(u_ref, p_ref, half, w_ref=None, b_ref=None):
    n2 = DFT_N2
    seq = half * n2
    rid = lax.broadcasted_iota(jnp.int32, (n2, LANES), 0)

    def step(t1, carry):
        r0 = pl.multiple_of(t1 * n2, n2)
        cur = u_ref[pl.ds(r0, n2), :]
        if w_ref is not None:
            before = u_ref[pl.ds(pl.multiple_of(jnp.maximum(r0 - 8, 0), 8), 8), :][7:8, :]
            after = u_ref[pl.ds(pl.multiple_of(jnp.minimum(r0 + n2, seq - 8), 8), 8), :][0:1, :]
            before = jnp.where(t1 > 0, before, 0.0)
            after = jnp.where(t1 < half - 1, after, 0.0)
            prev = jnp.where(rid == 0, before, pltpu.roll(cur, 1, 0))
            nxt = jnp.where(rid == n2 - 1, after, pltpu.roll(cur, n2 - 1, 0))
            cur = prev * w_ref[0:1, :] + cur * w_ref[1:2, :] + nxt * w_ref[2:3, :] + b_ref[...]
        p_ref[pl.ds(pl.multiple_of(t1 * PITCH, 8), n2), :] = cur
        return carry

    lax.fori_loop(0, half, step, 0)


def _for_t2_groups(body):
    def step(g, carry):
        body(g * T2_GROUP)
        return carry

    lax.fori_loop(0, DFT_N2 // T2_GROUP, step, 0)


def _outer_stage(p_ref, half, mh_ref, ml_ref, a_ref, nrows, passes):
    def body(base):
        xs = jnp.concatenate([_gather(p_ref, base + j, half) for j in range(T2_GROUP)], axis=1)
        y = _dotp(mh_ref, ml_ref, xs, passes)
        for j in range(T2_GROUP):
            a_ref[pl.ds(base + j, nrows, stride=PITCH), :] = y[:, j * LANES:(j + 1) * LANES]

    _for_t2_groups(body)


def _lane_block(x, j):
    return x[:, j * LANES:(j + 1) * LANES]


def _inner_blocks(a_ref, tw_ref, wstep_ref, nblocks, group_fn, store_fn):
    n2 = DFT_N2
    tw_ref[0:n2, :] = jnp.ones((n2, LANES), F32)
    tw_ref[n2:, :] = jnp.zeros((n2, LANES), F32)

    def run(k0, count):
        twr = tw_ref[0:n2, :]
        twi = tw_ref[n2:, :]
        wr = wstep_ref[0:n2, :]
        wi = wstep_ref[n2:, :]
        ks, rows, tws, blocks = [], [], [], []
        for j in range(count):
            k = k0 + j
            rr = pl.multiple_of(k * 2 * PITCH, 8)
            ri = pl.multiple_of(k * 2 * PITCH + PITCH, 8)
            ar = a_ref[pl.ds(rr, n2), :]
            ai = a_ref[pl.ds(ri, n2), :]
            ks.append(k)
            rows.append((rr, ri))
            tws.append((twr, twi))
            blocks.append(jnp.concatenate([ar * twr - ai * twi, ar * twi + ai * twr], axis=0))
            twr, twi = twr * wr - twi * wi, twr * wi + twi * wr
        vals = group_fn(ks, jnp.concatenate(blocks, axis=1), tws)
        for k, (rr, ri), val in zip(ks, rows, vals):
            store_fn(k, rr, ri, val)
        tw_ref[0:n2, :] = twr
        tw_ref[n2:, :] = twi

    def step(g, carry):
        run(g * INNER_GROUP, INNER_GROUP)
        return carry

    lax.fori_loop(0, nblocks // INNER_GROUP, step, 0)
    if nblocks % INNER_GROUP:
        run(nblocks - nblocks % INNER_GROUP, nblocks % INNER_GROUP)


def _hyena_conv_kernel(sig_ref, gate_ref, wsig_ref, bsig_ref, wgate_ref, bgate_ref, skip_ref, kf_ref,
                       f1h_ref, f1l_ref, g3h_ref, g3l_ref, gch_ref, gcl_ref, gbh_ref, gbl_ref, wstep_ref,
                       o_ref, a_ref, tw_ref, pv_ref, pg_ref, *, half, k1, k1p, sig_conv):
    n2 = DFT_N2
    if sig_conv:
        _fill_pitched(sig_ref, pv_ref, half, wsig_ref, bsig_ref)
    else:
        _fill_pitched(sig_ref, pv_ref, half)
    _fill_pitched(gate_ref, pg_ref, half, wgate_ref, bgate_ref)
    _outer_stage(pv_ref, half, f1h_ref, f1l_ref, a_ref, 2 * k1p, CONV_PASSES)

    def block(ks, xcat, tws):
        x = _dotp(gch_ref, gcl_ref, xcat, CONV_PASSES)
        prods = []
        for j, k in enumerate(ks):
            xr, xi = _lane_block(x[:n2], j), _lane_block(x[n2:], j)
            k0 = pl.multiple_of(k * 2 * n2, 2 * n2)
            kr = kf_ref[pl.ds(k0, n2), :]
            ki = kf_ref[pl.ds(k0 + n2, n2), :]
            prods.append(jnp.concatenate([xr * kr - xi * ki, xr * ki + xi * kr], axis=0))
        bb = _dotp(gbh_ref, gbl_ref, jnp.concatenate(prods, axis=1), CONV_PASSES)
        out = []
        for j, (twr, twi) in enumerate(tws):
            br, bi = _lane_block(bb[:n2], j), _lane_block(bb[n2:], j)
            out.append((br * twr + bi * twi, bi * twr - br * twi))
        return out

    def put(k, rr, ri, vals):
        a_ref[pl.ds(rr, n2), :] = vals[0]
        a_ref[pl.ds(ri, n2), :] = vals[1]

    _inner_blocks(a_ref, tw_ref, wstep_ref, k1, block, put)

    def finish(base):
        bcat = jnp.concatenate([_gather(a_ref, base + j, 2 * k1p) for j in range(T2_GROUP)], axis=1)
        y = _dotp(g3h_ref, g3l_ref, bcat, CONV_PASSES)
        for j in range(T2_GROUP):
            v = _gather(pv_ref, base + j, half)
            gate = _gather(pg_ref, base + j, half)
            pg_ref[pl.ds(base + j, half, stride=PITCH), :] = gate * (y[:, j * LANES:(j + 1) * LANES]
                                                                    + v * skip_ref[...])

    _for_t2_groups(finish)

    def emit(t1, carry):
        o_ref[pl.ds(pl.multiple_of(t1 * n2, n2), n2), :] = pg_ref[pl.ds(pl.multiple_of(t1 * PITCH, 8), n2), :]
        return carry

    lax.fori_loop(0, half, emit, 0)


def _hyena_spec_kernel(ff_ref, fb_ref, f1h_ref, f1l_ref, gch_ref, gcl_ref, wstep_ref, o_ref, a_ref, tw_ref, pf_ref,
                       *, half, k1, k1p):
    n2 = DFT_N2
    _fill_pitched(ff_ref, pf_ref, half)
    _outer_stage(pf_ref, half, f1h_ref, f1l_ref, a_ref, 2 * k1p, SPEC_PASSES)

    def spectrum(ks, xcat, tws):
        x = _dotp(gch_ref, gcl_ref, xcat, SPEC_PASSES)
        return [_lane_block(x, j) for j in range(len(ks))]

    def put_fwd(k, rr, ri, x):
        o_ref[pl.ds(pl.multiple_of(k * 2 * n2, 2 * n2), 2 * n2), :] = x

    _inner_blocks(a_ref, tw_ref, wstep_ref, k1, spectrum, put_fwd)
    _fill_pitched(fb_ref, pf_ref, half)
    _outer_stage(pf_ref, half, f1h_ref, f1l_ref, a_ref, 2 * k1p, SPEC_PASSES)

    def add_bwd(k, rr, ri, x):
        k0 = pl.multiple_of(k * 2 * n2, 2 * n2)
        o_ref[pl.ds(k0, n2), :] = o_ref[pl.ds(k0, n2), :] + x[:n2]
        o_ref[pl.ds(k0 + n2, n2), :] = o_ref[pl.ds(k0 + n2, n2), :] - x[n2:]

    _inner_blocks(a_ref, tw_ref, wstep_ref, k1, spectrum, add_bwd)


def _conv_tables(seq):
    n = 2 * seq
    n2 = DFT_N2
    n1 = n // n2
    k1 = n1 // 2 + 1
    k1p = -(-k1 // 8) * 8
    half = n1 // 2
    kk = jnp.arange(k1p, dtype=jnp.int32)
    valid = (kk < k1)
    t1 = jnp.arange(half, dtype=jnp.int32)
    c, s = _cs(kk[:, None] * t1[None, :], n1)
    vm = valid[:, None].astype(F32)
    f1 = jnp.stack([c * vm, -s * vm], axis=1).reshape(2 * k1p, half)
    wgt = jnp.where((kk == 0) | (kk == n1 // 2), 1.0, 2.0) * valid.astype(F32) / n
    g3 = jnp.stack([c * vm * wgt[:, None], -s * vm * wgt[:, None]], axis=1).reshape(2 * k1p, half).T
    j = jnp.arange(n2, dtype=jnp.int32)
    cr, cs_ = _cs(j[:, None] * j[None, :], n2)
    gc = jnp.concatenate([jnp.concatenate([cr, cs_], axis=1), jnp.concatenate([-cs_, cr], axis=1)], axis=0)
    gb = jnp.concatenate([jnp.concatenate([cr, -cs_], axis=1), jnp.concatenate([cs_, cr], axis=1)], axis=0)
    wr, ws = _cs(j, n)
    wstep = jnp.concatenate([jnp.broadcast_to(wr[:, None], (n2, LANES)),
                             jnp.broadcast_to(-ws[:, None], (n2, LANES))], axis=0)
    tabs = dict(n1=n1, k1=k1, k1p=k1p, half=half, wstep=wstep)
    for name, m in (("f1", f1), ("g3", g3), ("gc", gc), ("gb", gb)):
        tabs[name + "h"], tabs[name + "l"] = _split_const(m)
    return tabs


def _filter_spectrum(seq, tabs, fw):
    filt = _hyena_filters(seq, *fw)
    half, k1, k1p = tabs["half"], tabs["k1"], tabs["k1p"]
    n2 = DFT_N2
    nct = HYENA_WIDTH // LANES
    consts = [tabs[n] for n in ("f1h", "f1l", "gch", "gcl", "wstep")]
    return pl.pallas_call(
        functools.partial(_hyena_spec_kernel, half=half, k1=k1, k1p=k1p),
        grid=(2, nct),
        in_specs=[pl.BlockSpec((seq, LANES), lambda o, j: (0, o * nct + j)),
                  pl.BlockSpec((seq, LANES), lambda o, j: (0, (2 + o) * nct + j))] + [_full_spec(a) for a in consts],
        out_specs=pl.BlockSpec((None, k1 * 2 * n2, LANES), lambda o, j: (o, 0, j)),
        out_shape=jax.ShapeDtypeStruct((2, k1 * 2 * n2, HYENA_WIDTH), F32),
        scratch_shapes=[pltpu.VMEM((k1p * 2 * PITCH, LANES), F32), pltpu.VMEM((2 * n2, LANES), F32),
                        pltpu.VMEM((half * PITCH, LANES), F32)],
        compiler_params=_cparams(("parallel", "parallel")),
        name="hyena_filter_spectrum",
    )(filt, filt, *consts)


def _skip_first_ref(kernel_fn, *refs, **kwargs):
    return kernel_fn(*refs[1:], **kwargs)


def _merged_out(kernel_fn, args, in_specs, into, total_rows, width, dtype):
    out_shape = jax.ShapeDtypeStruct((total_rows, width), dtype)
    if into is None:
        return kernel_fn, args, in_specs, out_shape, {}
    return (functools.partial(_skip_first_ref, kernel_fn), [into] + args,
            [pl.BlockSpec(memory_space=pl.ANY)] + in_specs, out_shape, {0: 0})


def _hyena_conv(sig, sig_cols, gate_cols, proj, row0, nbatch, seq, tabs, kf, order, short_w, short_b, skip,
                total_rows=None, into=None):
    assert row0 % seq == 0
    b0 = row0 // seq
    half, k1, k1p = tabs["half"], tabs["k1"], tabs["k1p"]
    n2 = DFT_N2
    nct = HYENA_WIDTH // LANES
    sig_conv = sig is None
    if sig_conv:
        sig_arr = proj
        sig_spec = pl.BlockSpec((seq, LANES), lambda j, b: (b0 + b, sig_cols + j))
    else:
        sig_arr = sig
        sig_spec = pl.BlockSpec((seq, LANES), lambda j, b: (b, j))
    sw = short_w.astype(F32)
    sb = short_b.reshape(1, -1).astype(F32)
    consts = [tabs[n] for n in ("f1h", "f1l", "g3h", "g3l", "gch", "gcl", "gbh", "gbl", "wstep")]
    args = [sig_arr, proj, sw, sb, sw, sb, skip.reshape(1, -1).astype(F32), kf] + consts
    in_specs = [
        sig_spec,
        pl.BlockSpec((seq, LANES), lambda j, b: (b0 + b, gate_cols + j), pipeline_mode=pl.Buffered(1)),
        pl.BlockSpec((3, LANES), lambda j, b: (0, sig_cols + j)),
        pl.BlockSpec((1, LANES), lambda j, b: (0, sig_cols + j)),
        pl.BlockSpec((3, LANES), lambda j, b: (0, gate_cols + j)),
        pl.BlockSpec((1, LANES), lambda j, b: (0, gate_cols + j)),
        pl.BlockSpec((1, LANES), lambda j, b: (0, j)),
        pl.BlockSpec((None, k1 * 2 * n2, LANES), lambda j, b: (order, 0, j), pipeline_mode=pl.Buffered(1)),
    ] + [_full_spec(a) for a in consts]
    kern = functools.partial(_hyena_conv_kernel, half=half, k1=k1, k1p=k1p, sig_conv=sig_conv)
    out_b0 = 0 if total_rows is None else b0
    kern, args, in_specs, out_shape, aliases = _merged_out(
        kern, args, in_specs, into, total_rows or nbatch * seq, HYENA_WIDTH, F32)
    return pl.pallas_call(
        kern,
        grid=(nct, nbatch),
        in_specs=in_specs,
        out_specs=pl.BlockSpec((seq, LANES), lambda j, b: (out_b0 + b, j)),
        out_shape=out_shape,
        input_output_aliases=aliases,
        scratch_shapes=[pltpu.VMEM((k1p * 2 * PITCH, LANES), F32), pltpu.VMEM((2 * n2, LANES), F32),
                        pltpu.VMEM((half * PITCH, LANES), F32), pltpu.VMEM((half * PITCH, LANES), F32)],
        compiler_params=_cparams(("parallel", "arbitrary")),
        name="hyena_conv",
    )(*args)


def _hyena_batch(proj, row0, nbatch, seq, tabs, kf, short_w, short_b, skip, total_rows, into):
    nct = HYENA_WIDTH // LANES
    z = _hyena_conv(None, 2 * nct, 0, proj, row0, nbatch, seq, tabs, kf, 0, short_w, short_b, skip[0])
    return _hyena_conv(z, 2 * nct, nct, proj, row0, nbatch, seq, tabs, kf, 1, short_w, short_b, skip[1],
                       total_rows=total_rows, into=into)


def _fnet_kernel(u_ref, chan_ref, m1_ref, gri_ref, wstep_ref, o_ref, zr_ref, zi_ref, a_ref, tw_ref, *, n1):
    n2 = DFT_N2
    slabs = 4

    def chan(g, carry):
        x = u_ref[pl.ds(pl.multiple_of(g * slabs * n2, slabs * n2), slabs * n2), :].astype(BF16)
        z = _dot(x, chan_ref[...])
        for i in range(slabs):
            r = pl.multiple_of((g * slabs + i) * PITCH, 8)
            zr_ref[pl.ds(r, n2), :] = z[i * n2:(i + 1) * n2, :LANES]
            zi_ref[pl.ds(r, n2), :] = z[i * n2:(i + 1) * n2, LANES:]
        return carry

    lax.fori_loop(0, n1 // slabs, chan, 0)

    def outer(base):
        xs = jnp.concatenate(
            [jnp.concatenate([_gather(zr_ref, base + j, n1), _gather(zi_ref, base + j, n1)], axis=0)
             for j in range(T2_GROUP)], axis=1)
        y = _dot(m1_ref[...], xs.astype(BF16))
        for j in range(T2_GROUP):
            a_ref[pl.ds(base + j, 2 * n1, stride=PITCH), :] = y[:, j * LANES:(j + 1) * LANES]

    _for_t2_groups(outer)

    def real_part(ks, xcat, tws):
        y = _dot(gri_ref[...], xcat.astype(BF16))
        return [_lane_block(y, j) for j in range(len(ks))]

    def put(k, rr, ri, y):
        zr_ref[pl.ds(pl.multiple_of(k * PITCH, 8), n2), :] = y

    _inner_blocks(a_ref, tw_ref, wstep_ref, n1, real_part, put)

    def emit(k2, carry):
        o_ref[pl.ds(pl.multiple_of(k2 * n1, n1), n1), :] = _gather(zr_ref, k2, n1).astype(o_ref.dtype)
        return carry

    lax.fori_loop(0, n2, emit, 0)


def _fnet_tables(seq):
    n2 = DFT_N2
    n1 = seq // n2
    j = jnp.arange(HEAD_DIM, dtype=jnp.int32)
    c, s = _cs(j[:, None] * j[None, :], HEAD_DIM)
    scale = (seq * HEAD_DIM) ** -0.5
    chan = jnp.concatenate([c * scale, -s * scale], axis=1).astype(BF16)
    kk = jnp.arange(n1, dtype=jnp.int32)
    c1, s1 = _cs(kk[:, None] * kk[None, :], n1)
    m1 = jnp.stack([jnp.concatenate([c1, s1], axis=1), jnp.concatenate([-s1, c1], axis=1)], axis=1)
    m1 = m1.reshape(2 * n1, 2 * n1).astype(BF16)
    t2 = jnp.arange(n2, dtype=jnp.int32)
    cr, cs_ = _cs(t2[:, None] * t2[None, :], n2)
    gri = jnp.concatenate([cr, cs_], axis=1).astype(BF16)
    wr, ws = _cs(t2, seq)
    wstep = jnp.concatenate([jnp.broadcast_to(wr[:, None], (n2, LANES)),
                             jnp.broadcast_to(-ws[:, None], (n2, LANES))], axis=0)
    return dict(n1=n1, chan=chan, m1=m1, gri=gri, wstep=wstep)


def _fnet_batch(proj, row0, nbatch, seq, tabs, total_rows, into):
    assert row0 % seq == 0
    b0 = row0 // seq
    n1 = tabs["n1"]
    consts = [tabs[n] for n in ("chan", "m1", "gri", "wstep")]
    in_specs = ([pl.BlockSpec((seq, LANES), lambda j, b: (b0 + b, OFF_FNET // LANES + j))]
                + [_full_spec(a) for a in consts])
    kern, args, in_specs, out_shape, aliases = _merged_out(
        functools.partial(_fnet_kernel, n1=n1), [proj] + consts, in_specs, into, total_rows, FNET_WIDTH, BF16)
    return pl.pallas_call(
        kern,
        grid=(FNET_HEADS, nbatch),
        in_specs=in_specs,
        out_specs=pl.BlockSpec((seq, LANES), lambda j, b: (b0 + b, j)),
        out_shape=out_shape,
        input_output_aliases=aliases,
        scratch_shapes=[pltpu.VMEM((n1 * PITCH, LANES), F32), pltpu.VMEM((n1 * PITCH, LANES), F32),
                        pltpu.VMEM((2 * n1 * PITCH, LANES), F32), pltpu.VMEM((2 * DFT_N2, LANES), F32)],
        compiler_params=_cparams(("parallel", "parallel")),
        name="fnet_mixer",
    )(*args)


def _pick_tile(t, pref):
    while t % pref:
        pref //= 2
    return pref


def kernel(x_prompt, x_sample, ln0_g, ln0_b, w_in, short_w, short_b, filt_w1, filt_b1, filt_w2, filt_b2, filt_w3, filt_b3, filt_freq, filt_w4, hyena_skip, w_fnet, b_fnet, attn_sink, w_out, ln1_g, ln1_b, w_gate, w_up, w_down, ln2_g, ln2_b):
    bp, lp, _ = x_prompt.shape
    bs, ls, _ = x_sample.shape
    tp, ts = bp * lp, bs * ls
    batches = ((0, bp, lp), (tp, bs, ls))
    tm = _pick_tile(math.gcd(tp, ts), 1024)
    tln = _pick_tile(math.gcd(tp, ts), 256)

    conv_tabs = {seq: _conv_tables(seq) for seq in {lp, ls}}
    fnet_tabs = {seq: _fnet_tables(seq) for seq in {lp, ls}}

    xf, xb = _ln0(x_prompt.reshape(tp, D_MODEL), x_sample.reshape(ts, D_MODEL), ln0_g, ln0_b, tln)
    for l in range(DEPTH):
        wd_b = w_down[l].astype(BF16)
        fw = (filt_w1[l], filt_b1[l], filt_w2[l], filt_b2[l], filt_w3[l], filt_b3[l], filt_freq[l], filt_w4[l])

        proj = _matmul([xb], w_in, None, F32, tm, 512, "in_proj", layer=l)

        kf = {seq: _filter_spectrum(seq, conv_tabs[seq], fw) for seq in {lp, ls}}
        y_h = y_f = None
        for r0, nb, seq in batches:
            y_h = _hyena_batch(proj, r0, nb, seq, conv_tabs[seq], kf[seq], short_w[l], short_b[l], hyena_skip[l],
                               tp + ts, y_h)
            y_f = _fnet_batch(proj, r0, nb, seq, fnet_tabs[seq], tp + ts, y_f)
        y_f = _matmul([y_f], w_fnet, b_fnet[l], BF16, tm, 1024, "fnet_linear", layer=l)
        y_a = _attention(proj, attn_sink[l], tp, lp, ls)

        y = _matmul([y_h, y_f, y_a], w_out, None, F32, tm, 512, "out_proj", resid=xf, layer=l)
        xf, xb = _ln(y, ln1_g[l], ln1_b[l], tln)

        hid = _gate_up(xb, w_gate, w_up, l, _pick_tile(math.gcd(tp, ts), 2048), FF_TILE)
        y = _matmul([hid], wd_b, None, F32, _pick_tile(tm, 512), 512, "ffn_down", resid=xf)
        if l + 1 < DEPTH:
            xf, xb = _ln(y, ln2_g[l], ln2_b[l], tln)
        else:
            y_p, y_s = _ln_final(y, ln2_g[l], ln2_b[l], tln, tp)
    return y_p.reshape(bp, lp, D_MODEL), y_s.reshape(bs, ls, D_MODEL)
```

```python
import functools
import math

import jax
import jax.numpy as jnp
from jax import lax
from jax.experimental import pallas as pl
from jax.experimental.pallas import tpu as pltpu

F32 = jnp.float32
BF16 = jnp.bfloat16

D_MODEL = 4096
HEAD_DIM = 128
HYENA_WIDTH = 1024
FNET_WIDTH = 1024
ATTN_WIDTH = 2048
FNET_HEADS = 8
N_HEADS = 16
N_KV_HEADS = 4
KV_GROUP = 4
KV_WIDTH = 512
BLOCK = 128
HYENA_IN = 3 * HYENA_WIDTH
FILTER_EMB = 33
FILTER_HIDDEN = 64
FILTER_OUT = 4 * HYENA_WIDTH
OFF_FNET = HYENA_IN
OFF_Q = OFF_FNET + FNET_WIDTH
OFF_K = OFF_Q + ATTN_WIDTH
OFF_V = OFF_K + KV_WIDTH
IN_WIDTH = OFF_V + KV_WIDTH
D_FF = 11008
FF_TILE = 256
DEPTH = 2
ALPHA = (2 * DEPTH) ** 0.25
LN_EPS = 1e-5
DECAY_FAST_PCT = 0.3
DECAY_SLOW_PCT = 1.5
DECAY_TARGET = 1e-2

DFT_N2 = 128
LANES = 128
T2_GROUP = 8
PITCH = 136
CONV_PASSES = 1
SPEC_PASSES = 1
INNER_GROUP = 8
ATTN_TQ = 512
VMEM_LIMIT = 56 * 1024 * 1024


def _cparams(sem, vmem=VMEM_LIMIT):
    return pltpu.CompilerParams(dimension_semantics=sem, vmem_limit_bytes=vmem)


def _dot(a, b):
    return jnp.dot(a, b, preferred_element_type=F32)


def _split(x):
    hi = x.astype(BF16)
    lo = (x - hi.astype(F32)).astype(BF16)
    return hi, lo


def _dot3_left(m_hi, m_lo, x):
    x_hi, x_lo = _split(x)
    return _dot(m_hi, x_hi) + (_dot(m_hi, x_lo) + _dot(m_lo, x_hi))


def _split_const(m):
    m = m.astype(F32)
    hi = m.astype(BF16)
    lo = (m - hi.astype(F32)).astype(BF16)
    return hi, lo


def _cs(num, den):
    ang = (2.0 * math.pi / den) * (num % den).astype(F32)
    return jnp.cos(ang), jnp.sin(ang)


def _full_spec(a):
    return pl.BlockSpec(a.shape, lambda *_: (0,) * a.ndim)


def _ln_math(x, g, b):
    mu = jnp.mean(x, axis=-1, keepdims=True)
    xc = x - mu
    var = jnp.mean(xc * xc, axis=-1, keepdims=True)
    return xc * lax.rsqrt(var + LN_EPS) * g + b


def _ln0_kernel(xp_ref, xs_ref, g_ref, b_ref, of_ref, ob_ref, *, n_p):
    i = pl.program_id(0)

    def emit(x):
        y = _ln_math(x, g_ref[...], b_ref[...])
        of_ref[...] = y
        ob_ref[...] = y.astype(BF16)

    @pl.when(i < n_p)
    def _():
        emit(xp_ref[...])

    @pl.when(i >= n_p)
    def _():
        emit(xs_ref[...])


def _ln0(xp, xs, g, b, tm):
    tp, ts = xp.shape[0], xs.shape[0]
    n_p, n_s = tp // tm, ts // tm
    t = tp + ts
    return pl.pallas_call(
        functools.partial(_ln0_kernel, n_p=n_p),
        grid=(n_p + n_s,),
        in_specs=[
            pl.BlockSpec((tm, D_MODEL), lambda i: (jnp.minimum(i, n_p - 1), 0)),
            pl.BlockSpec((tm, D_MODEL), lambda i: (jnp.maximum(i - n_p, 0), 0)),
            pl.BlockSpec((1, D_MODEL), lambda i: (0, 0)),
            pl.BlockSpec((1, D_MODEL), lambda i: (0, 0)),
        ],
        out_specs=[
            pl.BlockSpec((tm, D_MODEL), lambda i: (i, 0)),
            pl.BlockSpec((tm, D_MODEL), lambda i: (i, 0)),
        ],
        out_shape=[jax.ShapeDtypeStruct((t, D_MODEL), F32), jax.ShapeDtypeStruct((t, D_MODEL), BF16)],
        compiler_params=_cparams(("parallel",)),
        name="ln0",
    )(xp, xs, g.reshape(1, -1), b.reshape(1, -1))


def _ln_kernel(y_ref, g_ref, b_ref, of_ref, ob_ref):
    y = _ln_math(y_ref[...], g_ref[...], b_ref[...])
    of_ref[...] = y
    ob_ref[...] = y.astype(BF16)


def _ln(y, g, b, tm):
    t = y.shape[0]
    row = pl.BlockSpec((tm, D_MODEL), lambda i: (i, 0))
    vec = pl.BlockSpec((1, D_MODEL), lambda i: (0, 0))
    return pl.pallas_call(
        _ln_kernel,
        grid=(t // tm,),
        in_specs=[row, vec, vec],
        out_specs=[row, row],
        out_shape=[jax.ShapeDtypeStruct((t, D_MODEL), F32), jax.ShapeDtypeStruct((t, D_MODEL), BF16)],
        compiler_params=_cparams(("parallel",)),
        name="layer_norm",
    )(y, g.reshape(1, -1), b.reshape(1, -1))


def _ln_final_kernel(y_ref, g_ref, b_ref, op_ref, os_ref, *, n_p):
    i = pl.program_id(0)
    y = _ln_math(y_ref[...], g_ref[...], b_ref[...])

    @pl.when(i < n_p)
    def _():
        op_ref[...] = y

    @pl.when(i >= n_p)
    def _():
        os_ref[...] = y


def _ln_final(y, g, b, tm, tp):
    t = y.shape[0]
    n_p = tp // tm
    row = pl.BlockSpec((tm, D_MODEL), lambda i: (i, 0))
    vec = pl.BlockSpec((1, D_MODEL), lambda i: (0, 0))
    return pl.pallas_call(
        functools.partial(_ln_final_kernel, n_p=n_p),
        grid=(t // tm,),
        in_specs=[row, vec, vec],
        out_specs=[
            pl.BlockSpec((tm, D_MODEL), lambda i: (jnp.minimum(i, n_p - 1), 0)),
            pl.BlockSpec((tm, D_MODEL), lambda i: (jnp.maximum(i - n_p, 0), 0)),
        ],
        out_shape=[jax.ShapeDtypeStruct((tp, D_MODEL), F32), jax.ShapeDtypeStruct((t - tp, D_MODEL), F32)],
        compiler_params=_cparams(("arbitrary",)),
        name="layer_norm_final",
    )(y, g.reshape(1, -1), b.reshape(1, -1))


def _mm_kernel(*refs, widths, has_bias, has_resid):
    n_a = len(widths)
    a_refs = refs[:n_a]
    w_ref = refs[n_a]
    o_ref = refs[-1]
    nxt = n_a + 1
    acc = None
    off = 0
    for a_ref, wd in zip(a_refs, widths):
        part = _dot(a_ref[...].astype(BF16), w_ref[off:off + wd, :].astype(BF16))
        acc = part if acc is None else acc + part
        off += wd
    if has_bias:
        acc = acc + refs[nxt][...]
        nxt += 1
    if has_resid:
        acc = ALPHA * refs[nxt][...] + acc
    o_ref[...] = acc.astype(o_ref.dtype)


def _matmul(a_list, w, bias, out_dtype, tm, tn, name, resid=None, layer=None):
    t = a_list[0].shape[0]
    k, n = w.shape[-2:]
    widths = tuple(a.shape[1] for a in a_list)
    assert sum(widths) == k and t % tm == 0 and n % tn == 0
    in_specs = [pl.BlockSpec((tm, wd), lambda i, j: (i, 0)) for wd in widths]
    if layer is None:
        in_specs.append(pl.BlockSpec((k, tn), lambda i, j: (0, j)))
    else:
        in_specs.append(pl.BlockSpec((None, k, tn), lambda i, j: (layer, 0, j)))
    args = list(a_list) + [w]
    if bias is not None:
        in_specs.append(pl.BlockSpec((1, tn), lambda i, j: (0, j)))
        args.append(bias.reshape(1, n).astype(F32))
    if resid is not None:
        in_specs.append(pl.BlockSpec((tm, tn), lambda i, j: (i, j)))
        args.append(resid)
    return pl.pallas_call(
        functools.partial(_mm_kernel, widths=widths, has_bias=bias is not None, has_resid=resid is not None),
        grid=(t // tm, n // tn),
        in_specs=in_specs,
        out_specs=pl.BlockSpec((tm, tn), lambda i, j: (i, j)),
        out_shape=jax.ShapeDtypeStruct((t, n), out_dtype),
        compiler_params=_cparams(("parallel", "parallel")),
        name=name,
    )(*args)


def _gate_up_kernel(x_ref, wg_ref, wu_ref, o_ref):
    x = x_ref[...]
    g = _dot(x, wg_ref[...].astype(BF16))
    u = _dot(x, wu_ref[...].astype(BF16))
    o_ref[...] = (g * (1.0 / (1.0 + jnp.exp(-g))) * u).astype(o_ref.dtype)


def _gate_up(x, wg, wu, layer, tm, tn):
    t, k = x.shape
    n = wg.shape[-1]
    return pl.pallas_call(
        _gate_up_kernel,
        grid=(t // tm, n // tn),
        in_specs=[
            pl.BlockSpec((tm, k), lambda i, j: (i, 0)),
            pl.BlockSpec((None, k, tn), lambda i, j: (layer, 0, j)),
            pl.BlockSpec((None, k, tn), lambda i, j: (layer, 0, j)),
        ],
        out_specs=pl.BlockSpec((tm, tn), lambda i, j: (i, j)),
        out_shape=jax.ShapeDtypeStruct((t, n), BF16),
        compiler_params=_cparams(("parallel", "parallel")),
        name="ffn_gate_up",
    )(x, wg, wu)


def _attn_kernel(q_ref, kp_ref, kc_ref, kn_ref, vp_ref, vc_ref, vn_ref, bias_ref, sink_ref, o_ref, kbuf, vbuf,
                 *, tq, tiles_p, per_p, per_s):
    g = pl.program_id(0)
    h = pl.program_id(1)
    in_p = g < tiles_p
    n_loc = jnp.where(in_p, g % per_p, (g - tiles_p) % per_s)
    n_seq = jnp.where(in_p, per_p, per_s)
    pen_prev = jnp.where(n_loc > 0, 0.0, -jnp.inf)
    pen_next = jnp.where(n_loc < n_seq - 1, 0.0, -jnp.inf)

    kbuf[0:BLOCK, :] = kp_ref[...].astype(BF16)
    kbuf[BLOCK:BLOCK + tq, :] = kc_ref[...].astype(BF16)
    kbuf[BLOCK + tq:, :] = kn_ref[...].astype(BF16)
    vbuf[:, 0:BLOCK] = vp_ref[...].T.astype(BF16)
    for c0 in range(0, tq, BLOCK):
        vbuf[:, BLOCK + c0:2 * BLOCK + c0] = vc_ref[c0:c0 + BLOCK, :].T.astype(BF16)
    vbuf[:, BLOCK + tq:] = vn_ref[...].T.astype(BF16)

    key = lax.broadcasted_iota(jnp.int32, (3 * BLOCK, BLOCK), 0)
    col_prev = jnp.where(key < BLOCK, pen_prev, 0.0)
    col_next = jnp.where(key >= 2 * BLOCK, pen_next, 0.0)
    scale = HEAD_DIM ** -0.5
    dn = (((1,), (1,)), ((), ()))
    nsb = tq // BLOCK
    for sb in range(nsb):
        r0 = sb * BLOCK
        k3 = kbuf[r0:r0 + 3 * BLOCK, :]
        v3 = vbuf[:, r0:r0 + 3 * BLOCK]
        for gi in range(KV_GROUP):
            sink = sink_ref[h * KV_GROUP + gi]
            q = q_ref[r0:r0 + BLOCK, gi * HEAD_DIM:(gi + 1) * HEAD_DIM].astype(BF16)
            s = lax.dot_general(k3, q, dn, preferred_element_type=F32) * scale
            s = s + bias_ref[gi * 3 * BLOCK:(gi + 1) * 3 * BLOCK, :]
            if sb == 0:
                s = s + col_prev
            if sb == nsb - 1:
                s = s + col_next
            m = jnp.maximum(jnp.max(s, axis=0, keepdims=True), sink)
            p = jnp.exp(s - m)
            denom = jnp.sum(p, axis=0, keepdims=True) + jnp.exp(sink - m)
            o_t = _dot(v3, p.astype(BF16)) * (1.0 / denom)
            o_ref[r0:r0 + BLOCK, gi * HEAD_DIM:(gi + 1) * HEAD_DIM] = o_t.T.astype(o_ref.dtype)


def _attention(proj, sink, tp, lp, ls):
    t = proj.shape[0]
    tq = math.gcd(ATTN_TQ, math.gcd(lp, ls))
    nblk = t // BLOCK
    bpt = tq // BLOCK
    slopes = 2.0 ** (-8.0 * jnp.arange(1, N_HEADS + 1, dtype=F32) / N_HEADS)
    qi = jnp.arange(BLOCK)[:, None]
    ki = jnp.arange(3 * BLOCK)[None, :]
    dist = jnp.abs(qi + BLOCK - ki)
    bias = jnp.where(dist[None] <= BLOCK, -slopes[:, None, None] * dist[None].astype(F32), -jnp.inf)
    bias = jnp.swapaxes(bias, 1, 2).reshape(N_HEADS * 3 * BLOCK, BLOCK)
    qc = OFF_Q // (KV_GROUP * HEAD_DIM)
    kc = OFF_K // HEAD_DIM
    vc = OFF_V // HEAD_DIM
    prev = lambda g: jnp.maximum(g * bpt - 1, 0)
    nxt = lambda g: jnp.minimum((g + 1) * bpt, nblk - 1)
    halo = (BLOCK, HEAD_DIM)
    cur = (tq, HEAD_DIM)
    return pl.pallas_call(
        functools.partial(_attn_kernel, tq=tq, tiles_p=tp // tq, per_p=lp // tq, per_s=ls // tq),
        grid=(t // tq, N_KV_HEADS),
        in_specs=[
            pl.BlockSpec((tq, KV_GROUP * HEAD_DIM), lambda g, h: (g, qc + h)),
            pl.BlockSpec(halo, lambda g, h: (prev(g), kc + h)),
            pl.BlockSpec(cur, lambda g, h: (g, kc + h)),
            pl.BlockSpec(halo, lambda g, h: (nxt(g), kc + h)),
            pl.BlockSpec(halo, lambda g, h: (prev(g), vc + h)),
            pl.BlockSpec(cur, lambda g, h: (g, vc + h)),
            pl.BlockSpec(halo, lambda g, h: (nxt(g), vc + h)),
            pl.BlockSpec((KV_GROUP * 3 * BLOCK, BLOCK), lambda g, h: (h, 0)),
            pl.BlockSpec(memory_space=pltpu.SMEM),
        ],
        out_specs=pl.BlockSpec((tq, KV_GROUP * HEAD_DIM), lambda g, h: (g, h)),
        out_shape=jax.ShapeDtypeStruct((t, ATTN_WIDTH), BF16),
        scratch_shapes=[pltpu.VMEM((tq + 2 * BLOCK, HEAD_DIM), BF16), pltpu.VMEM((HEAD_DIM, tq + 2 * BLOCK), BF16)],
        compiler_params=_cparams(("parallel", "parallel")),
        name="band_attention",
    )(proj, proj, proj, proj, proj, proj, proj, bias, sink.astype(F32))


def _filter_kernel(fvec_ref, w1_ref, b1_ref, w2_ref, b2_ref, w3_ref, b3_ref, fr_ref, w4_ref, delta_ref, o_ref,
                   *, seq, tl):
    r0 = pl.program_id(0) * tl
    row = (lax.broadcasted_iota(jnp.int32, (tl, 128), 0) + r0).astype(F32)
    lane = lax.broadcasted_iota(jnp.int32, (tl, 128), 1)
    t = row * (1.0 / (seq - 1))
    ang = (row * (2.0 * math.pi / seq)) * fvec_ref[...]
    z = jnp.where(lane == 0, t,
                  jnp.where(lane <= 16, jnp.cos(ang), jnp.where(lane <= 32, -jnp.sin(ang), 0.0)))
    def dot3(x, w_ref, cols=slice(None)):
        x_hi, x_lo = _split(x)
        w_hi, w_lo = w_ref[0, :, cols], w_ref[1, :, cols]
        return _dot(x_hi, w_hi) + (_dot(x_lo, w_hi) + _dot(x_hi, w_lo))

    h = jnp.sin(fr_ref[0:1, :] * (dot3(z, w1_ref) + b1_ref[...]))
    h = jnp.sin(fr_ref[1:2, :] * (dot3(h, w2_ref) + b2_ref[...]))
    h = jnp.sin(fr_ref[2:3, :] * (dot3(h, w3_ref) + b3_ref[...]))
    trow = (lax.broadcasted_iota(jnp.int32, (tl, HYENA_WIDTH), 0) + r0)
    decay = jnp.exp(-(trow.astype(F32) * (1.0 / (seq - 1))) * delta_ref[...])
    first = trow == 0
    for part in range(4):
        sl = slice(part * HYENA_WIDTH, (part + 1) * HYENA_WIDTH)
        v = dot3(h, w4_ref, sl) * decay
        if part >= 2:
            v = jnp.where(first, 0.0, v)
        o_ref[:, sl] = v


def _hyena_filters(seq, w1, b1, w2, b2, w3, b3, freq, w4):
    tl = min(seq, 512)
    bands = (FILTER_EMB - 1) // 2
    f = jnp.linspace(1e-4, bands - 1, bands, dtype=F32)
    fvec = jnp.zeros((1, 128), F32).at[0, 1:1 + bands].set(f).at[0, 1 + bands:1 + 2 * bands].set(f)
    w1p = jnp.zeros((128, FILTER_HIDDEN), F32).at[:FILTER_EMB].set(w1.astype(F32))
    max_decay = math.log(DECAY_TARGET) / DECAY_FAST_PCT
    min_decay = math.log(DECAY_TARGET) / DECAY_SLOW_PCT
    deltas = jnp.abs(jnp.linspace(min_decay, max_decay, HYENA_WIDTH, dtype=F32)).reshape(1, -1)
    planes = lambda w: jnp.stack(_split_const(w))
    args = [fvec, planes(w1p), b1.reshape(1, -1).astype(F32), planes(w2), b2.reshape(1, -1).astype(F32),
            planes(w3), b3.reshape(1, -1).astype(F32), freq.astype(F32), planes(w4), deltas]
    return pl.pallas_call(
        functools.partial(_filter_kernel, seq=seq, tl=tl),
        grid=(seq // tl,),
        in_specs=[_full_spec(a) for a in args],
        out_specs=pl.BlockSpec((tl, FILTER_OUT), lambda i: (i, 0)),
        out_shape=jax.ShapeDtypeStruct((seq, FILTER_OUT), F32),
        compiler_params=_cparams(("parallel",)),
        name="hyena_filter",
    )(*args)


def _dotp(mh_ref, ml_ref, x, passes):
    if passes == 1:
        return _dot(mh_ref[...], x.astype(BF16))
    return _dot3_left(mh_ref[...], ml_ref[...], x)


def _gather(ref, t2, n):
    return ref[pl.ds(t2, n, stride=PITCH), :]


def _fill_pitched(u_ref, p_ref, half, w_ref=None, b_ref=None):
    n2 = DFT_N2
    seq = half * n2
    rid = lax.broadcasted_iota(jnp.int32, (n2, LANES), 0)

    def step(t1, carry):
        r0 = pl.multiple_of(t1 * n2, n2)
        cur = u_ref[pl.ds(r0, n2), :]
        if w_ref is not None:
            before = u_ref[pl.ds(pl.multiple_of(jnp.maximum(r0 - 8, 0), 8), 8), :][7:8, :]
            after = u_ref[pl.ds(pl.multiple_of(jnp.minimum(r0 + n2, seq - 8), 8), 8), :][0:1, :]
            before = jnp.where(t1 > 0, before, 0.0)
            after = jnp.where(t1 < half - 1, after, 0.0)
            prev = jnp.where(rid == 0, before, pltpu.roll(cur, 1, 0))
            nxt = jnp.where(rid == n2 - 1, after, pltpu.roll(cur, n2 - 1, 0))
            cur = prev * w_ref[0:1, :] + cur * w_ref[1:2, :] + nxt * w_ref[2:3, :] + b_ref[...]
        p_ref[pl.ds(pl.multiple_of(t1 * PITCH, 8), n2), :] = cur
        return carry

    lax.fori_loop(0, half, step, 0)


def _for_t2_groups(body):
    def step(g, carry):
        body(g * T2_GROUP)
        return carry

    lax.fori_loop(0, DFT_N2 // T2_GROUP, step, 0, unroll=2)


def _outer_stage(p_ref, half, mh_ref, ml_ref, a_ref, nrows, passes):
    def body(base):
        xs = jnp.concatenate([_gather(p_ref, base + j, half) for j in range(T2_GROUP)], axis=1)
        y = _dotp(mh_ref, ml_ref, xs, passes)
        for j in range(T2_GROUP):
            a_ref[pl.ds(base + j, nrows, stride=PITCH), :] = y[:, j * LANES:(j + 1) * LANES]

    _for_t2_groups(body)


def _lane_block(x, j):
    return x[:, j * LANES:(j + 1) * LANES]


def _inner_blocks(a_ref, tw_ref, wstep_ref, nblocks, group_fn, store_fn):
    n2 = DFT_N2
    tw_ref[0:n2, :] = jnp.ones((n2, LANES), F32)
    tw_ref[n2:, :] = jnp.zeros((n2, LANES), F32)

    def run(k0, count):
        twr = tw_ref[0:n2, :]
        twi = tw_ref[n2:, :]
        wr = wstep_ref[0:n2, :]
        wi = wstep_ref[n2:, :]
        ks, rows, tws, blocks = [], [], [], []
        for j in range(count):
            k = k0 + j
            rr = pl.multiple_of(k * 2 * PITCH, 8)
            ri = pl.multiple_of(k * 2 * PITCH + PITCH, 8)
            ar = a_ref[pl.ds(rr, n2), :]
            ai = a_ref[pl.ds(ri, n2), :]
            ks.append(k)
            rows.append((rr, ri))
            tws.append((twr, twi))
            blocks.append(jnp.concatenate([ar * twr - ai * twi, ar * twi + ai * twr], axis=0))
            twr, twi = twr * wr - twi * wi, twr * wi + twi * wr
        vals = group_fn(ks, jnp.concatenate(blocks, axis=1), tws)
        for k, (rr, ri), val in zip(ks, rows, vals):
            store_fn(k, rr, ri, val)
        tw_ref[0:n2, :] = twr
        tw_ref[n2:, :] = twi

    def step(g, carry):
        run(g * INNER_GROUP, INNER_GROUP)
        return carry

    lax.fori_loop(0, nblocks // INNER_GROUP, step, 0)
    if nblocks % INNER_GROUP:
        run(nblocks - nblocks % INNER_GROUP, nblocks % INNER_GROUP)


def _hyena_conv_kernel(sig_ref, gate_ref, wsig_ref, bsig_ref, wgate_ref, bgate_ref, skip_ref, kf_ref,
                       f1h_ref, f1l_ref, g3h_ref, g3l_ref, gch_ref, gcl_ref, gbh_ref, gbl_ref, wstep_ref,
                       o_ref, a_ref, tw_ref, pv_ref, pg_ref, *, half, k1, k1p, sig_conv):
    n2 = DFT_N2
    if sig_conv:
        _fill_pitched(sig_ref, pv_ref, half, wsig_ref, bsig_ref)
    else:
        _fill_pitched(sig_ref, pv_ref, half)
    _fill_pitched(gate_ref, pg_ref, half, wgate_ref, bgate_ref)
    _outer_stage(pv_ref, half, f1h_ref, f1l_ref, a_ref, 2 * k1p, CONV_PASSES)

    def block(ks, xcat, tws):
        x = _dotp(gch_ref, gcl_ref, xcat, CONV_PASSES)
        prods = []
        for j, k in enumerate(ks):
            xr, xi = _lane_block(x[:n2], j), _lane_block(x[n2:], j)
            k0 = pl.multiple_of(k * 2 * n2, 2 * n2)
            kr = kf_ref[pl.ds(k0, n2), :]
            ki = kf_ref[pl.ds(k0 + n2, n2), :]
            prods.append(jnp.concatenate([xr * kr - xi * ki, xr * ki + xi * kr], axis=0))
        bb = _dotp(gbh_ref, gbl_ref, jnp.concatenate(prods, axis=1), CONV_PASSES)
        out = []
        for j, (twr, twi) in enumerate(tws):
            br, bi = _lane_block(bb[:n2], j), _lane_block(bb[n2:], j)
            out.append((br * twr + bi * twi, bi * twr - br * twi))
        return out

    def put(k, rr, ri, vals):
        a_ref[pl.ds(rr, n2), :] = vals[0]
        a_ref[pl.ds(ri, n2), :] = vals[1]

    _inner_blocks(a_ref, tw_ref, wstep_ref, k1, block, put)

    def finish(base):
        bcat = jnp.concatenate([_gather(a_ref, base + j, 2 * k1p) for j in range(T2_GROUP)], axis=1)
        y = _dotp(g3h_ref, g3l_ref, bcat, CONV_PASSES)
        for j in range(T2_GROUP):
            v = _gather(pv_ref, base + j, half)
            gate = _gather(pg_ref, base + j, half)
            pg_ref[pl.ds(base + j, half, stride=PITCH), :] = gate * (y[:, j * LANES:(j + 1) * LANES]
                                                                    + v * skip_ref[...])

    _for_t2_groups(finish)

    def emit(t1, carry):
        o_ref[pl.ds(pl.multiple_of(t1 * n2, n2), n2), :] = pg_ref[pl.ds(pl.multiple_of(t1 * PITCH, 8), n2), :]
        return carry

    lax.fori_loop(0, half, emit, 0)


def _hyena_spec_kernel(ff_ref, fb_ref, f1h_ref, f1l_ref, gch_ref, gcl_ref, wstep_ref, o_ref, a_ref, tw_ref, pf_ref,
                       *, half, k1, k1p):
    n2 = DFT_N2
    _fill_pitched(ff_ref, pf_ref, half)
    _outer_stage(pf_ref, half, f1h_ref, f1l_ref, a_ref, 2 * k1p, SPEC_PASSES)

    def spectrum(ks, xcat, tws):
        x = _dotp(gch_ref, gcl_ref, xcat, SPEC_PASSES)
        return [_lane_block(x, j) for j in range(len(ks))]

    def put_fwd(k, rr, ri, x):
        o_ref[pl.ds(pl.multiple_of(k * 2 * n2, 2 * n2), 2 * n2), :] = x

    _inner_blocks(a_ref, tw_ref, wstep_ref, k1, spectrum, put_fwd)
    _fill_pitched(fb_ref, pf_ref, half)
    _outer_stage(pf_ref, half, f1h_ref, f1l_ref, a_ref, 2 * k1p, SPEC_PASSES)

    def add_bwd(k, rr, ri, x):
        k0 = pl.multiple_of(k * 2 * n2, 2 * n2)
        o_ref[pl.ds(k0, n2), :] = o_ref[pl.ds(k0, n2), :] + x[:n2]
        o_ref[pl.ds(k0 + n2, n2), :] = o_ref[pl.ds(k0 + n2, n2), :] - x[n2:]

    _inner_blocks(a_ref, tw_ref, wstep_ref, k1, spectrum, add_bwd)


def _conv_tables(seq):
    n = 2 * seq
    n2 = DFT_N2
    n1 = n // n2
    k1 = n1 // 2 + 1
    k1p = -(-k1 // 8) * 8
    half = n1 // 2
    kk = jnp.arange(k1p, dtype=jnp.int32)
    valid = (kk < k1)
    t1 = jnp.arange(half, dtype=jnp.int32)
    c, s = _cs(kk[:, None] * t1[None, :], n1)
    vm = valid[:, None].astype(F32)
    f1 = jnp.stack([c * vm, -s * vm], axis=1).reshape(2 * k1p, half)
    wgt = jnp.where((kk == 0) | (kk == n1 // 2), 1.0, 2.0) * valid.astype(F32) / n
    g3 = jnp.stack([c * vm * wgt[:, None], -s * vm * wgt[:, None]], axis=1).reshape(2 * k1p, half).T
    j = jnp.arange(n2, dtype=jnp.int32)
    cr, cs_ = _cs(j[:, None] * j[None, :], n2)
    gc = jnp.concatenate([jnp.concatenate([cr, cs_], axis=1), jnp.concatenate([-cs_, cr], axis=1)], axis=0)
    gb = jnp.concatenate([jnp.concatenate([cr, -cs_], axis=1), jnp.concatenate([cs_, cr], axis=1)], axis=0)
    wr, ws = _cs(j, n)
    wstep = jnp.concatenate([jnp.broadcast_to(wr[:, None], (n2, LANES)),
                             jnp.broadcast_to(-ws[:, None], (n2, LANES))], axis=0)
    tabs = dict(n1=n1, k1=k1, k1p=k1p, half=half, wstep=wstep)
    for name, m in (("f1", f1), ("g3", g3), ("gc", gc), ("gb", gb)):
        tabs[name + "h"], tabs[name + "l"] = _split_const(m)
    return tabs


def _filter_spectrum(seq, tabs, fw):
    filt = _hyena_filters(seq, *fw)
    half, k1, k1p = tabs["half"], tabs["k1"], tabs["k1p"]
    n2 = DFT_N2
    nct = HYENA_WIDTH // LANES
    consts = [tabs[n] for n in ("f1h", "f1l", "gch", "gcl", "wstep")]
    return pl.pallas_call(
        functools.partial(_hyena_spec_kernel, half=half, k1=k1, k1p=k1p),
        grid=(2, nct),
        in_specs=[pl.BlockSpec((seq, LANES), lambda o, j: (0, o * nct + j)),
                  pl.BlockSpec((seq, LANES), lambda o, j: (0, (2 + o) * nct + j))] + [_full_spec(a) for a in consts],
        out_specs=pl.BlockSpec((None, k1 * 2 * n2, LANES), lambda o, j: (o, 0, j)),
        out_shape=jax.ShapeDtypeStruct((2, k1 * 2 * n2, HYENA_WIDTH), F32),
        scratch_shapes=[pltpu.VMEM((k1p * 2 * PITCH, LANES), F32), pltpu.VMEM((2 * n2, LANES), F32),
                        pltpu.VMEM((half * PITCH, LANES), F32)],
        compiler_params=_cparams(("parallel", "parallel")),
        name="hyena_filter_spectrum",
    )(filt, filt, *consts)


def _skip_first_ref(kernel_fn, *refs, **kwargs):
    return kernel_fn(*refs[1:], **kwargs)


def _merged_out(kernel_fn, args, in_specs, into, total_rows, width, dtype):
    out_shape = jax.ShapeDtypeStruct((total_rows, width), dtype)
    if into is None:
        return kernel_fn, args, in_specs, out_shape, {}
    return (functools.partial(_skip_first_ref, kernel_fn), [into] + args,
            [pl.BlockSpec(memory_space=pl.ANY)] + in_specs, out_shape, {0: 0})


def _hyena_conv(sig, sig_cols, gate_cols, proj, row0, nbatch, seq, tabs, kf, order, short_w, short_b, skip,
                total_rows=None, into=None):
    assert row0 % seq == 0
    b0 = row0 // seq
    half, k1, k1p = tabs["half"], tabs["k1"], tabs["k1p"]
    n2 = DFT_N2
    nct = HYENA_WIDTH // LANES
    sig_conv = sig is None
    if sig_conv:
        sig_arr = proj
        sig_spec = pl.BlockSpec((seq, LANES), lambda j, b: (b0 + b, sig_cols + j))
    else:
        sig_arr = sig
        sig_spec = pl.BlockSpec((seq, LANES), lambda j, b: (b, j))
    sw = short_w.astype(F32)
    sb = short_b.reshape(1, -1).astype(F32)
    consts = [tabs[n] for n in ("f1h", "f1l", "g3h", "g3l", "gch", "gcl", "gbh", "gbl", "wstep")]
    args = [sig_arr, proj, sw, sb, sw, sb, skip.reshape(1, -1).astype(F32), kf] + consts
    in_specs = [
        sig_spec,
        pl.BlockSpec((seq, LANES), lambda j, b: (b0 + b, gate_cols + j), pipeline_mode=pl.Buffered(1)),
        pl.BlockSpec((3, LANES), lambda j, b: (0, sig_cols + j)),
        pl.BlockSpec((1, LANES), lambda j, b: (0, sig_cols + j)),
        pl.BlockSpec((3, LANES), lambda j, b: (0, gate_cols + j)),
        pl.BlockSpec((1, LANES), lambda j, b: (0, gate_cols + j)),
        pl.BlockSpec((1, LANES), lambda j, b: (0, j)),
        pl.BlockSpec((None, k1 * 2 * n2, LANES), lambda j, b: (order, 0, j), pipeline_mode=pl.Buffered(1)),
    ] + [_full_spec(a) for a in consts]
    kern = functools.partial(_hyena_conv_kernel, half=half, k1=k1, k1p=k1p, sig_conv=sig_conv)
    out_b0 = 0 if total_rows is None else b0
    kern, args, in_specs, out_shape, aliases = _merged_out(
        kern, args, in_specs, into, total_rows or nbatch * seq, HYENA_WIDTH, F32)
    return pl.pallas_call(
        kern,
        grid=(nct, nbatch),
        in_specs=in_specs,
        out_specs=pl.BlockSpec((seq, LANES), lambda j, b: (out_b0 + b, j)),
        out_shape=out_shape,
        input_output_aliases=aliases,
        scratch_shapes=[pltpu.VMEM((k1p * 2 * PITCH, LANES), F32), pltpu.VMEM((2 * n2, LANES), F32),
                        pltpu.VMEM((half * PITCH, LANES), F32), pltpu.VMEM((half * PITCH, LANES), F32)],
        compiler_params=_cparams(("parallel", "arbitrary")),
        name="hyena_conv",
    )(*args)


def _hyena_batch(proj, row0, nbatch, seq, tabs, kf, short_w, short_b, skip, total_rows, into):
    nct = HYENA_WIDTH // LANES
    z = _hyena_conv(None, 2 * nct, 0, proj, row0, nbatch, seq, tabs, kf, 0, short_w, short_b, skip[0])
    return _hyena_conv(z, 2 * nct, nct, proj, row0, nbatch, seq, tabs, kf, 1, short_w, short_b, skip[1],
                       total_rows=total_rows, into=into)


def _fnet_kernel(u_ref, chan_ref, m1_ref, gri_ref, wstep_ref, o_ref, zr_ref, zi_ref, a_ref, tw_ref, *, n1):
    n2 = DFT_N2
    slabs = 4

    def chan(g, carry):
        x = u_ref[pl.ds(pl.multiple_of(g * slabs * n2, slabs * n2), slabs * n2), :].astype(BF16)
        z = _dot(x, chan_ref[...])
        for i in range(slabs):
            r = pl.multiple_of((g * slabs + i) * PITCH, 8)
            zr_ref[pl.ds(r, n2), :] = z[i * n2:(i + 1) * n2, :LANES]
            zi_ref[pl.ds(r, n2), :] = z[i * n2:(i + 1) * n2, LANES:]
        return carry

    lax.fori_loop(0, n1 // slabs, chan, 0)

    def outer(base):
        xs = jnp.concatenate(
            [jnp.concatenate([_gather(zr_ref, base + j, n1), _gather(zi_ref, base + j, n1)], axis=0)
             for j in range(T2_GROUP)], axis=1)
        y = _dot(m1_ref[...], xs.astype(BF16))
        for j in range(T2_GROUP):
            a_ref[pl.ds(base + j, 2 * n1, stride=PITCH), :] = y[:, j * LANES:(j + 1) * LANES]

    _for_t2_groups(outer)

    def real_part(ks, xcat, tws):
        y = _dot(gri_ref[...], xcat.astype(BF16))
        return [_lane_block(y, j) for j in range(len(ks))]

    def put(k, rr, ri, y):
        zr_ref[pl.ds(pl.multiple_of(k * PITCH, 8), n2), :] = y

    _inner_blocks(a_ref, tw_ref, wstep_ref, n1, real_part, put)

    def emit(k2, carry):
        o_ref[pl.ds(pl.multiple_of(k2 * n1, n1), n1), :] = _gather(zr_ref, k2, n1).astype(o_ref.dtype)
        return carry

    lax.fori_loop(0, n2, emit, 0)


def _fnet_tables(seq):
    n2 = DFT_N2
    n1 = seq // n2
    j = jnp.arange(HEAD_DIM, dtype=jnp.int32)
    c, s = _cs(j[:, None] * j[None, :], HEAD_DIM)
    scale = (seq * HEAD_DIM) ** -0.5
    chan = jnp.concatenate([c * scale, -s * scale], axis=1).astype(BF16)
    kk = jnp.arange(n1, dtype=jnp.int32)
    c1, s1 = _cs(kk[:, None] * kk[None, :], n1)
    m1 = jnp.stack([jnp.concatenate([c1, s1], axis=1), jnp.concatenate([-s1, c1], axis=1)], axis=1)
    m1 = m1.reshape(2 * n1, 2 * n1).astype(BF16)
    t2 = jnp.arange(n2, dtype=jnp.int32)
    cr, cs_ = _cs(t2[:, None] * t2[None, :], n2)
    gri = jnp.concatenate([cr, cs_], axis=1).astype(BF16)
    wr, ws = _cs(t2, seq)
    wstep = jnp.concatenate([jnp.broadcast_to(wr[:, None], (n2, LANES)),
                             jnp.broadcast_to(-ws[:, None], (n2, LANES))], axis=0)
    return dict(n1=n1, chan=chan, m1=m1, gri=gri, wstep=wstep)


def _fnet_batch(proj, row0, nbatch, seq, tabs, total_rows, into):
    assert row0 % seq == 0
    b0 = row0 // seq
    n1 = tabs["n1"]
    consts = [tabs[n] for n in ("chan", "m1", "gri", "wstep")]
    in_specs = ([pl.BlockSpec((seq, LANES), lambda j, b: (b0 + b, OFF_FNET // LANES + j))]
                + [_full_spec(a) for a in consts])
    kern, args, in_specs, out_shape, aliases = _merged_out(
        functools.partial(_fnet_kernel, n1=n1), [proj] + consts, in_specs, into, total_rows, FNET_WIDTH, BF16)
    return pl.pallas_call(
        kern,
        grid=(FNET_HEADS, nbatch),
        in_specs=in_specs,
        out_specs=pl.BlockSpec((seq, LANES), lambda j, b: (b0 + b, j)),
        out_shape=out_shape,
        input_output_aliases=aliases,
        scratch_shapes=[pltpu.VMEM((n1 * PITCH, LANES), F32), pltpu.VMEM((n1 * PITCH, LANES), F32),
                        pltpu.VMEM((2 * n1 * PITCH, LANES), F32), pltpu.VMEM((2 * DFT_N2, LANES), F32)],
        compiler_params=_cparams(("parallel", "parallel")),
        name="fnet_mixer",
    )(*args)


def _pick_tile(t, pref):
    while t % pref:
        pref //= 2
    return pref


def kernel(x_prompt, x_sample, ln0_g, ln0_b, w_in, short_w, short_b, filt_w1, filt_b1, filt_w2, filt_b2, filt_w3, filt_b3, filt_freq, filt_w4, hyena_skip, w_fnet, b_fnet, attn_sink, w_out, ln1_g, ln1_b, w_gate, w_up, w_down, ln2_g, ln2_b):
    bp, lp, _ = x_prompt.shape
    bs, ls, _ = x_sample.shape
    tp, ts = bp * lp, bs * ls
    batches = ((0, bp, lp), (tp, bs, ls))
    tm = _pick_tile(math.gcd(tp, ts), 1024)
    tln = _pick_tile(math.gcd(tp, ts), 256)

    conv_tabs = {seq: _conv_tables(seq) for seq in {lp, ls}}
    fnet_tabs = {seq: _fnet_tables(seq) for seq in {lp, ls}}

    xf, xb = _ln0(x_prompt.reshape(tp, D_MODEL), x_sample.reshape(ts, D_MODEL), ln0_g, ln0_b, tln)
    for l in range(DEPTH):
        wd_b = w_down[l].astype(BF16)
        fw = (filt_w1[l], filt_b1[l], filt_w2[l], filt_b2[l], filt_w3[l], filt_b3[l], filt_freq[l], filt_w4[l])

        proj = _matmul([xb], w_in[l].astype(BF16), None, F32, tm, 1024, "in_proj")

        kf = {seq: _filter_spectrum(seq, conv_tabs[seq], fw) for seq in {lp, ls}}
        y_h = y_f = None
        for r0, nb, seq in batches:
            y_h = _hyena_batch(proj, r0, nb, seq, conv_tabs[seq], kf[seq], short_w[l], short_b[l], hyena_skip[l],
                               tp + ts, y_h)
            y_f = _fnet_batch(proj, r0, nb, seq, fnet_tabs[seq], tp + ts, y_f)
        y_f = _matmul([y_f], w_fnet, b_fnet[l], BF16, tm, 1024, "fnet_linear", layer=l)
        y_a = _attention(proj, attn_sink[l], tp, lp, ls)

        y = _matmul([y_h, y_f, y_a], w_out[l].astype(BF16), None, F32, tm, 512, "out_proj", resid=xf)
        xf, xb = _ln(y, ln1_g[l], ln1_b[l], tln)

        hid = _gate_up(xb, w_gate, w_up, l, _pick_tile(math.gcd(tp, ts), 2048), FF_TILE)
        y = _matmul([hid], wd_b, None, F32, _pick_tile(tm, 512), 512, "ffn_down", resid=xf)
        if l + 1 < DEPTH:
            xf, xb = _ln(y, ln2_g[l], ln2_b[l], tln)
        else:
            y_p, y_s = _ln_final(y, ln2_g[l], ln2_b[l], tln, tp)
    return y_p.reshape(bp, lp, D_MODEL), y_s.reshape(bs, ls, D_MODEL)
```

```python
import functools
import math

import jax
import jax.numpy as jnp
from jax import lax
from jax.experimental import pallas as pl
from jax.experimental.pallas import tpu as pltpu

F32 = jnp.float32
BF16 = jnp.bfloat16

D_MODEL = 4096
HEAD_DIM = 128
HYENA_WIDTH = 1024
FNET_WIDTH = 1024
ATTN_WIDTH = 2048
FNET_HEADS = 8
N_HEADS = 16
N_KV_HEADS = 4
KV_GROUP = 4
KV_WIDTH = 512
BLOCK = 128
HYENA_IN = 3 * HYENA_WIDTH
FILTER_EMB = 33
FILTER_HIDDEN = 64
FILTER_OUT = 4 * HYENA_WIDTH
OFF_FNET = HYENA_IN
OFF_Q = OFF_FNET + FNET_WIDTH
OFF_K = OFF_Q + ATTN_WIDTH
OFF_V = OFF_K + KV_WIDTH
IN_WIDTH = OFF_V + KV_WIDTH
D_FF = 11008
FF_TILE = 256
DEPTH = 2
ALPHA = (2 * DEPTH) ** 0.25
LN_EPS = 1e-5
DECAY_FAST_PCT = 0.3
DECAY_SLOW_PCT = 1.5
DECAY_TARGET = 1e-2

DFT_N2 = 128
LANES = 128
T2_GROUP = 8
PITCH = 136
CONV_PASSES = 1
SPEC_PASSES = 1
INNER_GROUP = 8
ATTN_TQ = 512
VMEM_LIMIT = 56 * 1024 * 1024


def _cparams(sem, vmem=VMEM_LIMIT):
    return pltpu.CompilerParams(dimension_semantics=sem, vmem_limit_bytes=vmem)


def _dot(a, b):
    return jnp.dot(a, b, preferred_element_type=F32)


def _split(x):
    hi = x.astype(BF16)
    lo = (x - hi.astype(F32)).astype(BF16)
    return hi, lo


def _dot3_left(m_hi, m_lo, x):
    x_hi, x_lo = _split(x)
    return _dot(m_hi, x_hi) + (_dot(m_hi, x_lo) + _dot(m_lo, x_hi))


def _split_const(m):
    m = m.astype(F32)
    hi = m.astype(BF16)
    lo = (m - hi.astype(F32)).astype(BF16)
    return hi, lo


def _cs(num, den):
    ang = (2.0 * math.pi / den) * (num % den).astype(F32)
    return jnp.cos(ang), jnp.sin(ang)


def _full_spec(a):
    return pl.BlockSpec(a.shape, lambda *_: (0,) * a.ndim)


def _ln_math(x, g, b):
    mu = jnp.mean(x, axis=-1, keepdims=True)
    xc = x - mu
    var = jnp.mean(xc * xc, axis=-1, keepdims=True)
    return xc * lax.rsqrt(var + LN_EPS) * g + b


def _ln0_kernel(xp_ref, xs_ref, g_ref, b_ref, of_ref, ob_ref, *, n_p):
    i = pl.program_id(0)

    def emit(x):
        y = _ln_math(x, g_ref[...], b_ref[...])
        of_ref[...] = y
        ob_ref[...] = y.astype(BF16)

    @pl.when(i < n_p)
    def _():
        emit(xp_ref[...])

    @pl.when(i >= n_p)
    def _():
        emit(xs_ref[...])


def _ln0(xp, xs, g, b, tm):
    tp, ts = xp.shape[0], xs.shape[0]
    n_p, n_s = tp // tm, ts // tm
    t = tp + ts
    return pl.pallas_call(
        functools.partial(_ln0_kernel, n_p=n_p),
        grid=(n_p + n_s,),
        in_specs=[
            pl.BlockSpec((tm, D_MODEL), lambda i: (jnp.minimum(i, n_p - 1), 0)),
            pl.BlockSpec((tm, D_MODEL), lambda i: (jnp.maximum(i - n_p, 0), 0)),
            pl.BlockSpec((1, D_MODEL), lambda i: (0, 0)),
            pl.BlockSpec((1, D_MODEL), lambda i: (0, 0)),
        ],
        out_specs=[
            pl.BlockSpec((tm, D_MODEL), lambda i: (i, 0)),
            pl.BlockSpec((tm, D_MODEL), lambda i: (i, 0)),
        ],
        out_shape=[jax.ShapeDtypeStruct((t, D_MODEL), F32), jax.ShapeDtypeStruct((t, D_MODEL), BF16)],
        compiler_params=_cparams(("parallel",)),
        name="ln0",
    )(xp, xs, g.reshape(1, -1), b.reshape(1, -1))


def _ln_kernel(y_ref, g_ref, b_ref, of_ref, ob_ref):
    y = _ln_math(y_ref[...], g_ref[...], b_ref[...])
    of_ref[...] = y
    ob_ref[...] = y.astype(BF16)


def _ln(y, g, b, tm):
    t = y.shape[0]
    row = pl.BlockSpec((tm, D_MODEL), lambda i: (i, 0))
    vec = pl.BlockSpec((1, D_MODEL), lambda i: (0, 0))
    return pl.pallas_call(
        _ln_kernel,
        grid=(t // tm,),
        in_specs=[row, vec, vec],
        out_specs=[row, row],
        out_shape=[jax.ShapeDtypeStruct((t, D_MODEL), F32), jax.ShapeDtypeStruct((t, D_MODEL), BF16)],
        compiler_params=_cparams(("parallel",)),
        name="layer_norm",
    )(y, g.reshape(1, -1), b.reshape(1, -1))


def _ln_final_kernel(y_ref, g_ref, b_ref, op_ref, os_ref, *, n_p):
    i = pl.program_id(0)
    y = _ln_math(y_ref[...], g_ref[...], b_ref[...])

    @pl.when(i < n_p)
    def _():
        op_ref[...] = y

    @pl.when(i >= n_p)
    def _():
        os_ref[...] = y


def _ln_final(y, g, b, tm, tp):
    t = y.shape[0]
    n_p = tp // tm
    row = pl.BlockSpec((tm, D_MODEL), lambda i: (i, 0))
    vec = pl.BlockSpec((1, D_MODEL), lambda i: (0, 0))
    return pl.pallas_call(
        functools.partial(_ln_final_kernel, n_p=n_p),
        grid=(t // tm,),
        in_specs=[row, vec, vec],
        out_specs=[
            pl.BlockSpec((tm, D_MODEL), lambda i: (jnp.minimum(i, n_p - 1), 0)),
            pl.BlockSpec((tm, D_MODEL), lambda i: (jnp.maximum(i - n_p, 0), 0)),
        ],
        out_shape=[jax.ShapeDtypeStruct((tp, D_MODEL), F32), jax.ShapeDtypeStruct((t - tp, D_MODEL), F32)],
        compiler_params=_cparams(("arbitrary",)),
        name="layer_norm_final",
    )(y, g.reshape(1, -1), b.reshape(1, -1))


def _mm_kernel(*refs, widths, has_bias, has_resid):
    n_a = len(widths)
    a_refs = refs[:n_a]
    w_ref = refs[n_a]
    o_ref = refs[-1]
    nxt = n_a + 1
    acc = None
    off = 0
    for a_ref, wd in zip(a_refs, widths):
        part = _dot(a_ref[...].astype(BF16), w_ref[off:off + wd, :].astype(BF16))
        acc = part if acc is None else acc + part
        off += wd
    if has_bias:
        acc = acc + refs[nxt][...]
        nxt += 1
    if has_resid:
        acc = ALPHA * refs[nxt][...] + acc
    o_ref[...] = acc.astype(o_ref.dtype)


def _matmul(a_list, w, bias, out_dtype, tm, tn, name, resid=None, layer=None):
    t = a_list[0].shape[0]
    k, n = w.shape[-2:]
    widths = tuple(a.shape[1] for a in a_list)
    assert sum(widths) == k and t % tm == 0 and n % tn == 0
    in_specs = [pl.BlockSpec((tm, wd), lambda i, j: (i, 0)) for wd in widths]
    if layer is None:
        in_specs.append(pl.BlockSpec((k, tn), lambda i, j: (0, j)))
    else:
        in_specs.append(pl.BlockSpec((None, k, tn), lambda i, j: (layer, 0, j)))
    args = list(a_list) + [w]
    if bias is not None:
        in_specs.append(pl.BlockSpec((1, tn), lambda i, j: (0, j)))
        args.append(bias.reshape(1, n).astype(F32))
    if resid is not None:
        in_specs.append(pl.BlockSpec((tm, tn), lambda i, j: (i, j)))
        args.append(resid)
    return pl.pallas_call(
        functools.partial(_mm_kernel, widths=widths, has_bias=bias is not None, has_resid=resid is not None),
        grid=(t // tm, n // tn),
        in_specs=in_specs,
        out_specs=pl.BlockSpec((tm, tn), lambda i, j: (i, j)),
        out_shape=jax.ShapeDtypeStruct((t, n), out_dtype),
        compiler_params=_cparams(("parallel", "parallel")),
        name=name,
    )(*args)


def _gate_up_kernel(x_ref, wg_ref, wu_ref, o_ref):
    x = x_ref[...]
    g = _dot(x, wg_ref[...].astype(BF16))
    u = _dot(x, wu_ref[...].astype(BF16))
    o_ref[...] = (g * (1.0 / (1.0 + jnp.exp(-g))) * u).astype(o_ref.dtype)


def _gate_up(x, wg, wu, layer, tm, tn):
    t, k = x.shape
    n = wg.shape[-1]
    return pl.pallas_call(
        _gate_up_kernel,
        grid=(t // tm, n // tn),
        in_specs=[
            pl.BlockSpec((tm, k), lambda i, j: (i, 0)),
            pl.BlockSpec((None, k, tn), lambda i, j: (layer, 0, j)),
            pl.BlockSpec((None, k, tn), lambda i, j: (layer, 0, j)),
        ],
        out_specs=pl.BlockSpec((tm, tn), lambda i, j: (i, j)),
        out_shape=jax.ShapeDtypeStruct((t, n), BF16),
        compiler_params=_cparams(("parallel", "parallel")),
        name="ffn_gate_up",
    )(x, wg, wu)


def _attn_kernel(q_ref, kp_ref, kc_ref, kn_ref, vp_ref, vc_ref, vn_ref, bias_ref, sink_ref, o_ref, kbuf, vbuf,
                 *, tq, tiles_p, per_p, per_s):
    g = pl.program_id(0)
    h = pl.program_id(1)
    in_p = g < tiles_p
    n_loc = jnp.where(in_p, g % per_p, (g - tiles_p) % per_s)
    n_seq = jnp.where(in_p, per_p, per_s)
    pen_prev = jnp.where(n_loc > 0, 0.0, -jnp.inf)
    pen_next = jnp.where(n_loc < n_seq - 1, 0.0, -jnp.inf)

    kbuf[0:BLOCK, :] = kp_ref[...].astype(BF16)
    kbuf[BLOCK:BLOCK + tq, :] = kc_ref[...].astype(BF16)
    kbuf[BLOCK + tq:, :] = kn_ref[...].astype(BF16)
    vbuf[:, 0:BLOCK] = vp_ref[...].T.astype(BF16)
    for c0 in range(0, tq, BLOCK):
        vbuf[:, BLOCK + c0:2 * BLOCK + c0] = vc_ref[c0:c0 + BLOCK, :].T.astype(BF16)
    vbuf[:, BLOCK + tq:] = vn_ref[...].T.astype(BF16)

    key = lax.broadcasted_iota(jnp.int32, (3 * BLOCK, BLOCK), 0)
    col_prev = jnp.where(key < BLOCK, pen_prev, 0.0)
    col_next = jnp.where(key >= 2 * BLOCK, pen_next, 0.0)
    scale = HEAD_DIM ** -0.5
    dn = (((1,), (1,)), ((), ()))
    nsb = tq // BLOCK
    for sb in range(nsb):
        r0 = sb * BLOCK
        k3 = kbuf[r0:r0 + 3 * BLOCK, :]
        v3 = vbuf[:, r0:r0 + 3 * BLOCK]
        for gi in range(KV_GROUP):
            sink = sink_ref[h * KV_GROUP + gi]
            q = q_ref[r0:r0 + BLOCK, gi * HEAD_DIM:(gi + 1) * HEAD_DIM].astype(BF16)
            s = lax.dot_general(k3, q, dn, preferred_element_type=F32) * scale
            s = s + bias_ref[gi * 3 * BLOCK:(gi + 1) * 3 * BLOCK, :]
            if sb == 0:
                s = s + col_prev
            if sb == nsb - 1:
                s = s + col_next
            m = jnp.maximum(jnp.max(s, axis=0, keepdims=True), sink)
            p = jnp.exp(s - m)
            denom = jnp.sum(p, axis=0, keepdims=True) + jnp.exp(sink - m)
            o_t = _dot(v3, p.astype(BF16)) * (1.0 / denom)
            o_ref[r0:r0 + BLOCK, gi * HEAD_DIM:(gi + 1) * HEAD_DIM] = o_t.T.astype(o_ref.dtype)


def _attention(proj, sink, tp, lp, ls):
    t = proj.shape[0]
    tq = math.gcd(ATTN_TQ, math.gcd(lp, ls))
    nblk = t // BLOCK
    bpt = tq // BLOCK
    slopes = 2.0 ** (-8.0 * jnp.arange(1, N_HEADS + 1, dtype=F32) / N_HEADS)
    qi = jnp.arange(BLOCK)[:, None]
    ki = jnp.arange(3 * BLOCK)[None, :]
    dist = jnp.abs(qi + BLOCK - ki)
    bias = jnp.where(dist[None] <= BLOCK, -slopes[:, None, None] * dist[None].astype(F32), -jnp.inf)
    bias = jnp.swapaxes(bias, 1, 2).reshape(N_HEADS * 3 * BLOCK, BLOCK)
    qc = OFF_Q // (KV_GROUP * HEAD_DIM)
    kc = OFF_K // HEAD_DIM
    vc = OFF_V // HEAD_DIM
    prev = lambda g: jnp.maximum(g * bpt - 1, 0)
    nxt = lambda g: jnp.minimum((g + 1) * bpt, nblk - 1)
    halo = (BLOCK, HEAD_DIM)
    cur = (tq, HEAD_DIM)
    return pl.pallas_call(
        functools.partial(_attn_kernel, tq=tq, tiles_p=tp // tq, per_p=lp // tq, per_s=ls // tq),
        grid=(t // tq, N_KV_HEADS),
        in_specs=[
            pl.BlockSpec((tq, KV_GROUP * HEAD_DIM), lambda g, h: (g, qc + h)),
            pl.BlockSpec(halo, lambda g, h: (prev(g), kc + h)),
            pl.BlockSpec(cur, lambda g, h: (g, kc + h)),
            pl.BlockSpec(halo, lambda g, h: (nxt(g), kc + h)),
            pl.BlockSpec(halo, lambda g, h: (prev(g), vc + h)),
            pl.BlockSpec(cur, lambda g, h: (g, vc + h)),
            pl.BlockSpec(halo, lambda g, h: (nxt(g), vc + h)),
            pl.BlockSpec((KV_GROUP * 3 * BLOCK, BLOCK), lambda g, h: (h, 0)),
            pl.BlockSpec(memory_space=pltpu.SMEM),
        ],
        out_specs=pl.BlockSpec((tq, KV_GROUP * HEAD_DIM), lambda g, h: (g, h)),
        out_shape=jax.ShapeDtypeStruct((t, ATTN_WIDTH), BF16),
        scratch_shapes=[pltpu.VMEM((tq + 2 * BLOCK, HEAD_DIM), BF16), pltpu.VMEM((HEAD_DIM, tq + 2 * BLOCK), BF16)],
        compiler_params=_cparams(("parallel", "parallel")),
        name="band_attention",
    )(proj, proj, proj, proj, proj, proj, proj, bias, sink.astype(F32))


def _filter_kernel(fvec_ref, w1_ref, b1_ref, w2_ref, b2_ref, w3_ref, b3_ref, fr_ref, w4_ref, delta_ref, o_ref,
                   *, seq, tl):
    r0 = pl.program_id(0) * tl
    row = (lax.broadcasted_iota(jnp.int32, (tl, 128), 0) + r0).astype(F32)
    lane = lax.broadcasted_iota(jnp.int32, (tl, 128), 1)
    t = row * (1.0 / (seq - 1))
    ang = (row * (2.0 * math.pi / seq)) * fvec_ref[...]
    z = jnp.where(lane == 0, t,
                  jnp.where(lane <= 16, jnp.cos(ang), jnp.where(lane <= 32, -jnp.sin(ang), 0.0)))
    def dot3(x, w_ref, cols=slice(None)):
        x_hi, x_lo = _split(x)
        w_hi, w_lo = w_ref[0, :, cols], w_ref[1, :, cols]
        return _dot(x_hi, w_hi) + (_dot(x_lo, w_hi) + _dot(x_hi, w_lo))

    h = jnp.sin(fr_ref[0:1, :] * (dot3(z, w1_ref) + b1_ref[...]))
    h = jnp.sin(fr_ref[1:2, :] * (dot3(h, w2_ref) + b2_ref[...]))
    h = jnp.sin(fr_ref[2:3, :] * (dot3(h, w3_ref) + b3_ref[...]))
    trow = (lax.broadcasted_iota(jnp.int32, (tl, HYENA_WIDTH), 0) + r0)
    decay = jnp.exp(-(trow.astype(F32) * (1.0 / (seq - 1))) * delta_ref[...])
    first = trow == 0
    for part in range(4):
        sl = slice(part * HYENA_WIDTH, (part + 1) * HYENA_WIDTH)
        v = dot3(h, w4_ref, sl) * decay
        if part >= 2:
            v = jnp.where(first, 0.0, v)
        o_ref[:, sl] = v


def _hyena_filters(seq, w1, b1, w2, b2, w3, b3, freq, w4):
    tl = min(seq, 512)
    bands = (FILTER_EMB - 1) // 2
    f = jnp.linspace(1e-4, bands - 1, bands, dtype=F32)
    fvec = jnp.zeros((1, 128), F32).at[0, 1:1 + bands].set(f).at[0, 1 + bands:1 + 2 * bands].set(f)
    w1p = jnp.zeros((128, FILTER_HIDDEN), F32).at[:FILTER_EMB].set(w1.astype(F32))
    max_decay = math.log(DECAY_TARGET) / DECAY_FAST_PCT
    min_decay = math.log(DECAY_TARGET) / DECAY_SLOW_PCT
    deltas = jnp.abs(jnp.linspace(min_decay, max_decay, HYENA_WIDTH, dtype=F32)).reshape(1, -1)
    planes = lambda w: jnp.stack(_split_const(w))
    args = [fvec, planes(w1p), b1.reshape(1, -1).astype(F32), planes(w2), b2.reshape(1, -1).astype(F32),
            planes(w3), b3.reshape(1, -1).astype(F32), freq.astype(F32), planes(w4), deltas]
    return pl.pallas_call(
        functools.partial(_filter_kernel, seq=seq, tl=tl),
        grid=(seq // tl,),
        in_specs=[_full_spec(a) for a in args],
        out_specs=pl.BlockSpec((tl, FILTER_OUT), lambda i: (i, 0)),
        out_shape=jax.ShapeDtypeStruct((seq, FILTER_OUT), F32),
        compiler_params=_cparams(("parallel",)),
        name="hyena_filter",
    )(*args)


def _dotp(mh_ref, ml_ref, x, passes):
    if passes == 1:
        return _dot(mh_ref[...], x.astype(BF16))
    return _dot3_left(mh_ref[...], ml_ref[...], x)


def _gather(ref, t2, n):
    return ref[pl.ds(t2, n, stride=PITCH), :]


def _fill_pitched(u_ref, p_ref, half, w_ref=None, b_ref=None):
    n2 = DFT_N2
    seq = half * n2
    rid = lax.broadcasted_iota(jnp.int32, (n2, LANES), 0)

    def step(t1, carry):
        r0 = pl.multiple_of(t1 * n2, n2)
        cur = u_ref[pl.ds(r0, n2), :]
        if w_ref is not None:
            before = u_ref[pl.ds(pl.multiple_of(jnp.maximum(r0 - 8, 0), 8), 8), :][7:8, :]
            after = u_ref[pl.ds(pl.multiple_of(jnp.minimum(r0 + n2, seq - 8), 8), 8), :][0:1, :]
            before = jnp.where(t1 > 0, before, 0.0)
            after = jnp.where(t1 < half - 1, after, 0.0)
            prev = jnp.where(rid == 0, before, pltpu.roll(cur, 1, 0))
            nxt = jnp.where(rid == n2 - 1, after, pltpu.roll(cur, n2 - 1, 0))
            cur = prev * w_ref[0:1, :] + cur * w_ref[1:2, :] + nxt * w_ref[2:3, :] + b_ref[...]
        p_ref[pl.ds(pl.multiple_of(t1 * PITCH, 8), n2), :] = cur
        return carry

    lax.fori_loop(0, half, step, 0)


def _for_t2_groups(body):
    def step(g, carry):
        body(g * T2_GROUP)
        return carry

    lax.fori_loop(0, DFT_N2 // T2_GROUP, step, 0, unroll=2)


def _outer_stage(p_ref, half, mh_ref, ml_ref, a_ref, nrows, passes):
    def body(base):
        xs = jnp.concatenate([_gather(p_ref, base + j, half) for j in range(T2_GROUP)], axis=1)
        y = _dotp(mh_ref, ml_ref, xs, passes)
        for j in range(T2_GROUP):
            a_ref[pl.ds(base + j, nrows, stride=PITCH), :] = y[:, j * LANES:(j + 1) * LANES]

    _for_t2_groups(body)


def _lane_block(x, j):
    return x[:, j * LANES:(j + 1) * LANES]


def _inner_blocks(a_ref, tw_ref, wstep_ref, nblocks, group_fn, store_fn):
    n2 = DFT_N2
    tw_ref[0:n2, :] = jnp.ones((n2, LANES), F32)
    tw_ref[n2:, :] = jnp.zeros((n2, LANES), F32)

    def run(k0, count):
        twr = tw_ref[0:n2, :]
        twi = tw_ref[n2:, :]
        wr = wstep_ref[0:n2, :]
        wi = wstep_ref[n2:, :]
        ks, rows, tws, blocks = [], [], [], []
        for j in range(count):
            k = k0 + j
            rr = pl.multiple_of(k * 2 * PITCH, 8)
            ri = pl.multiple_of(k * 2 * PITCH + PITCH, 8)
            ar = a_ref[pl.ds(rr, n2), :]
            ai = a_ref[pl.ds(ri, n2), :]
            ks.append(k)
            rows.append((rr, ri))
            tws.append((twr, twi))
            blocks.append(jnp.concatenate([ar * twr - ai * twi, ar * twi + ai * twr], axis=0))
            twr, twi = twr * wr - twi * wi, twr * wi + twi * wr
        vals = group_fn(ks, jnp.concatenate(blocks, axis=1), tws)
        for k, (rr, ri), val in zip(ks, rows, vals):
            store_fn(k, rr, ri, val)
        tw_ref[0:n2, :] = twr
        tw_ref[n2:, :] = twi

    def step(g, carry):
        run(g * INNER_GROUP, INNER_GROUP)
        return carry

    lax.fori_loop(0, nblocks // INNER_GROUP, step, 0)
    if nblocks % INNER_GROUP:
        run(nblocks - nblocks % INNER_GROUP, nblocks % INNER_GROUP)


def _hyena_conv_kernel(sig_ref, gate_ref, wsig_ref, bsig_ref, wgate_ref, bgate_ref, skip_ref, kf_ref,
                       f1h_ref, f1l_ref, g3h_ref, g3l_ref, gch_ref, gcl_ref, gbh_ref, gbl_ref, wstep_ref,
                       o_ref, a_ref, tw_ref, pv_ref, pg_ref, *, half, k1, k1p, sig_conv):
    n2 = DFT_N2
    if sig_conv:
        _fill_pitched(sig_ref, pv_ref, half, wsig_ref, bsig_ref)
    else:
        _fill_pitched(sig_ref, pv_ref, half)
    _fill_pitched(gate_ref, pg_ref, half, wgate_ref, bgate_ref)
    _outer_stage(pv_ref, half, f1h_ref, f1l_ref, a_ref, 2 * k1p, CONV_PASSES)

    def block(ks, xcat, tws):
        x = _dotp(gch_ref, gcl_ref, xcat, CONV_PASSES)
        prods = []
        for j, k in enumerate(ks):
            xr, xi = _lane_block(x[:n2], j), _lane_block(x[n2:], j)
            k0 = pl.multiple_of(k * 2 * n2, 2 * n2)
            kr = kf_ref[pl.ds(k0, n2), :].astype(F32)
            ki = kf_ref[pl.ds(k0 + n2, n2), :].astype(F32)
            prods.append(jnp.concatenate([xr * kr - xi * ki, xr * ki + xi * kr], axis=0))
        bb = _dotp(gbh_ref, gbl_ref, jnp.concatenate(prods, axis=1), CONV_PASSES)
        out = []
        for j, (twr, twi) in enumerate(tws):
            br, bi = _lane_block(bb[:n2], j), _lane_block(bb[n2:], j)
            out.append((br * twr + bi * twi, bi * twr - br * twi))
        return out

    def put(k, rr, ri, vals):
        a_ref[pl.ds(rr, n2), :] = vals[0]
        a_ref[pl.ds(ri, n2), :] = vals[1]

    _inner_blocks(a_ref, tw_ref, wstep_ref, k1, block, put)

    def finish(base):
        bcat = jnp.concatenate([_gather(a_ref, base + j, 2 * k1p) for j in range(T2_GROUP)], axis=1)
        y = _dotp(g3h_ref, g3l_ref, bcat, CONV_PASSES)
        for j in range(T2_GROUP):
            v = _gather(pv_ref, base + j, half)
            gate = _gather(pg_ref, base + j, half)
            pg_ref[pl.ds(base + j, half, stride=PITCH), :] = gate * (y[:, j * LANES:(j + 1) * LANES]
                                                                    + v * skip_ref[...])

    _for_t2_groups(finish)

    def emit(t1, carry):
        o_ref[pl.ds(pl.multiple_of(t1 * n2, n2), n2), :] = pg_ref[pl.ds(pl.multiple_of(t1 * PITCH, 8), n2), :]
        return carry

    lax.fori_loop(0, half, emit, 0)


def _hyena_spec_kernel(ff_ref, fb_ref, f1h_ref, f1l_ref, gch_ref, gcl_ref, wstep_ref, o_ref, a_ref, tw_ref, pf_ref,
                       acc_ref, *, half, k1, k1p):
    n2 = DFT_N2
    _fill_pitched(ff_ref, pf_ref, half)
    _outer_stage(pf_ref, half, f1h_ref, f1l_ref, a_ref, 2 * k1p, SPEC_PASSES)

    def spectrum(ks, xcat, tws):
        x = _dotp(gch_ref, gcl_ref, xcat, SPEC_PASSES)
        return [_lane_block(x, j) for j in range(len(ks))]

    def put_fwd(k, rr, ri, x):
        acc_ref[pl.ds(pl.multiple_of(k * 2 * n2, 2 * n2), 2 * n2), :] = x

    _inner_blocks(a_ref, tw_ref, wstep_ref, k1, spectrum, put_fwd)
    _fill_pitched(fb_ref, pf_ref, half)
    _outer_stage(pf_ref, half, f1h_ref, f1l_ref, a_ref, 2 * k1p, SPEC_PASSES)

    def add_bwd(k, rr, ri, x):
        k0 = pl.multiple_of(k * 2 * n2, 2 * n2)
        o_ref[pl.ds(k0, n2), :] = (acc_ref[pl.ds(k0, n2), :] + x[:n2]).astype(o_ref.dtype)
        o_ref[pl.ds(k0 + n2, n2), :] = (acc_ref[pl.ds(k0 + n2, n2), :] - x[n2:]).astype(o_ref.dtype)

    _inner_blocks(a_ref, tw_ref, wstep_ref, k1, spectrum, add_bwd)


def _conv_tables(seq):
    n = 2 * seq
    n2 = DFT_N2
    n1 = n // n2
    k1 = n1 // 2 + 1
    k1p = -(-k1 // 8) * 8
    half = n1 // 2
    kk = jnp.arange(k1p, dtype=jnp.int32)
    valid = (kk < k1)
    t1 = jnp.arange(half, dtype=jnp.int32)
    c, s = _cs(kk[:, None] * t1[None, :], n1)
    vm = valid[:, None].astype(F32)
    f1 = jnp.stack([c * vm, -s * vm], axis=1).reshape(2 * k1p, half)
    wgt = jnp.where((kk == 0) | (kk == n1 // 2), 1.0, 2.0) * valid.astype(F32) / n
    g3 = jnp.stack([c * vm * wgt[:, None], -s * vm * wgt[:, None]], axis=1).reshape(2 * k1p, half).T
    j = jnp.arange(n2, dtype=jnp.int32)
    cr, cs_ = _cs(j[:, None] * j[None, :], n2)
    gc = jnp.concatenate([jnp.concatenate([cr, cs_], axis=1), jnp.concatenate([-cs_, cr], axis=1)], axis=0)
    gb = jnp.concatenate([jnp.concatenate([cr, -cs_], axis=1), jnp.concatenate([cs_, cr], axis=1)], axis=0)
    wr, ws = _cs(j, n)
    wstep = jnp.concatenate([jnp.broadcast_to(wr[:, None], (n2, LANES)),
                             jnp.broadcast_to(-ws[:, None], (n2, LANES))], axis=0)
    tabs = dict(n1=n1, k1=k1, k1p=k1p, half=half, wstep=wstep)
    for name, m in (("f1", f1), ("g3", g3), ("gc", gc), ("gb", gb)):
        tabs[name + "h"], tabs[name + "l"] = _split_const(m)
    return tabs


def _filter_spectrum(seq, tabs, fw):
    filt = _hyena_filters(seq, *fw)
    half, k1, k1p = tabs["half"], tabs["k1"], tabs["k1p"]
    n2 = DFT_N2
    nct = HYENA_WIDTH // LANES
    consts = [tabs[n] for n in ("f1h", "f1l", "gch", "gcl", "wstep")]
    return pl.pallas_call(
        functools.partial(_hyena_spec_kernel, half=half, k1=k1, k1p=k1p),
        grid=(2, nct),
        in_specs=[pl.BlockSpec((seq, LANES), lambda o, j: (0, o * nct + j)),
                  pl.BlockSpec((seq, LANES), lambda o, j: (0, (2 + o) * nct + j))] + [_full_spec(a) for a in consts],
        out_specs=pl.BlockSpec((None, None, k1 * 2 * n2, LANES), lambda o, j: (o, j, 0, 0)),
        out_shape=jax.ShapeDtypeStruct((2, nct, k1 * 2 * n2, LANES), BF16),
        scratch_shapes=[pltpu.VMEM((k1p * 2 * PITCH, LANES), F32), pltpu.VMEM((2 * n2, LANES), F32),
                        pltpu.VMEM((half * PITCH, LANES), F32), pltpu.VMEM((k1 * 2 * n2, LANES), F32)],
        compiler_params=_cparams(("parallel", "parallel")),
        name="hyena_filter_spectrum",
    )(filt, filt, *consts)


def _skip_first_ref(kernel_fn, *refs, **kwargs):
    return kernel_fn(*refs[1:], **kwargs)


def _merged_out(kernel_fn, args, in_specs, into, total_rows, width, dtype):
    out_shape = jax.ShapeDtypeStruct((total_rows, width), dtype)
    if into is None:
        return kernel_fn, args, in_specs, out_shape, {}
    return (functools.partial(_skip_first_ref, kernel_fn), [into] + args,
            [pl.BlockSpec(memory_space=pl.ANY)] + in_specs, out_shape, {0: 0})


def _hyena_conv(sig, sig_cols, gate_cols, proj, row0, nbatch, seq, tabs, kf, order, short_w, short_b, skip,
                total_rows=None, into=None):
    assert row0 % seq == 0
    b0 = row0 // seq
    half, k1, k1p = tabs["half"], tabs["k1"], tabs["k1p"]
    n2 = DFT_N2
    nct = HYENA_WIDTH // LANES
    sig_conv = sig is None
    if sig_conv:
        sig_arr = proj
        sig_spec = pl.BlockSpec((seq, LANES), lambda j, b: (b0 + b, sig_cols + j))
    else:
        sig_arr = sig
        sig_spec = pl.BlockSpec((seq, LANES), lambda j, b: (b, j))
    sw = short_w.astype(F32)
    sb = short_b.reshape(1, -1).astype(F32)
    consts = [tabs[n] for n in ("f1h", "f1l", "g3h", "g3l", "gch", "gcl", "gbh", "gbl", "wstep")]
    args = [sig_arr, proj, sw, sb, sw, sb, skip.reshape(1, -1).astype(F32), kf] + consts
    in_specs = [
        sig_spec,
        pl.BlockSpec((seq, LANES), lambda j, b: (b0 + b, gate_cols + j)),
        pl.BlockSpec((3, LANES), lambda j, b: (0, sig_cols + j)),
        pl.BlockSpec((1, LANES), lambda j, b: (0, sig_cols + j)),
        pl.BlockSpec((3, LANES), lambda j, b: (0, gate_cols + j)),
        pl.BlockSpec((1, LANES), lambda j, b: (0, gate_cols + j)),
        pl.BlockSpec((1, LANES), lambda j, b: (0, j)),
        pl.BlockSpec((None, None, k1 * 2 * n2, LANES), lambda j, b: (order, j, 0, 0)),
    ] + [_full_spec(a) for a in consts]
    kern = functools.partial(_hyena_conv_kernel, half=half, k1=k1, k1p=k1p, sig_conv=sig_conv)
    out_b0 = 0 if total_rows is None else b0
    kern, args, in_specs, out_shape, aliases = _merged_out(
        kern, args, in_specs, into, total_rows or nbatch * seq, HYENA_WIDTH, F32)
    return pl.pallas_call(
        kern,
        grid=(nct, nbatch),
        in_specs=in_specs,
        out_specs=pl.BlockSpec((seq, LANES), lambda j, b: (out_b0 + b, j)),
        out_shape=out_shape,
        input_output_aliases=aliases,
        scratch_shapes=[pltpu.VMEM((k1p * 2 * PITCH, LANES), F32), pltpu.VMEM((2 * n2, LANES), F32),
                        pltpu.VMEM((half * PITCH, LANES), F32), pltpu.VMEM((half * PITCH, LANES), F32)],
        compiler_params=_cparams(("parallel", "arbitrary")),
        name="hyena_conv",
    )(*args)


def _hyena_batch(proj, row0, nbatch, seq, tabs, kf, short_w, short_b, skip, total_rows, into):
    nct = HYENA_WIDTH // LANES
    z = _hyena_conv(None, 2 * nct, 0, proj, row0, nbatch, seq, tabs, kf, 0, short_w, short_b, skip[0])
    return _hyena_conv(z, 2 * nct, nct, proj, row0, nbatch, seq, tabs, kf, 1, short_w, short_b, skip[1],
                       total_rows=total_rows, into=into)


def _fnet_kernel(u_ref, chan_ref, m1_ref, gri_ref, wstep_ref, o_ref, zr_ref, zi_ref, a_ref, tw_ref, *, n1):
    n2 = DFT_N2
    slabs = 4

    def chan(g, carry):
        x = u_ref[pl.ds(pl.multiple_of(g * slabs * n2, slabs * n2), slabs * n2), :].astype(BF16)
        z = _dot(x, chan_ref[...])
        for i in range(slabs):
            r = pl.multiple_of((g * slabs + i) * PITCH, 8)
            zr_ref[pl.ds(r, n2), :] = z[i * n2:(i + 1) * n2, :LANES]
            zi_ref[pl.ds(r, n2), :] = z[i * n2:(i + 1) * n2, LANES:]
        return carry

    lax.fori_loop(0, n1 // slabs, chan, 0)

    def outer(base):
        xs = jnp.concatenate(
            [jnp.concatenate([_gather(zr_ref, base + j, n1), _gather(zi_ref, base + j, n1)], axis=0)
             for j in range(T2_GROUP)], axis=1)
        y = _dot(m1_ref[...], xs.astype(BF16))
        for j in range(T2_GROUP):
            a_ref[pl.ds(base + j, 2 * n1, stride=PITCH), :] = y[:, j * LANES:(j + 1) * LANES]

    _for_t2_groups(outer)

    def real_part(ks, xcat, tws):
        y = _dot(gri_ref[...], xcat.astype(BF16))
        return [_lane_block(y, j) for j in range(len(ks))]

    def put(k, rr, ri, y):
        zr_ref[pl.ds(pl.multiple_of(k * PITCH, 8), n2), :] = y

    _inner_blocks(a_ref, tw_ref, wstep_ref, n1, real_part, put)

    def emit(k2, carry):
        o_ref[pl.ds(pl.multiple_of(k2 * n1, n1), n1), :] = _gather(zr_ref, k2, n1).astype(o_ref.dtype)
        return carry

    lax.fori_loop(0, n2, emit, 0)


def _fnet_tables(seq):
    n2 = DFT_N2
    n1 = seq // n2
    j = jnp.arange(HEAD_DIM, dtype=jnp.int32)
    c, s = _cs(j[:, None] * j[None, :], HEAD_DIM)
    scale = (seq * HEAD_DIM) ** -0.5
    chan = jnp.concatenate([c * scale, -s * scale], axis=1).astype(BF16)
    kk = jnp.arange(n1, dtype=jnp.int32)
    c1, s1 = _cs(kk[:, None] * kk[None, :], n1)
    m1 = jnp.stack([jnp.concatenate([c1, s1], axis=1), jnp.concatenate([-s1, c1], axis=1)], axis=1)
    m1 = m1.reshape(2 * n1, 2 * n1).astype(BF16)
    t2 = jnp.arange(n2, dtype=jnp.int32)
    cr, cs_ = _cs(t2[:, None] * t2[None, :], n2)
    gri = jnp.concatenate([cr, cs_], axis=1).astype(BF16)
    wr, ws = _cs(t2, seq)
    wstep = jnp.concatenate([jnp.broadcast_to(wr[:, None], (n2, LANES)),
                             jnp.broadcast_to(-ws[:, None], (n2, LANES))], axis=0)
    return dict(n1=n1, chan=chan, m1=m1, gri=gri, wstep=wstep)


def _fnet_batch(proj, row0, nbatch, seq, tabs, total_rows, into):
    assert row0 % seq == 0
    b0 = row0 // seq
    n1 = tabs["n1"]
    consts = [tabs[n] for n in ("chan", "m1", "gri", "wstep")]
    in_specs = ([pl.BlockSpec((seq, LANES), lambda j, b: (b0 + b, OFF_FNET // LANES + j))]
                + [_full_spec(a) for a in consts])
    kern, args, in_specs, out_shape, aliases = _merged_out(
        functools.partial(_fnet_kernel, n1=n1), [proj] + consts, in_specs, into, total_rows, FNET_WIDTH, BF16)
    return pl.pallas_call(
        kern,
        grid=(FNET_HEADS, nbatch),
        in_specs=in_specs,
        out_specs=pl.BlockSpec((seq, LANES), lambda j, b: (b0 + b, j)),
        out_shape=out_shape,
        input_output_aliases=aliases,
        scratch_shapes=[pltpu.VMEM((n1 * PITCH, LANES), F32), pltpu.VMEM((n1 * PITCH, LANES), F32),
                        pltpu.VMEM((2 * n1 * PITCH, LANES), F32), pltpu.VMEM((2 * DFT_N2, LANES), F32)],
        compiler_params=_cparams(("parallel", "parallel")),
        name="fnet_mixer",
    )(*args)


def _pick_tile(t, pref):
    while t % pref:
        pref //= 2
    return pref


def kernel(x_prompt, x_sample, ln0_g, ln0_b, w_in, short_w, short_b, filt_w1, filt_b1, filt_w2, filt_b2, filt_w3, filt_b3, filt_freq, filt_w4, hyena_skip, w_fnet, b_fnet, attn_sink, w_out, ln1_g, ln1_b, w_gate, w_up, w_down, ln2_g, ln2_b):
    bp, lp, _ = x_prompt.shape
    bs, ls, _ = x_sample.shape
    tp, ts = bp * lp, bs * ls
    batches = ((0, bp, lp), (tp, bs, ls))
    tm = _pick_tile(math.gcd(tp, ts), 1024)
    tln = _pick_tile(math.gcd(tp, ts), 256)

    conv_tabs = {seq: _conv_tables(seq) for seq in {lp, ls}}
    fnet_tabs = {seq: _fnet_tables(seq) for seq in {lp, ls}}

    xf, xb = _ln0(x_prompt.reshape(tp, D_MODEL), x_sample.reshape(ts, D_MODEL), ln0_g, ln0_b, tln)
    for l in range(DEPTH):
        wd_b = w_down[l].astype(BF16)
        fw = (filt_w1[l], filt_b1[l], filt_w2[l], filt_b2[l], filt_w3[l], filt_b3[l], filt_freq[l], filt_w4[l])

        proj = _matmul([xb], w_in[l].astype(BF16), None, F32, tm, 1024, "in_proj")

        kf = {seq: _filter_spectrum(seq, conv_tabs[seq], fw) for seq in {lp, ls}}
        y_h = y_f = None
        for r0, nb, seq in batches:
            y_h = _hyena_batch(proj, r0, nb, seq, conv_tabs[seq], kf[seq], short_w[l], short_b[l], hyena_skip[l],
                               tp + ts, y_h)
            y_f = _fnet_batch(proj, r0, nb, seq, fnet_tabs[seq], tp + ts, y_f)
        y_f = _matmul([y_f], w_fnet, b_fnet[l], BF16, tm, 1024, "fnet_linear", layer=l)
        y_a = _attention(proj, attn_sink[l], tp, lp, ls)

        y = _matmul([y_h, y_f, y_a], w_out[l].astype(BF16), None, F32, tm, 512, "out_proj", resid=xf)
        xf, xb = _ln(y, ln1_g[l], ln1_b[l], tln)

        hid = _gate_up(xb, w_gate, w_up, l, _pick_tile(math.gcd(tp, ts), 2048), FF_TILE)
        y = _matmul([hid], wd_b, None, F32, _pick_tile(tm, 512), 512, "ffn_down", resid=xf)
        if l + 1 < DEPTH:
            xf, xb = _ln(y, ln2_g[l], ln2_b[l], tln)
        else:
            y_p, y_s = _ln_final(y, ln2_g[l], ln2_b[l], tln, tp)
    return y_p.reshape(bp, lp, D_MODEL), y_s.reshape(bs, ls, D_MODEL)
```

```python
import functools
import math

import jax
import jax.numpy as jnp
from jax import lax
from jax.experimental import pallas as pl
from jax.experimental.pallas import tpu as pltpu

F32 = jnp.float32
BF16 = jnp.bfloat16

D_MODEL = 4096
HEAD_DIM = 128
HYENA_WIDTH = 1024
FNET_WIDTH = 1024
ATTN_WIDTH = 2048
FNET_HEADS = 8
N_HEADS = 16
N_KV_HEADS = 4
KV_GROUP = 4
KV_WIDTH = 512
BLOCK = 128
HYENA_IN = 3 * HYENA_WIDTH
FILTER_EMB = 33
FILTER_HIDDEN = 64
FILTER_OUT = 4 * HYENA_WIDTH
OFF_FNET = HYENA_IN
OFF_Q = OFF_FNET + FNET_WIDTH
OFF_K = OFF_Q + ATTN_WIDTH
OFF_V = OFF_K + KV_WIDTH
IN_WIDTH = OFF_V + KV_WIDTH
D_FF = 11008
FF_TILE = 256
DEPTH = 2
ALPHA = (2 * DEPTH) ** 0.25
LN_EPS = 1e-5
DECAY_FAST_PCT = 0.3
DECAY_SLOW_PCT = 1.5
DECAY_TARGET = 1e-2

DFT_N2 = 128
LANES = 128
T2_GROUP = 8
PITCH = 136
CONV_PASSES = 1
SPEC_PASSES = 1
INNER_GROUP = 8
ATTN_TQ = 512
VMEM_LIMIT = 56 * 1024 * 1024


def _cparams(sem, vmem=VMEM_LIMIT):
    return pltpu.CompilerParams(dimension_semantics=sem, vmem_limit_bytes=vmem)


def _dot(a, b):
    return jnp.dot(a, b, preferred_element_type=F32)


def _split(x):
    hi = x.astype(BF16)
    lo = (x - hi.astype(F32)).astype(BF16)
    return hi, lo


def _dot3_left(m_hi, m_lo, x):
    x_hi, x_lo = _split(x)
    return _dot(m_hi, x_hi) + (_dot(m_hi, x_lo) + _dot(m_lo, x_hi))


def _split_const(m):
    m = m.astype(F32)
    hi = m.astype(BF16)
    lo = (m - hi.astype(F32)).astype(BF16)
    return hi, lo


def _cs(num, den):
    ang = (2.0 * math.pi / den) * (num % den).astype(F32)
    return jnp.cos(ang), jnp.sin(ang)


def _full_spec(a):
    return pl.BlockSpec(a.shape, lambda *_: (0,) * a.ndim)


def _ln_math(x, g, b):
    mu = jnp.mean(x, axis=-1, keepdims=True)
    xc = x - mu
    var = jnp.mean(xc * xc, axis=-1, keepdims=True)
    return xc * lax.rsqrt(var + LN_EPS) * g + b


def _ln0_kernel(xp_ref, xs_ref, g_ref, b_ref, of_ref, ob_ref, *, n_p):
    i = pl.program_id(0)

    def emit(x):
        y = _ln_math(x, g_ref[...], b_ref[...])
        of_ref[...] = y
        ob_ref[...] = y.astype(BF16)

    @pl.when(i < n_p)
    def _():
        emit(xp_ref[...])

    @pl.when(i >= n_p)
    def _():
        emit(xs_ref[...])


def _ln0(xp, xs, g, b, tm):
    tp, ts = xp.shape[0], xs.shape[0]
    n_p, n_s = tp // tm, ts // tm
    t = tp + ts
    return pl.pallas_call(
        functools.partial(_ln0_kernel, n_p=n_p),
        grid=(n_p + n_s,),
        in_specs=[
            pl.BlockSpec((tm, D_MODEL), lambda i: (jnp.minimum(i, n_p - 1), 0)),
            pl.BlockSpec((tm, D_MODEL), lambda i: (jnp.maximum(i - n_p, 0), 0)),
            pl.BlockSpec((1, D_MODEL), lambda i: (0, 0)),
            pl.BlockSpec((1, D_MODEL), lambda i: (0, 0)),
        ],
        out_specs=[
            pl.BlockSpec((tm, D_MODEL), lambda i: (i, 0)),
            pl.BlockSpec((tm, D_MODEL), lambda i: (i, 0)),
        ],
        out_shape=[jax.ShapeDtypeStruct((t, D_MODEL), F32), jax.ShapeDtypeStruct((t, D_MODEL), BF16)],
        compiler_params=_cparams(("parallel",)),
        name="ln0",
    )(xp, xs, g.reshape(1, -1), b.reshape(1, -1))


def _ln_kernel(y_ref, g_ref, b_ref, ob_ref, mu_ref, rs_ref):
    y = y_ref[...]
    mu = jnp.mean(y, axis=-1, keepdims=True)
    xc = y - mu
    rs = lax.rsqrt(jnp.mean(xc * xc, axis=-1, keepdims=True) + LN_EPS)
    ob_ref[...] = (xc * rs * g_ref[...] + b_ref[...]).astype(BF16)
    mu_ref[...] = jnp.broadcast_to(mu, mu_ref.shape)
    rs_ref[...] = jnp.broadcast_to(rs, rs_ref.shape)


def _ln(y, g, b, tm):
    t = y.shape[0]
    row = pl.BlockSpec((tm, D_MODEL), lambda i: (i, 0))
    vec = pl.BlockSpec((1, D_MODEL), lambda i: (0, 0))
    stat = pl.BlockSpec((tm, LANES), lambda i: (i, 0))
    return pl.pallas_call(
        _ln_kernel,
        grid=(t // tm,),
        in_specs=[row, vec, vec],
        out_specs=[row, stat, stat],
        out_shape=[jax.ShapeDtypeStruct((t, D_MODEL), BF16), jax.ShapeDtypeStruct((t, LANES), F32),
                   jax.ShapeDtypeStruct((t, LANES), F32)],
        compiler_params=_cparams(("parallel",)),
        name="layer_norm",
    )(y, g.reshape(1, -1), b.reshape(1, -1))


def _ln_final_kernel(y_ref, g_ref, b_ref, op_ref, os_ref, *, n_p):
    i = pl.program_id(0)
    y = _ln_math(y_ref[...], g_ref[...], b_ref[...])

    @pl.when(i < n_p)
    def _():
        op_ref[...] = y

    @pl.when(i >= n_p)
    def _():
        os_ref[...] = y


def _ln_final(y, g, b, tm, tp):
    t = y.shape[0]
    n_p = tp // tm
    row = pl.BlockSpec((tm, D_MODEL), lambda i: (i, 0))
    vec = pl.BlockSpec((1, D_MODEL), lambda i: (0, 0))
    return pl.pallas_call(
        functools.partial(_ln_final_kernel, n_p=n_p),
        grid=(t // tm,),
        in_specs=[row, vec, vec],
        out_specs=[
            pl.BlockSpec((tm, D_MODEL), lambda i: (jnp.minimum(i, n_p - 1), 0)),
            pl.BlockSpec((tm, D_MODEL), lambda i: (jnp.maximum(i - n_p, 0), 0)),
        ],
        out_shape=[jax.ShapeDtypeStruct((tp, D_MODEL), F32), jax.ShapeDtypeStruct((t - tp, D_MODEL), F32)],
        compiler_params=_cparams(("arbitrary",)),
        name="layer_norm_final",
    )(y, g.reshape(1, -1), b.reshape(1, -1))


def _mm_kernel(*refs, widths, has_bias, resid_mode):
    n_a = len(widths)
    a_refs = refs[:n_a]
    w_ref = refs[n_a]
    o_ref = refs[-1]
    nxt = n_a + 1
    acc = None
    off = 0
    for a_ref, wd in zip(a_refs, widths):
        part = _dot(a_ref[...].astype(BF16), w_ref[off:off + wd, :].astype(BF16))
        acc = part if acc is None else acc + part
        off += wd
    if has_bias:
        acc = acc + refs[nxt][...]
        nxt += 1
    if resid_mode == "plain":
        acc = ALPHA * refs[nxt][...] + acc
    elif resid_mode == "normalise":
        y_ref, mu_ref, rs_ref, g_ref, b_ref = refs[nxt:nxt + 5]
        reps = o_ref.shape[1] // LANES
        mu = jnp.concatenate([mu_ref[...]] * reps, axis=1)
        rs = jnp.concatenate([rs_ref[...]] * reps, axis=1)
        acc = ALPHA * ((y_ref[...] - mu) * rs * g_ref[...] + b_ref[...]) + acc
    o_ref[...] = acc.astype(o_ref.dtype)


def _matmul(a_list, w, bias, out_dtype, tm, tn, name, resid=None, layer=None):
    t = a_list[0].shape[0]
    k, n = w.shape[-2:]
    widths = tuple(a.shape[1] for a in a_list)
    assert sum(widths) == k and t % tm == 0 and n % tn == 0
    in_specs = [pl.BlockSpec((tm, wd), lambda i, j: (i, 0)) for wd in widths]
    if layer is None:
        in_specs.append(pl.BlockSpec((k, tn), lambda i, j: (0, j)))
    else:
        in_specs.append(pl.BlockSpec((None, k, tn), lambda i, j: (layer, 0, j)))
    args = list(a_list) + [w]
    if bias is not None:
        in_specs.append(pl.BlockSpec((1, tn), lambda i, j: (0, j)))
        args.append(bias.reshape(1, n).astype(F32))
    resid_mode = None
    if isinstance(resid, tuple):
        resid_mode = "normalise"
        y, mu, rs, g, b = resid
        stat = pl.BlockSpec((tm, LANES), lambda i, j: (i, 0))
        vec = pl.BlockSpec((1, tn), lambda i, j: (0, j))
        in_specs += [pl.BlockSpec((tm, tn), lambda i, j: (i, j)), stat, stat, vec, vec]
        args += [y, mu, rs, g.reshape(1, n).astype(F32), b.reshape(1, n).astype(F32)]
    elif resid is not None:
        resid_mode = "plain"
        in_specs.append(pl.BlockSpec((tm, tn), lambda i, j: (i, j)))
        args.append(resid)
    return pl.pallas_call(
        functools.partial(_mm_kernel, widths=widths, has_bias=bias is not None, resid_mode=resid_mode),
        grid=(t // tm, n // tn),
        in_specs=in_specs,
        out_specs=pl.BlockSpec((tm, tn), lambda i, j: (i, j)),
        out_shape=jax.ShapeDtypeStruct((t, n), out_dtype),
        compiler_params=_cparams(("parallel", "parallel")),
        name=name,
    )(*args)


def _gate_up_kernel(x_ref, wg_ref, wu_ref, o_ref):
    x = x_ref[...]
    g = _dot(x, wg_ref[...].astype(BF16))
    u = _dot(x, wu_ref[...].astype(BF16))
    o_ref[...] = (g * (1.0 / (1.0 + jnp.exp(-g))) * u).astype(o_ref.dtype)


def _gate_up(x, wg, wu, layer, tm, tn):
    t, k = x.shape
    n = wg.shape[-1]
    return pl.pallas_call(
        _gate_up_kernel,
        grid=(t // tm, n // tn),
        in_specs=[
            pl.BlockSpec((tm, k), lambda i, j: (i, 0)),
            pl.BlockSpec((None, k, tn), lambda i, j: (layer, 0, j)),
            pl.BlockSpec((None, k, tn), lambda i, j: (layer, 0, j)),
        ],
        out_specs=pl.BlockSpec((tm, tn), lambda i, j: (i, j)),
        out_shape=jax.ShapeDtypeStruct((t, n), BF16),
        compiler_params=_cparams(("parallel", "parallel")),
        name="ffn_gate_up",
    )(x, wg, wu)


def _attn_kernel(q_ref, kp_ref, kc_ref, kn_ref, vp_ref, vc_ref, vn_ref, bias_ref, sink_ref, o_ref, kbuf, vbuf,
                 *, tq, tiles_p, per_p, per_s):
    g = pl.program_id(0)
    h = pl.program_id(1)
    in_p = g < tiles_p
    n_loc = jnp.where(in_p, g % per_p, (g - tiles_p) % per_s)
    n_seq = jnp.where(in_p, per_p, per_s)
    pen_prev = jnp.where(n_loc > 0, 0.0, -jnp.inf)
    pen_next = jnp.where(n_loc < n_seq - 1, 0.0, -jnp.inf)

    kbuf[0:BLOCK, :] = kp_ref[...].astype(BF16)
    kbuf[BLOCK:BLOCK + tq, :] = kc_ref[...].astype(BF16)
    kbuf[BLOCK + tq:, :] = kn_ref[...].astype(BF16)
    vbuf[:, 0:BLOCK] = vp_ref[...].T.astype(BF16)
    for c0 in range(0, tq, BLOCK):
        vbuf[:, BLOCK + c0:2 * BLOCK + c0] = vc_ref[c0:c0 + BLOCK, :].T.astype(BF16)
    vbuf[:, BLOCK + tq:] = vn_ref[...].T.astype(BF16)

    key = lax.broadcasted_iota(jnp.int32, (3 * BLOCK, BLOCK), 0)
    col_prev = jnp.where(key < BLOCK, pen_prev, 0.0)
    col_next = jnp.where(key >= 2 * BLOCK, pen_next, 0.0)
    scale = HEAD_DIM ** -0.5
    dn = (((1,), (1,)), ((), ()))
    nsb = tq // BLOCK
    for sb in range(nsb):
        r0 = sb * BLOCK
        k3 = kbuf[r0:r0 + 3 * BLOCK, :]
        v3 = vbuf[:, r0:r0 + 3 * BLOCK]
        for gi in range(KV_GROUP):
            sink = sink_ref[h * KV_GROUP + gi]
            q = q_ref[r0:r0 + BLOCK, gi * HEAD_DIM:(gi + 1) * HEAD_DIM].astype(BF16)
            s = lax.dot_general(k3, q, dn, preferred_element_type=F32) * scale
            s = s + bias_ref[gi * 3 * BLOCK:(gi + 1) * 3 * BLOCK, :]
            if sb == 0:
                s = s + col_prev
            if sb == nsb - 1:
                s = s + col_next
            m = jnp.maximum(jnp.max(s, axis=0, keepdims=True), sink)
            p = jnp.exp(s - m)
            denom = jnp.sum(p, axis=0, keepdims=True) + jnp.exp(sink - m)
            o_t = _dot(v3, p.astype(BF16)) * (1.0 / denom)
            o_ref[r0:r0 + BLOCK, gi * HEAD_DIM:(gi + 1) * HEAD_DIM] = o_t.T.astype(o_ref.dtype)


def _attention(proj, sink, tp, lp, ls):
    t = proj.shape[0]
    tq = math.gcd(ATTN_TQ, math.gcd(lp, ls))
    nblk = t // BLOCK
    bpt = tq // BLOCK
    slopes = 2.0 ** (-8.0 * jnp.arange(1, N_HEADS + 1, dtype=F32) / N_HEADS)
    qi = jnp.arange(BLOCK)[:, None]
    ki = jnp.arange(3 * BLOCK)[None, :]
    dist = jnp.abs(qi + BLOCK - ki)
    bias = jnp.where(dist[None] <= BLOCK, -slopes[:, None, None] * dist[None].astype(F32), -jnp.inf)
    bias = jnp.swapaxes(bias, 1, 2).reshape(N_HEADS * 3 * BLOCK, BLOCK)
    qc = OFF_Q // (KV_GROUP * HEAD_DIM)
    kc = OFF_K // HEAD_DIM
    vc = OFF_V // HEAD_DIM
    prev = lambda g: jnp.maximum(g * bpt - 1, 0)
    nxt = lambda g: jnp.minimum((g + 1) * bpt, nblk - 1)
    halo = (BLOCK, HEAD_DIM)
    cur = (tq, HEAD_DIM)
    return pl.pallas_call(
        functools.partial(_attn_kernel, tq=tq, tiles_p=tp // tq, per_p=lp // tq, per_s=ls // tq),
        grid=(t // tq, N_KV_HEADS),
        in_specs=[
            pl.BlockSpec((tq, KV_GROUP * HEAD_DIM), lambda g, h: (g, qc + h)),
            pl.BlockSpec(halo, lambda g, h: (prev(g), kc + h)),
            pl.BlockSpec(cur, lambda g, h: (g, kc + h)),
            pl.BlockSpec(halo, lambda g, h: (nxt(g), kc + h)),
            pl.BlockSpec(halo, lambda g, h: (prev(g), vc + h)),
            pl.BlockSpec(cur, lambda g, h: (g, vc + h)),
            pl.BlockSpec(halo, lambda g, h: (nxt(g), vc + h)),
            pl.BlockSpec((KV_GROUP * 3 * BLOCK, BLOCK), lambda g, h: (h, 0)),
            pl.BlockSpec(memory_space=pltpu.SMEM),
        ],
        out_specs=pl.BlockSpec((tq, KV_GROUP * HEAD_DIM), lambda g, h: (g, h)),
        out_shape=jax.ShapeDtypeStruct((t, ATTN_WIDTH), BF16),
        scratch_shapes=[pltpu.VMEM((tq + 2 * BLOCK, HEAD_DIM), BF16), pltpu.VMEM((HEAD_DIM, tq + 2 * BLOCK), BF16)],
        compiler_params=_cparams(("parallel", "parallel")),
        name="band_attention",
    )(proj, proj, proj, proj, proj, proj, proj, bias, sink.astype(F32))


def _filter_kernel(fvec_ref, w1_ref, b1_ref, w2_ref, b2_ref, w3_ref, b3_ref, fr_ref, w4_ref, delta_ref, o_ref,
                   *, seq, tl):
    r0 = pl.program_id(0) * tl
    row = (lax.broadcasted_iota(jnp.int32, (tl, 128), 0) + r0).astype(F32)
    lane = lax.broadcasted_iota(jnp.int32, (tl, 128), 1)
    t = row * (1.0 / (seq - 1))
    ang = (row * (2.0 * math.pi / seq)) * fvec_ref[...]
    z = jnp.where(lane == 0, t,
                  jnp.where(lane <= 16, jnp.cos(ang), jnp.where(lane <= 32, -jnp.sin(ang), 0.0)))
    def dot3(x, w_ref, cols=slice(None)):
        x_hi, x_lo = _split(x)
        w_hi, w_lo = w_ref[0, :, cols], w_ref[1, :, cols]
        return _dot(x_hi, w_hi) + (_dot(x_lo, w_hi) + _dot(x_hi, w_lo))

    h = jnp.sin(fr_ref[0:1, :] * (dot3(z, w1_ref) + b1_ref[...]))
    h = jnp.sin(fr_ref[1:2, :] * (dot3(h, w2_ref) + b2_ref[...]))
    h = jnp.sin(fr_ref[2:3, :] * (dot3(h, w3_ref) + b3_ref[...]))
    trow = (lax.broadcasted_iota(jnp.int32, (tl, HYENA_WIDTH), 0) + r0)
    decay = jnp.exp(-(trow.astype(F32) * (1.0 / (seq - 1))) * delta_ref[...])
    first = trow == 0
    for part in range(4):
        sl = slice(part * HYENA_WIDTH, (part + 1) * HYENA_WIDTH)
        v = dot3(h, w4_ref, sl) * decay
        if part >= 2:
            v = jnp.where(first, 0.0, v)
        o_ref[:, sl] = v


def _hyena_filters(seq, w1, b1, w2, b2, w3, b3, freq, w4):
    tl = min(seq, 512)
    bands = (FILTER_EMB - 1) // 2
    f = jnp.linspace(1e-4, bands - 1, bands, dtype=F32)
    fvec = jnp.zeros((1, 128), F32).at[0, 1:1 + bands].set(f).at[0, 1 + bands:1 + 2 * bands].set(f)
    w1p = jnp.zeros((128, FILTER_HIDDEN), F32).at[:FILTER_EMB].set(w1.astype(F32))
    max_decay = math.log(DECAY_TARGET) / DECAY_FAST_PCT
    min_decay = math.log(DECAY_TARGET) / DECAY_SLOW_PCT
    deltas = jnp.abs(jnp.linspace(min_decay, max_decay, HYENA_WIDTH, dtype=F32)).reshape(1, -1)
    planes = lambda w: jnp.stack(_split_const(w))
    args = [fvec, planes(w1p), b1.reshape(1, -1).astype(F32), planes(w2), b2.reshape(1, -1).astype(F32),
            planes(w3), b3.reshape(1, -1).astype(F32), freq.astype(F32), planes(w4), deltas]
    return pl.pallas_call(
        functools.partial(_filter_kernel, seq=seq, tl=tl),
        grid=(seq // tl,),
        in_specs=[_full_spec(a) for a in args],
        out_specs=pl.BlockSpec((tl, FILTER_OUT), lambda i: (i, 0)),
        out_shape=jax.ShapeDtypeStruct((seq, FILTER_OUT), F32),
        compiler_params=_cparams(("parallel",)),
        name="hyena_filter",
    )(*args)


def _dotp(mh_ref, ml_ref, x, passes):
    if passes == 1:
        return _dot(mh_ref[...], x.astype(BF16))
    return _dot3_left(mh_ref[...], ml_ref[...], x)


def _gather(ref, t2, n):
    return ref[pl.ds(t2, n, stride=PITCH), :]


def _fill_pitched(u_ref, p_ref, half, w_ref=None, b_ref=None):
    n2 = DFT_N2
    seq = half * n2
    rid = lax.broadcasted_iota(jnp.int32, (n2, LANES), 0)

    def step(t1, carry):
        r0 = pl.multiple_of(t1 * n2, n2)
        cur = u_ref[pl.ds(r0, n2), :]
        if w_ref is not None:
            before = u_ref[pl.ds(pl.multiple_of(jnp.maximum(r0 - 8, 0), 8), 8), :][7:8, :]
            after = u_ref[pl.ds(pl.multiple_of(jnp.minimum(r0 + n2, seq - 8), 8), 8), :][0:1, :]
            before = jnp.where(t1 > 0, before, 0.0)
            after = jnp.where(t1 < half - 1, after, 0.0)
            prev = jnp.where(rid == 0, before, pltpu.roll(cur, 1, 0))
            nxt = jnp.where(rid == n2 - 1, after, pltpu.roll(cur, n2 - 1, 0))
            cur = prev * w_ref[0:1, :] + cur * w_ref[1:2, :] + nxt * w_ref[2:3, :] + b_ref[...]
        p_ref[pl.ds(pl.multiple_of(t1 * PITCH, 8), n2), :] = cur
        return carry

    lax.fori_loop(0, half, step, 0)


def _for_t2_groups(body):
    def step(g, carry):
        body(g * T2_GROUP)
        return carry

    lax.fori_loop(0, DFT_N2 // T2_GROUP, step, 0, unroll=2)


def _outer_stage(p_ref, half, mh_ref, ml_ref, a_ref, nrows, passes):
    def body(base):
        xs = jnp.concatenate([_gather(p_ref, base + j, half) for j in range(T2_GROUP)], axis=1)
        y = _dotp(mh_ref, ml_ref, xs, passes)
        for j in range(T2_GROUP):
            a_ref[pl.ds(base + j, nrows, stride=PITCH), :] = y[:, j * LANES:(j + 1) * LANES]

    _for_t2_groups(body)


def _lane_block(x, j):
    return x[:, j * LANES:(j + 1) * LANES]


def _inner_blocks(a_ref, tw_ref, wstep_ref, nblocks, group_fn, store_fn):
    n2 = DFT_N2
    tw_ref[0:n2, :] = jnp.ones((n2, LANES), F32)
    tw_ref[n2:, :] = jnp.zeros((n2, LANES), F32)

    def run(k0, count):
        twr = tw_ref[0:n2, :]
        twi = tw_ref[n2:, :]
        wr = wstep_ref[0:n2, :]
        wi = wstep_ref[n2:, :]
        ks, rows, tws, blocks = [], [], [], []
        for j in range(count):
            k = k0 + j
            rr = pl.multiple_of(k * 2 * PITCH, 8)
            ri = pl.multiple_of(k * 2 * PITCH + PITCH, 8)
            ar = a_ref[pl.ds(rr, n2), :]
            ai = a_ref[pl.ds(ri, n2), :]
            ks.append(k)
            rows.append((rr, ri))
            tws.append((twr, twi))
            blocks.append(jnp.concatenate([ar * twr - ai * twi, ar * twi + ai * twr], axis=0))
            twr, twi = twr * wr - twi * wi, twr * wi + twi * wr
        vals = group_fn(ks, jnp.concatenate(blocks, axis=1), tws)
        for k, (rr, ri), val in zip(ks, rows, vals):
            store_fn(k, rr, ri, val)
        tw_ref[0:n2, :] = twr
        tw_ref[n2:, :] = twi

    def step(g, carry):
        run(g * INNER_GROUP, INNER_GROUP)
        return carry

    lax.fori_loop(0, nblocks // INNER_GROUP, step, 0)
    if nblocks % INNER_GROUP:
        run(nblocks - nblocks % INNER_GROUP, nblocks % INNER_GROUP)


def _hyena_conv_kernel(sig_ref, gate_ref, wsig_ref, bsig_ref, wgate_ref, bgate_ref, skip_ref, kf_ref,
                       f1h_ref, f1l_ref, g3h_ref, g3l_ref, gch_ref, gcl_ref, gbh_ref, gbl_ref, wstep_ref,
                       o_ref, a_ref, tw_ref, pv_ref, pg_ref, *, half, k1, k1p, sig_conv):
    n2 = DFT_N2
    if sig_conv:
        _fill_pitched(sig_ref, pv_ref, half, wsig_ref, bsig_ref)
    else:
        _fill_pitched(sig_ref, pv_ref, half)
    _fill_pitched(gate_ref, pg_ref, half, wgate_ref, bgate_ref)
    _outer_stage(pv_ref, half, f1h_ref, f1l_ref, a_ref, 2 * k1p, CONV_PASSES)

    def block(ks, xcat, tws):
        x = _dotp(gch_ref, gcl_ref, xcat, CONV_PASSES)
        prods = []
        for j, k in enumerate(ks):
            xr, xi = _lane_block(x[:n2], j), _lane_block(x[n2:], j)
            k0 = pl.multiple_of(k * 2 * n2, 2 * n2)
            kr = kf_ref[pl.ds(k0, n2), :].astype(F32)
            ki = kf_ref[pl.ds(k0 + n2, n2), :].astype(F32)
            prods.append(jnp.concatenate([xr * kr - xi * ki, xr * ki + xi * kr], axis=0))
        bb = _dotp(gbh_ref, gbl_ref, jnp.concatenate(prods, axis=1), CONV_PASSES)
        out = []
        for j, (twr, twi) in enumerate(tws):
            br, bi = _lane_block(bb[:n2], j), _lane_block(bb[n2:], j)
            out.append((br * twr + bi * twi, bi * twr - br * twi))
        return out

    def put(k, rr, ri, vals):
        a_ref[pl.ds(rr, n2), :] = vals[0]
        a_ref[pl.ds(ri, n2), :] = vals[1]

    _inner_blocks(a_ref, tw_ref, wstep_ref, k1, block, put)

    def finish(base):
        bcat = jnp.concatenate([_gather(a_ref, base + j, 2 * k1p) for j in range(T2_GROUP)], axis=1)
        y = _dotp(g3h_ref, g3l_ref, bcat, CONV_PASSES)
        for j in range(T2_GROUP):
            v = _gather(pv_ref, base + j, half)
            gate = _gather(pg_ref, base + j, half)
            pg_ref[pl.ds(base + j, half, stride=PITCH), :] = gate * (y[:, j * LANES:(j + 1) * LANES]
                                                                    + v * skip_ref[...])

    _for_t2_groups(finish)

    def emit(t1, carry):
        o_ref[pl.ds(pl.multiple_of(t1 * n2, n2), n2), :] = pg_ref[pl.ds(pl.multiple_of(t1 * PITCH, 8), n2), :]
        return carry

    lax.fori_loop(0, half, emit, 0)


def _hyena_spec_kernel(ff_ref, fb_ref, f1h_ref, f1l_ref, gch_ref, gcl_ref, wstep_ref, o_ref, a_ref, tw_ref, pf_ref,
                       acc_ref, *, half, k1, k1p):
    n2 = DFT_N2
    _fill_pitched(ff_ref, pf_ref, half)
    _outer_stage(pf_ref, half, f1h_ref, f1l_ref, a_ref, 2 * k1p, SPEC_PASSES)

    def spectrum(ks, xcat, tws):
        x = _dotp(gch_ref, gcl_ref, xcat, SPEC_PASSES)
        return [_lane_block(x, j) for j in range(len(ks))]

    def put_fwd(k, rr, ri, x):
        acc_ref[pl.ds(pl.multiple_of(k * 2 * n2, 2 * n2), 2 * n2), :] = x

    _inner_blocks(a_ref, tw_ref, wstep_ref, k1, spectrum, put_fwd)
    _fill_pitched(fb_ref, pf_ref, half)
    _outer_stage(pf_ref, half, f1h_ref, f1l_ref, a_ref, 2 * k1p, SPEC_PASSES)

    def add_bwd(k, rr, ri, x):
        k0 = pl.multiple_of(k * 2 * n2, 2 * n2)
        o_ref[pl.ds(k0, n2), :] = (acc_ref[pl.ds(k0, n2), :] + x[:n2]).astype(o_ref.dtype)
        o_ref[pl.ds(k0 + n2, n2), :] = (acc_ref[pl.ds(k0 + n2, n2), :] - x[n2:]).astype(o_ref.dtype)

    _inner_blocks(a_ref, tw_ref, wstep_ref, k1, spectrum, add_bwd)


def _conv_tables(seq):
    n = 2 * seq
    n2 = DFT_N2
    n1 = n // n2
    k1 = n1 // 2 + 1
    k1p = -(-k1 // 8) * 8
    half = n1 // 2
    kk = jnp.arange(k1p, dtype=jnp.int32)
    valid = (kk < k1)
    t1 = jnp.arange(half, dtype=jnp.int32)
    c, s = _cs(kk[:, None] * t1[None, :], n1)
    vm = valid[:, None].astype(F32)
    f1 = jnp.stack([c * vm, -s * vm], axis=1).reshape(2 * k1p, half)
    wgt = jnp.where((kk == 0) | (kk == n1 // 2), 1.0, 2.0) * valid.astype(F32) / n
    g3 = jnp.stack([c * vm * wgt[:, None], -s * vm * wgt[:, None]], axis=1).reshape(2 * k1p, half).T
    j = jnp.arange(n2, dtype=jnp.int32)
    cr, cs_ = _cs(j[:, None] * j[None, :], n2)
    gc = jnp.concatenate([jnp.concatenate([cr, cs_], axis=1), jnp.concatenate([-cs_, cr], axis=1)], axis=0)
    gb = jnp.concatenate([jnp.concatenate([cr, -cs_], axis=1), jnp.concatenate([cs_, cr], axis=1)], axis=0)
    wr, ws = _cs(j, n)
    wstep = jnp.concatenate([jnp.broadcast_to(wr[:, None], (n2, LANES)),
                             jnp.broadcast_to(-ws[:, None], (n2, LANES))], axis=0)
    tabs = dict(n1=n1, k1=k1, k1p=k1p, half=half, wstep=wstep)
    for name, m in (("f1", f1), ("g3", g3), ("gc", gc), ("gb", gb)):
        tabs[name + "h"], tabs[name + "l"] = _split_const(m)
    return tabs


def _filter_spectrum(seq, tabs, fw):
    filt = _hyena_filters(seq, *fw)
    half, k1, k1p = tabs["half"], tabs["k1"], tabs["k1p"]
    n2 = DFT_N2
    nct = HYENA_WIDTH // LANES
    consts = [tabs[n] for n in ("f1h", "f1l", "gch", "gcl", "wstep")]
    return pl.pallas_call(
        functools.partial(_hyena_spec_kernel, half=half, k1=k1, k1p=k1p),
        grid=(2, nct),
        in_specs=[pl.BlockSpec((seq, LANES), lambda o, j: (0, o * nct + j)),
                  pl.BlockSpec((seq, LANES), lambda o, j: (0, (2 + o) * nct + j))] + [_full_spec(a) for a in consts],
        out_specs=pl.BlockSpec((None, None, k1 * 2 * n2, LANES), lambda o, j: (o, j, 0, 0)),
        out_shape=jax.ShapeDtypeStruct((2, nct, k1 * 2 * n2, LANES), BF16),
        scratch_shapes=[pltpu.VMEM((k1p * 2 * PITCH, LANES), F32), pltpu.VMEM((2 * n2, LANES), F32),
                        pltpu.VMEM((half * PITCH, LANES), F32), pltpu.VMEM((k1 * 2 * n2, LANES), F32)],
        compiler_params=_cparams(("parallel", "parallel")),
        name="hyena_filter_spectrum",
    )(filt, filt, *consts)


def _skip_first_ref(kernel_fn, *refs, **kwargs):
    return kernel_fn(*refs[1:], **kwargs)


def _merged_out(kernel_fn, args, in_specs, into, total_rows, width, dtype):
    out_shape = jax.ShapeDtypeStruct((total_rows, width), dtype)
    if into is None:
        return kernel_fn, args, in_specs, out_shape, {}
    return (functools.partial(_skip_first_ref, kernel_fn), [into] + args,
            [pl.BlockSpec(memory_space=pl.ANY)] + in_specs, out_shape, {0: 0})


def _hyena_conv(sig, sig_cols, gate_cols, proj, row0, nbatch, seq, tabs, kf, order, short_w, short_b, skip,
                total_rows=None, into=None):
    assert row0 % seq == 0
    b0 = row0 // seq
    half, k1, k1p = tabs["half"], tabs["k1"], tabs["k1p"]
    n2 = DFT_N2
    nct = HYENA_WIDTH // LANES
    sig_conv = sig is None
    if sig_conv:
        sig_arr = proj
        sig_spec = pl.BlockSpec((seq, LANES), lambda j, b: (b0 + b, sig_cols + j))
    else:
        sig_arr = sig
        sig_spec = pl.BlockSpec((seq, LANES), lambda j, b: (b, j))
    sw = short_w.astype(F32)
    sb = short_b.reshape(1, -1).astype(F32)
    consts = [tabs[n] for n in ("f1h", "f1l", "g3h", "g3l", "gch", "gcl", "gbh", "gbl", "wstep")]
    args = [sig_arr, proj, sw, sb, sw, sb, skip.reshape(1, -1).astype(F32), kf] + consts
    in_specs = [
        sig_spec,
        pl.BlockSpec((seq, LANES), lambda j, b: (b0 + b, gate_cols + j)),
        pl.BlockSpec((3, LANES), lambda j, b: (0, sig_cols + j)),
        pl.BlockSpec((1, LANES), lambda j, b: (0, sig_cols + j)),
        pl.BlockSpec((3, LANES), lambda j, b: (0, gate_cols + j)),
        pl.BlockSpec((1, LANES), lambda j, b: (0, gate_cols + j)),
        pl.BlockSpec((1, LANES), lambda j, b: (0, j)),
        pl.BlockSpec((None, None, k1 * 2 * n2, LANES), lambda j, b: (order, j, 0, 0)),
    ] + [_full_spec(a) for a in consts]
    kern = functools.partial(_hyena_conv_kernel, half=half, k1=k1, k1p=k1p, sig_conv=sig_conv)
    out_b0 = 0 if total_rows is None else b0
    kern, args, in_specs, out_shape, aliases = _merged_out(
        kern, args, in_specs, into, total_rows or nbatch * seq, HYENA_WIDTH, F32)
    return pl.pallas_call(
        kern,
        grid=(nct, nbatch),
        in_specs=in_specs,
        out_specs=pl.BlockSpec((seq, LANES), lambda j, b: (out_b0 + b, j)),
        out_shape=out_shape,
        input_output_aliases=aliases,
        scratch_shapes=[pltpu.VMEM((k1p * 2 * PITCH, LANES), F32), pltpu.VMEM((2 * n2, LANES), F32),
                        pltpu.VMEM((half * PITCH, LANES), F32), pltpu.VMEM((half * PITCH, LANES), F32)],
        compiler_params=_cparams(("parallel", "arbitrary")),
        name="hyena_conv",
    )(*args)


def _hyena_batch(proj, row0, nbatch, seq, tabs, kf, short_w, short_b, skip, total_rows, into):
    nct = HYENA_WIDTH // LANES
    z = _hyena_conv(None, 2 * nct, 0, proj, row0, nbatch, seq, tabs, kf, 0, short_w, short_b, skip[0])
    return _hyena_conv(z, 2 * nct, nct, proj, row0, nbatch, seq, tabs, kf, 1, short_w, short_b, skip[1],
                       total_rows=total_rows, into=into)


def _fnet_kernel(u_ref, chan_ref, m1_ref, gri_ref, wstep_ref, o_ref, zr_ref, zi_ref, a_ref, tw_ref, *, n1):
    n2 = DFT_N2
    slabs = 4

    def chan(g, carry):
        x = u_ref[pl.ds(pl.multiple_of(g * slabs * n2, slabs * n2), slabs * n2), :].astype(BF16)
        z = _dot(x, chan_ref[...])
        for i in range(slabs):
            r = pl.multiple_of((g * slabs + i) * PITCH, 8)
            zr_ref[pl.ds(r, n2), :] = z[i * n2:(i + 1) * n2, :LANES]
            zi_ref[pl.ds(r, n2), :] = z[i * n2:(i + 1) * n2, LANES:]
        return carry

    lax.fori_loop(0, n1 // slabs, chan, 0)

    def outer(base):
        xs = jnp.concatenate(
            [jnp.concatenate([_gather(zr_ref, base + j, n1), _gather(zi_ref, base + j, n1)], axis=0)
             for j in range(T2_GROUP)], axis=1)
        y = _dot(m1_ref[...], xs.astype(BF16))
        for j in range(T2_GROUP):
            a_ref[pl.ds(base + j, 2 * n1, stride=PITCH), :] = y[:, j * LANES:(j + 1) * LANES]

    _for_t2_groups(outer)

    def real_part(ks, xcat, tws):
        y = _dot(gri_ref[...], xcat.astype(BF16))
        return [_lane_block(y, j) for j in range(len(ks))]

    def put(k, rr, ri, y):
        zr_ref[pl.ds(pl.multiple_of(k * PITCH, 8), n2), :] = y

    _inner_blocks(a_ref, tw_ref, wstep_ref, n1, real_part, put)

    def emit(k2, carry):
        o_ref[pl.ds(pl.multiple_of(k2 * n1, n1), n1), :] = _gather(zr_ref, k2, n1).astype(o_ref.dtype)
        return carry

    lax.fori_loop(0, n2, emit, 0)


def _fnet_tables(seq):
    n2 = DFT_N2
    n1 = seq // n2
    j = jnp.arange(HEAD_DIM, dtype=jnp.int32)
    c, s = _cs(j[:, None] * j[None, :], HEAD_DIM)
    scale = (seq * HEAD_DIM) ** -0.5
    chan = jnp.concatenate([c * scale, -s * scale], axis=1).astype(BF16)
    kk = jnp.arange(n1, dtype=jnp.int32)
    c1, s1 = _cs(kk[:, None] * kk[None, :], n1)
    m1 = jnp.stack([jnp.concatenate([c1, s1], axis=1), jnp.concatenate([-s1, c1], axis=1)], axis=1)
    m1 = m1.reshape(2 * n1, 2 * n1).astype(BF16)
    t2 = jnp.arange(n2, dtype=jnp.int32)
    cr, cs_ = _cs(t2[:, None] * t2[None, :], n2)
    gri = jnp.concatenate([cr, cs_], axis=1).astype(BF16)
    wr, ws = _cs(t2, seq)
    wstep = jnp.concatenate([jnp.broadcast_to(wr[:, None], (n2, LANES)),
                             jnp.broadcast_to(-ws[:, None], (n2, LANES))], axis=0)
    return dict(n1=n1, chan=chan, m1=m1, gri=gri, wstep=wstep)


def _fnet_batch(proj, row0, nbatch, seq, tabs, total_rows, into):
    assert row0 % seq == 0
    b0 = row0 // seq
    n1 = tabs["n1"]
    consts = [tabs[n] for n in ("chan", "m1", "gri", "wstep")]
    in_specs = ([pl.BlockSpec((seq, LANES), lambda j, b: (b0 + b, OFF_FNET // LANES + j))]
                + [_full_spec(a) for a in consts])
    kern, args, in_specs, out_shape, aliases = _merged_out(
        functools.partial(_fnet_kernel, n1=n1), [proj] + consts, in_specs, into, total_rows, FNET_WIDTH, BF16)
    return pl.pallas_call(
        kern,
        grid=(FNET_HEADS, nbatch),
        in_specs=in_specs,
        out_specs=pl.BlockSpec((seq, LANES), lambda j, b: (b0 + b, j)),
        out_shape=out_shape,
        input_output_aliases=aliases,
        scratch_shapes=[pltpu.VMEM((n1 * PITCH, LANES), F32), pltpu.VMEM((n1 * PITCH, LANES), F32),
                        pltpu.VMEM((2 * n1 * PITCH, LANES), F32), pltpu.VMEM((2 * DFT_N2, LANES), F32)],
        compiler_params=_cparams(("parallel", "parallel")),
        name="fnet_mixer",
    )(*args)


def _pick_tile(t, pref):
    while t % pref:
        pref //= 2
    return pref


def kernel(x_prompt, x_sample, ln0_g, ln0_b, w_in, short_w, short_b, filt_w1, filt_b1, filt_w2, filt_b2, filt_w3, filt_b3, filt_freq, filt_w4, hyena_skip, w_fnet, b_fnet, attn_sink, w_out, ln1_g, ln1_b, w_gate, w_up, w_down, ln2_g, ln2_b):
    bp, lp, _ = x_prompt.shape
    bs, ls, _ = x_sample.shape
    tp, ts = bp * lp, bs * ls
    batches = ((0, bp, lp), (tp, bs, ls))
    tm = _pick_tile(math.gcd(tp, ts), 1024)
    tln = _pick_tile(math.gcd(tp, ts), 256)

    conv_tabs = {seq: _conv_tables(seq) for seq in {lp, ls}}
    fnet_tabs = {seq: _fnet_tables(seq) for seq in {lp, ls}}

    resid, xb = _ln0(x_prompt.reshape(tp, D_MODEL), x_sample.reshape(ts, D_MODEL), ln0_g, ln0_b, tln)
    w_in_b, w_out_b, w_down_b = w_in.astype(BF16), w_out.astype(BF16), w_down.astype(BF16)
    for l in range(DEPTH):
        fw = (filt_w1[l], filt_b1[l], filt_w2[l], filt_b2[l], filt_w3[l], filt_b3[l], filt_freq[l], filt_w4[l])

        proj = _matmul([xb], w_in_b, None, F32, tm, 1024, "in_proj", layer=l)

        kf = {seq: _filter_spectrum(seq, conv_tabs[seq], fw) for seq in {lp, ls}}
        y_h = y_f = None
        for r0, nb, seq in batches:
            y_h = _hyena_batch(proj, r0, nb, seq, conv_tabs[seq], kf[seq], short_w[l], short_b[l], hyena_skip[l],
                               tp + ts, y_h)
            y_f = _fnet_batch(proj, r0, nb, seq, fnet_tabs[seq], tp + ts, y_f)
        y_f = _matmul([y_f], w_fnet, b_fnet[l], BF16, tm, 1024, "fnet_linear", layer=l)
        y_a = _attention(proj, attn_sink[l], tp, lp, ls)

        y = _matmul([y_h, y_f, y_a], w_out_b, None, F32, tm, 512, "out_proj", resid=resid, layer=l)
        xb, mu, rs = _ln(y, ln1_g[l], ln1_b[l], tln)
        resid = (y, mu, rs, ln1_g[l], ln1_b[l])

        hid = _gate_up(xb, w_gate, w_up, l, _pick_tile(math.gcd(tp, ts), 2048), FF_TILE)
        y = _matmul([hid], w_down_b, None, F32, _pick_tile(tm, 512), 512, "ffn_down", resid=resid, layer=l)
        if l + 1 < DEPTH:
            xb, mu, rs = _ln(y, ln2_g[l], ln2_b[l], tln)
            resid = (y, mu, rs, ln2_g[l], ln2_b[l])
        else:
            y_p, y_s = _ln_final(y, ln2_g[l], ln2_b[l], tln, tp)
    return y_p.reshape(bp, lp, D_MODEL), y_s.reshape(bs, ls, D_MODEL)
```

```python
import functools
import math

import jax
import jax.numpy as jnp
from jax import lax
from jax.experimental import pallas as pl
from jax.experimental.pallas import tpu as pltpu

F32 = jnp.float32
BF16 = jnp.bfloat16

D_MODEL = 4096
HEAD_DIM = 128
HYENA_WIDTH = 1024
FNET_WIDTH = 1024
ATTN_WIDTH = 2048
FNET_HEADS = 8
N_HEADS = 16
N_KV_HEADS = 4
KV_GROUP = 4
KV_WIDTH = 512
BLOCK = 128
HYENA_IN = 3 * HYENA_WIDTH
FILTER_EMB = 33
FILTER_HIDDEN = 64
FILTER_OUT = 4 * HYENA_WIDTH
OFF_FNET = HYENA_IN
OFF_Q = OFF_FNET + FNET_WIDTH
OFF_K = OFF_Q + ATTN_WIDTH
OFF_V = OFF_K + KV_WIDTH
IN_WIDTH = OFF_V + KV_WIDTH
D_FF = 11008
FF_TILE = 256
DEPTH = 2
ALPHA = (2 * DEPTH) ** 0.25
LN_EPS = 1e-5
DECAY_FAST_PCT = 0.3
DECAY_SLOW_PCT = 1.5
DECAY_TARGET = 1e-2

DFT_N2 = 128
LANES = 128
T2_GROUP = 8
PITCH = 136
CONV_PASSES = 1
SPEC_PASSES = 1
INNER_GROUP = 8
ATTN_TQ = 512
VMEM_LIMIT = 56 * 1024 * 1024


def _cparams(sem, vmem=VMEM_LIMIT):
    return pltpu.CompilerParams(dimension_semantics=sem, vmem_limit_bytes=vmem)


def _dot(a, b):
    return jnp.dot(a, b, preferred_element_type=F32)


def _split(x):
    hi = x.astype(BF16)
    lo = (x - hi.astype(F32)).astype(BF16)
    return hi, lo


def _dot3_left(m_hi, m_lo, x):
    x_hi, x_lo = _split(x)
    return _dot(m_hi, x_hi) + (_dot(m_hi, x_lo) + _dot(m_lo, x_hi))


def _split_const(m):
    m = m.astype(F32)
    hi = m.astype(BF16)
    lo = (m - hi.astype(F32)).astype(BF16)
    return hi, lo


def _cs(num, den):
    ang = (2.0 * math.pi / den) * (num % den).astype(F32)
    return jnp.cos(ang), jnp.sin(ang)


def _full_spec(a):
    return pl.BlockSpec(a.shape, lambda *_: (0,) * a.ndim)


def _ln_math(x, g, b):
    mu = jnp.mean(x, axis=-1, keepdims=True)
    xc = x - mu
    var = jnp.mean(xc * xc, axis=-1, keepdims=True)
    return xc * lax.rsqrt(var + LN_EPS) * g + b


def _ln0_kernel(xp_ref, xs_ref, g_ref, b_ref, of_ref, ob_ref, *, n_p):
    i = pl.program_id(0)

    def emit(x):
        y = _ln_math(x, g_ref[...], b_ref[...])
        of_ref[...] = y
        ob_ref[...] = y.astype(BF16)

    @pl.when(i < n_p)
    def _():
        emit(xp_ref[...])

    @pl.when(i >= n_p)
    def _():
        emit(xs_ref[...])


def _ln0(xp, xs, g, b, tm):
    tp, ts = xp.shape[0], xs.shape[0]
    n_p, n_s = tp // tm, ts // tm
    t = tp + ts
    return pl.pallas_call(
        functools.partial(_ln0_kernel, n_p=n_p),
        grid=(n_p + n_s,),
        in_specs=[
            pl.BlockSpec((tm, D_MODEL), lambda i: (jnp.minimum(i, n_p - 1), 0)),
            pl.BlockSpec((tm, D_MODEL), lambda i: (jnp.maximum(i - n_p, 0), 0)),
            pl.BlockSpec((1, D_MODEL), lambda i: (0, 0)),
            pl.BlockSpec((1, D_MODEL), lambda i: (0, 0)),
        ],
        out_specs=[
            pl.BlockSpec((tm, D_MODEL), lambda i: (i, 0)),
            pl.BlockSpec((tm, D_MODEL), lambda i: (i, 0)),
        ],
        out_shape=[jax.ShapeDtypeStruct((t, D_MODEL), F32), jax.ShapeDtypeStruct((t, D_MODEL), BF16)],
        compiler_params=_cparams(("parallel",)),
        name="ln0",
    )(xp, xs, g.reshape(1, -1), b.reshape(1, -1))


def _ln_kernel(y_ref, g_ref, b_ref, ob_ref, mu_ref, rs_ref):
    y = y_ref[...]
    mu = jnp.mean(y, axis=-1, keepdims=True)
    xc = y - mu
    rs = lax.rsqrt(jnp.mean(xc * xc, axis=-1, keepdims=True) + LN_EPS)
    ob_ref[...] = (xc * rs * g_ref[...] + b_ref[...]).astype(BF16)
    mu_ref[...] = jnp.broadcast_to(mu, mu_ref.shape)
    rs_ref[...] = jnp.broadcast_to(rs, rs_ref.shape)


def _ln(y, g, b, tm):
    t = y.shape[0]
    row = pl.BlockSpec((tm, D_MODEL), lambda i: (i, 0))
    vec = pl.BlockSpec((1, D_MODEL), lambda i: (0, 0))
    stat = pl.BlockSpec((tm, LANES), lambda i: (i, 0))
    return pl.pallas_call(
        _ln_kernel,
        grid=(t // tm,),
        in_specs=[row, vec, vec],
        out_specs=[row, stat, stat],
        out_shape=[jax.ShapeDtypeStruct((t, D_MODEL), BF16), jax.ShapeDtypeStruct((t, LANES), F32),
                   jax.ShapeDtypeStruct((t, LANES), F32)],
        compiler_params=_cparams(("parallel",)),
        name="layer_norm",
    )(y, g.reshape(1, -1), b.reshape(1, -1))


def _ln_final_kernel(y_ref, g_ref, b_ref, op_ref, os_ref, *, n_p):
    i = pl.program_id(0)
    y = _ln_math(y_ref[...], g_ref[...], b_ref[...])

    @pl.when(i < n_p)
    def _():
        op_ref[...] = y

    @pl.when(i >= n_p)
    def _():
        os_ref[...] = y


def _ln_final(y, g, b, tm, tp):
    t = y.shape[0]
    n_p = tp // tm
    row = pl.BlockSpec((tm, D_MODEL), lambda i: (i, 0))
    vec = pl.BlockSpec((1, D_MODEL), lambda i: (0, 0))
    return pl.pallas_call(
        functools.partial(_ln_final_kernel, n_p=n_p),
        grid=(t // tm,),
        in_specs=[row, vec, vec],
        out_specs=[
            pl.BlockSpec((tm, D_MODEL), lambda i: (jnp.minimum(i, n_p - 1), 0)),
            pl.BlockSpec((tm, D_MODEL), lambda i: (jnp.maximum(i - n_p, 0), 0)),
        ],
        out_shape=[jax.ShapeDtypeStruct((tp, D_MODEL), F32), jax.ShapeDtypeStruct((t - tp, D_MODEL), F32)],
        compiler_params=_cparams(("arbitrary",)),
        name="layer_norm_final",
    )(y, g.reshape(1, -1), b.reshape(1, -1))


def _mm_kernel(*refs, widths, has_bias, resid_mode):
    n_a = len(widths)
    a_refs = refs[:n_a]
    w_ref = refs[n_a]
    o_ref = refs[-1]
    nxt = n_a + 1
    acc = None
    off = 0
    for a_ref, wd in zip(a_refs, widths):
        part = _dot(a_ref[...].astype(BF16), w_ref[off:off + wd, :].astype(BF16))
        acc = part if acc is None else acc + part
        off += wd
    if has_bias:
        acc = acc + refs[nxt][...]
        nxt += 1
    if resid_mode == "plain":
        acc = ALPHA * refs[nxt][...] + acc
    elif resid_mode == "normalise":
        y_ref, mu_ref, rs_ref, g_ref, b_ref = refs[nxt:nxt + 5]
        reps = o_ref.shape[1] // LANES
        mu = jnp.concatenate([mu_ref[...]] * reps, axis=1)
        rs = jnp.concatenate([rs_ref[...]] * reps, axis=1)
        acc = ALPHA * ((y_ref[...] - mu) * rs * g_ref[...] + b_ref[...]) + acc
    o_ref[...] = acc.astype(o_ref.dtype)


def _matmul(a_list, w, bias, out_dtype, tm, tn, name, resid=None, layer=None):
    t = a_list[0].shape[0]
    k, n = w.shape[-2:]
    widths = tuple(a.shape[1] for a in a_list)
    assert sum(widths) == k and t % tm == 0 and n % tn == 0
    in_specs = [pl.BlockSpec((tm, wd), lambda i, j: (i, 0)) for wd in widths]
    if layer is None:
        in_specs.append(pl.BlockSpec((k, tn), lambda i, j: (0, j)))
    else:
        in_specs.append(pl.BlockSpec((None, k, tn), lambda i, j: (layer, 0, j)))
    args = list(a_list) + [w]
    if bias is not None:
        in_specs.append(pl.BlockSpec((1, tn), lambda i, j: (0, j)))
        args.append(bias.reshape(1, n).astype(F32))
    resid_mode = None
    if isinstance(resid, tuple):
        resid_mode = "normalise"
        y, mu, rs, g, b = resid
        stat = pl.BlockSpec((tm, LANES), lambda i, j: (i, 0))
        vec = pl.BlockSpec((1, tn), lambda i, j: (0, j))
        in_specs += [pl.BlockSpec((tm, tn), lambda i, j: (i, j)), stat, stat, vec, vec]
        args += [y, mu, rs, g.reshape(1, n).astype(F32), b.reshape(1, n).astype(F32)]
    elif resid is not None:
        resid_mode = "plain"
        in_specs.append(pl.BlockSpec((tm, tn), lambda i, j: (i, j)))
        args.append(resid)
    return pl.pallas_call(
        functools.partial(_mm_kernel, widths=widths, has_bias=bias is not None, resid_mode=resid_mode),
        grid=(t // tm, n // tn),
        in_specs=in_specs,
        out_specs=pl.BlockSpec((tm, tn), lambda i, j: (i, j)),
        out_shape=jax.ShapeDtypeStruct((t, n), out_dtype),
        compiler_params=_cparams(("parallel", "parallel")),
        name=name,
    )(*args)


def _gate_up_kernel(x_ref, wg_ref, wu_ref, o_ref):
    wg = wg_ref[...].astype(BF16)
    wu = wu_ref[...].astype(BF16)
    rows = x_ref.shape[0] // 2
    for r0 in (0, rows):
        x = x_ref[r0:r0 + rows, :]
        g = _dot(x, wg)
        u = _dot(x, wu)
        o_ref[r0:r0 + rows, :] = (g * (1.0 / (1.0 + jnp.exp(-g))) * u).astype(o_ref.dtype)


def _gate_up(x, wg, wu, layer, tm, tn):
    t, k = x.shape
    n = wg.shape[-1]
    return pl.pallas_call(
        _gate_up_kernel,
        grid=(t // tm, n // tn),
        in_specs=[
            pl.BlockSpec((tm, k), lambda i, j: (i, 0)),
            pl.BlockSpec((None, k, tn), lambda i, j: (layer, 0, j)),
            pl.BlockSpec((None, k, tn), lambda i, j: (layer, 0, j)),
        ],
        out_specs=pl.BlockSpec((tm, tn), lambda i, j: (i, j)),
        out_shape=jax.ShapeDtypeStruct((t, n), BF16),
        compiler_params=_cparams(("parallel", "parallel")),
        name="ffn_gate_up",
    )(x, wg, wu)


def _attn_kernel(q_ref, kp_ref, kc_ref, kn_ref, vp_ref, vc_ref, vn_ref, bias_ref, sink_ref, o_ref, kbuf, vbuf,
                 *, tq, tiles_p, per_p, per_s):
    g = pl.program_id(0)
    h = pl.program_id(1)
    in_p = g < tiles_p
    n_loc = jnp.where(in_p, g % per_p, (g - tiles_p) % per_s)
    n_seq = jnp.where(in_p, per_p, per_s)
    pen_prev = jnp.where(n_loc > 0, 0.0, -jnp.inf)
    pen_next = jnp.where(n_loc < n_seq - 1, 0.0, -jnp.inf)

    kbuf[0:BLOCK, :] = kp_ref[...].astype(BF16)
    kbuf[BLOCK:BLOCK + tq, :] = kc_ref[...].astype(BF16)
    kbuf[BLOCK + tq:, :] = kn_ref[...].astype(BF16)
    vbuf[:, 0:BLOCK] = vp_ref[...].T.astype(BF16)
    for c0 in range(0, tq, BLOCK):
        vbuf[:, BLOCK + c0:2 * BLOCK + c0] = vc_ref[c0:c0 + BLOCK, :].T.astype(BF16)
    vbuf[:, BLOCK + tq:] = vn_ref[...].T.astype(BF16)

    key = lax.broadcasted_iota(jnp.int32, (3 * BLOCK, BLOCK), 0)
    col_prev = jnp.where(key < BLOCK, pen_prev, 0.0)
    col_next = jnp.where(key >= 2 * BLOCK, pen_next, 0.0)
    scale = HEAD_DIM ** -0.5
    dn = (((1,), (1,)), ((), ()))
    nsb = tq // BLOCK
    for sb in range(nsb):
        r0 = sb * BLOCK
        k3 = kbuf[r0:r0 + 3 * BLOCK, :]
        v3 = vbuf[:, r0:r0 + 3 * BLOCK]
        for gi in range(KV_GROUP):
            sink = sink_ref[h * KV_GROUP + gi]
            q = q_ref[r0:r0 + BLOCK, gi * HEAD_DIM:(gi + 1) * HEAD_DIM].astype(BF16)
            s = lax.dot_general(k3, q, dn, preferred_element_type=F32) * scale
            s = s + bias_ref[gi * 3 * BLOCK:(gi + 1) * 3 * BLOCK, :]
            if sb == 0:
                s = s + col_prev
            if sb == nsb - 1:
                s = s + col_next
            m = jnp.maximum(jnp.max(s, axis=0, keepdims=True), sink)
            p = jnp.exp(s - m)
            denom = jnp.sum(p, axis=0, keepdims=True) + jnp.exp(sink - m)
            o_t = _dot(v3, p.astype(BF16)) * (1.0 / denom)
            o_ref[r0:r0 + BLOCK, gi * HEAD_DIM:(gi + 1) * HEAD_DIM] = o_t.T.astype(o_ref.dtype)


def _attention(proj, sink, tp, lp, ls):
    t = proj.shape[0]
    tq = math.gcd(ATTN_TQ, math.gcd(lp, ls))
    nblk = t // BLOCK
    bpt = tq // BLOCK
    slopes = 2.0 ** (-8.0 * jnp.arange(1, N_HEADS + 1, dtype=F32) / N_HEADS)
    qi = jnp.arange(BLOCK)[:, None]
    ki = jnp.arange(3 * BLOCK)[None, :]
    dist = jnp.abs(qi + BLOCK - ki)
    bias = jnp.where(dist[None] <= BLOCK, -slopes[:, None, None] * dist[None].astype(F32), -jnp.inf)
    bias = jnp.swapaxes(bias, 1, 2).reshape(N_HEADS * 3 * BLOCK, BLOCK)
    qc = OFF_Q // (KV_GROUP * HEAD_DIM)
    kc = OFF_K // HEAD_DIM
    vc = OFF_V // HEAD_DIM
    prev = lambda g: jnp.maximum(g * bpt - 1, 0)
    nxt = lambda g: jnp.minimum((g + 1) * bpt, nblk - 1)
    halo = (BLOCK, HEAD_DIM)
    cur = (tq, HEAD_DIM)
    return pl.pallas_call(
        functools.partial(_attn_kernel, tq=tq, tiles_p=tp // tq, per_p=lp // tq, per_s=ls // tq),
        grid=(t // tq, N_KV_HEADS),
        in_specs=[
            pl.BlockSpec((tq, KV_GROUP * HEAD_DIM), lambda g, h: (g, qc + h)),
            pl.BlockSpec(halo, lambda g, h: (prev(g), kc + h)),
            pl.BlockSpec(cur, lambda g, h: (g, kc + h)),
            pl.BlockSpec(halo, lambda g, h: (nxt(g), kc + h)),
            pl.BlockSpec(halo, lambda g, h: (prev(g), vc + h)),
            pl.BlockSpec(cur, lambda g, h: (g, vc + h)),
            pl.BlockSpec(halo, lambda g, h: (nxt(g), vc + h)),
            pl.BlockSpec((KV_GROUP * 3 * BLOCK, BLOCK), lambda g, h: (h, 0)),
            pl.BlockSpec(memory_space=pltpu.SMEM),
        ],
        out_specs=pl.BlockSpec((tq, KV_GROUP * HEAD_DIM), lambda g, h: (g, h)),
        out_shape=jax.ShapeDtypeStruct((t, ATTN_WIDTH), BF16),
        scratch_shapes=[pltpu.VMEM((tq + 2 * BLOCK, HEAD_DIM), BF16), pltpu.VMEM((HEAD_DIM, tq + 2 * BLOCK), BF16)],
        compiler_params=_cparams(("parallel", "parallel")),
        name="band_attention",
    )(proj, proj, proj, proj, proj, proj, proj, bias, sink.astype(F32))


def _filter_kernel(fvec_ref, w1_ref, b1_ref, w2_ref, b2_ref, w3_ref, b3_ref, fr_ref, w4_ref, delta_ref, o_ref,
                   *, seq, tl):
    r0 = pl.program_id(0) * tl
    row = (lax.broadcasted_iota(jnp.int32, (tl, 128), 0) + r0).astype(F32)
    lane = lax.broadcasted_iota(jnp.int32, (tl, 128), 1)
    t = row * (1.0 / (seq - 1))
    ang = (row * (2.0 * math.pi / seq)) * fvec_ref[...]
    z = jnp.where(lane == 0, t,
                  jnp.where(lane <= 16, jnp.cos(ang), jnp.where(lane <= 32, -jnp.sin(ang), 0.0)))
    def dot3(x, w_ref, cols=slice(None)):
        x_hi, x_lo = _split(x)
        w_hi, w_lo = w_ref[0, :, cols], w_ref[1, :, cols]
        return _dot(x_hi, w_hi) + (_dot(x_lo, w_hi) + _dot(x_hi, w_lo))

    h = jnp.sin(fr_ref[0:1, :] * (dot3(z, w1_ref) + b1_ref[...]))
    h = jnp.sin(fr_ref[1:2, :] * (dot3(h, w2_ref) + b2_ref[...]))
    h = jnp.sin(fr_ref[2:3, :] * (dot3(h, w3_ref) + b3_ref[...]))
    trow = (lax.broadcasted_iota(jnp.int32, (tl, HYENA_WIDTH), 0) + r0)
    decay = jnp.exp(-(trow.astype(F32) * (1.0 / (seq - 1))) * delta_ref[...])
    first = trow == 0
    for part in range(4):
        sl = slice(part * HYENA_WIDTH, (part + 1) * HYENA_WIDTH)
        v = dot3(h, w4_ref, sl) * decay
        if part >= 2:
            v = jnp.where(first, 0.0, v)
        o_ref[:, sl] = v


def _hyena_filters(seq, w1, b1, w2, b2, w3, b3, freq, w4):
    tl = min(seq, 512)
    bands = (FILTER_EMB - 1) // 2
    f = jnp.linspace(1e-4, bands - 1, bands, dtype=F32)
    fvec = jnp.zeros((1, 128), F32).at[0, 1:1 + bands].set(f).at[0, 1 + bands:1 + 2 * bands].set(f)
    w1p = jnp.zeros((128, FILTER_HIDDEN), F32).at[:FILTER_EMB].set(w1.astype(F32))
    max_decay = math.log(DECAY_TARGET) / DECAY_FAST_PCT
    min_decay = math.log(DECAY_TARGET) / DECAY_SLOW_PCT
    deltas = jnp.abs(jnp.linspace(min_decay, max_decay, HYENA_WIDTH, dtype=F32)).reshape(1, -1)
    planes = lambda w: jnp.stack(_split_const(w))
    args = [fvec, planes(w1p), b1.reshape(1, -1).astype(F32), planes(w2), b2.reshape(1, -1).astype(F32),
            planes(w3), b3.reshape(1, -1).astype(F32), freq.astype(F32), planes(w4), deltas]
    return pl.pallas_call(
        functools.partial(_filter_kernel, seq=seq, tl=tl),
        grid=(seq // tl,),
        in_specs=[_full_spec(a) for a in args],
        out_specs=pl.BlockSpec((tl, FILTER_OUT), lambda i: (i, 0)),
        out_shape=jax.ShapeDtypeStruct((seq, FILTER_OUT), F32),
        compiler_params=_cparams(("parallel",)),
        name="hyena_filter",
    )(*args)


def _dotp(mh_ref, ml_ref, x, passes):
    if passes == 1:
        return _dot(mh_ref[...], x.astype(BF16))
    return _dot3_left(mh_ref[...], ml_ref[...], x)


def _gather(ref, t2, n):
    return ref[pl.ds(t2, n, stride=PITCH), :]


def _fill_pitched(u_ref, p_ref, half, w_ref=None, b_ref=None):
    n2 = DFT_N2
    seq = half * n2
    rid = lax.broadcasted_iota(jnp.int32, (n2, LANES), 0)

    def step(t1, carry):
        r0 = pl.multiple_of(t1 * n2, n2)
        cur = u_ref[pl.ds(r0, n2), :]
        if w_ref is not None:
            before = u_ref[pl.ds(pl.multiple_of(jnp.maximum(r0 - 8, 0), 8), 8), :][7:8, :]
            after = u_ref[pl.ds(pl.multiple_of(jnp.minimum(r0 + n2, seq - 8), 8), 8), :][0:1, :]
            before = jnp.where(t1 > 0, before, 0.0)
            after = jnp.where(t1 < half - 1, after, 0.0)
            prev = jnp.where(rid == 0, before, pltpu.roll(cur, 1, 0))
            nxt = jnp.where(rid == n2 - 1, after, pltpu.roll(cur, n2 - 1, 0))
            cur = prev * w_ref[0:1, :] + cur * w_ref[1:2, :] + nxt * w_ref[2:3, :] + b_ref[...]
        p_ref[pl.ds(pl.multiple_of(t1 * PITCH, 8), n2), :] = cur
        return carry

    lax.fori_loop(0, half, step, 0)


def _for_t2_groups(body):
    def step(g, carry):
        body(g * T2_GROUP)
        return carry

    lax.fori_loop(0, DFT_N2 // T2_GROUP, step, 0, unroll=2)


def _outer_stage(p_ref, half, mh_ref, ml_ref, a_ref, nrows, passes):
    def body(base):
        xs = jnp.concatenate([_gather(p_ref, base + j, half) for j in range(T2_GROUP)], axis=1)
        y = _dotp(mh_ref, ml_ref, xs, passes)
        for j in range(T2_GROUP):
            a_ref[pl.ds(base + j, nrows, stride=PITCH), :] = y[:, j * LANES:(j + 1) * LANES]

    _for_t2_groups(body)


def _lane_block(x, j):
    return x[:, j * LANES:(j + 1) * LANES]


def _inner_blocks(a_ref, tw_ref, wstep_ref, nblocks, group_fn, store_fn):
    n2 = DFT_N2
    tw_ref[0:n2, :] = jnp.ones((n2, LANES), F32)
    tw_ref[n2:, :] = jnp.zeros((n2, LANES), F32)

    def run(k0, count):
        twr = tw_ref[0:n2, :]
        twi = tw_ref[n2:, :]
        wr = wstep_ref[0:n2, :]
        wi = wstep_ref[n2:, :]
        ks, rows, tws, blocks = [], [], [], []
        for j in range(count):
            k = k0 + j
            rr = pl.multiple_of(k * 2 * PITCH, 8)
            ri = pl.multiple_of(k * 2 * PITCH + PITCH, 8)
            ar = a_ref[pl.ds(rr, n2), :]
            ai = a_ref[pl.ds(ri, n2), :]
            ks.append(k)
            rows.append((rr, ri))
            tws.append((twr, twi))
            blocks.append(jnp.concatenate([ar * twr - ai * twi, ar * twi + ai * twr], axis=0))
            twr, twi = twr * wr - twi * wi, twr * wi + twi * wr
        vals = group_fn(ks, jnp.concatenate(blocks, axis=1), tws)
        for k, (rr, ri), val in zip(ks, rows, vals):
            store_fn(k, rr, ri, val)
        tw_ref[0:n2, :] = twr
        tw_ref[n2:, :] = twi

    def step(g, carry):
        run(g * INNER_GROUP, INNER_GROUP)
        return carry

    lax.fori_loop(0, nblocks // INNER_GROUP, step, 0)
    if nblocks % INNER_GROUP:
        run(nblocks - nblocks % INNER_GROUP, nblocks % INNER_GROUP)


def _hyena_conv_kernel(sig_ref, gate_ref, wsig_ref, bsig_ref, wgate_ref, bgate_ref, skip_ref, kf_ref,
                       f1h_ref, f1l_ref, g3h_ref, g3l_ref, gch_ref, gcl_ref, gbh_ref, gbl_ref, wstep_ref,
                       o_ref, a_ref, tw_ref, pv_ref, pg_ref, *, half, k1, k1p, sig_conv):
    n2 = DFT_N2
    if sig_conv:
        _fill_pitched(sig_ref, pv_ref, half, wsig_ref, bsig_ref)
    else:
        _fill_pitched(sig_ref, pv_ref, half)
    _fill_pitched(gate_ref, pg_ref, half, wgate_ref, bgate_ref)
    _outer_stage(pv_ref, half, f1h_ref, f1l_ref, a_ref, 2 * k1p, CONV_PASSES)

    def block(ks, xcat, tws):
        x = _dotp(gch_ref, gcl_ref, xcat, CONV_PASSES)
        prods = []
        for j, k in enumerate(ks):
            xr, xi = _lane_block(x[:n2], j), _lane_block(x[n2:], j)
            k0 = pl.multiple_of(k * 2 * n2, 2 * n2)
            kr = kf_ref[pl.ds(k0, n2), :].astype(F32)
            ki = kf_ref[pl.ds(k0 + n2, n2), :].astype(F32)
            prods.append(jnp.concatenate([xr * kr - xi * ki, xr * ki + xi * kr], axis=0))
        bb = _dotp(gbh_ref, gbl_ref, jnp.concatenate(prods, axis=1), CONV_PASSES)
        out = []
        for j, (twr, twi) in enumerate(tws):
            br, bi = _lane_block(bb[:n2], j), _lane_block(bb[n2:], j)
            out.append((br * twr + bi * twi, bi * twr - br * twi))
        return out

    def put(k, rr, ri, vals):
        a_ref[pl.ds(rr, n2), :] = vals[0]
        a_ref[pl.ds(ri, n2), :] = vals[1]

    _inner_blocks(a_ref, tw_ref, wstep_ref, k1, block, put)

    def finish(base):
        bcat = jnp.concatenate([_gather(a_ref, base + j, 2 * k1p) for j in range(T2_GROUP)], axis=1)
        y = _dotp(g3h_ref, g3l_ref, bcat, CONV_PASSES)
        for j in range(T2_GROUP):
            v = _gather(pv_ref, base + j, half)
            gate = _gather(pg_ref, base + j, half)
            pg_ref[pl.ds(base + j, half, stride=PITCH), :] = gate * (y[:, j * LANES:(j + 1) * LANES]
                                                                    + v * skip_ref[...])

    _for_t2_groups(finish)

    def emit(t1, carry):
        o_ref[pl.ds(pl.multiple_of(t1 * n2, n2), n2), :] = pg_ref[pl.ds(pl.multiple_of(t1 * PITCH, 8), n2), :]
        return carry

    lax.fori_loop(0, half, emit, 0)


def _hyena_spec_kernel(ff_ref, fb_ref, f2h_ref, f2l_ref, gch_ref, gcl_ref, wstep_ref, rev_ref, o_ref, a_ref, tw_ref,
                       pf_ref, *, half, k1, k1p):
    n2 = DFT_N2
    seq = half * n2
    _fill_pitched(ff_ref, pf_ref, half)
    rid = lax.broadcasted_iota(jnp.int32, (n2, LANES), 0)

    def reversed_slab(a, carry):
        s = half - 1 - a
        src = fb_ref[pl.ds(pl.multiple_of(s * n2, n2), n2), :]
        if SPEC_PASSES == 1:
            flipped = _dot(rev_ref[...], src.astype(BF16))
        else:
            src_hi, src_lo = _split(src)
            flipped = _dot(rev_ref[...], src_hi) + _dot(rev_ref[...], src_lo)
        head = fb_ref[pl.ds(pl.multiple_of(jnp.minimum((s + 1) * n2, seq - 8), 8), 8), :][0:1, :]
        head = jnp.where(a > 0, head, 0.0)
        pf_ref[pl.ds(pl.multiple_of((half + a) * PITCH, 8), n2), :] = jnp.where(rid == 0, head, flipped)
        return carry

    lax.fori_loop(0, half, reversed_slab, 0, unroll=8)
    _outer_stage(pf_ref, 2 * half, f2h_ref, f2l_ref, a_ref, 2 * k1p, SPEC_PASSES)

    def spectrum(ks, xcat, tws):
        x = _dotp(gch_ref, gcl_ref, xcat, SPEC_PASSES)
        return [_lane_block(x, j) for j in range(len(ks))]

    def put(k, rr, ri, x):
        o_ref[pl.ds(pl.multiple_of(k * 2 * n2, 2 * n2), 2 * n2), :] = x.astype(o_ref.dtype)

    _inner_blocks(a_ref, tw_ref, wstep_ref, k1, spectrum, put)


def _conv_tables(seq):
    n = 2 * seq
    n2 = DFT_N2
    n1 = n // n2
    k1 = n1 // 2 + 1
    k1p = -(-k1 // 8) * 8
    half = n1 // 2
    kk = jnp.arange(k1p, dtype=jnp.int32)
    valid = (kk < k1)
    t1 = jnp.arange(half, dtype=jnp.int32)
    c, s = _cs(kk[:, None] * t1[None, :], n1)
    vm = valid[:, None].astype(F32)
    f1 = jnp.stack([c * vm, -s * vm], axis=1).reshape(2 * k1p, half)
    wgt = jnp.where((kk == 0) | (kk == n1 // 2), 1.0, 2.0) * valid.astype(F32) / n
    g3 = jnp.stack([c * vm * wgt[:, None], -s * vm * wgt[:, None]], axis=1).reshape(2 * k1p, half).T
    j = jnp.arange(n2, dtype=jnp.int32)
    cr, cs_ = _cs(j[:, None] * j[None, :], n2)
    gc = jnp.concatenate([jnp.concatenate([cr, cs_], axis=1), jnp.concatenate([-cs_, cr], axis=1)], axis=0)
    gb = jnp.concatenate([jnp.concatenate([cr, -cs_], axis=1), jnp.concatenate([cs_, cr], axis=1)], axis=0)
    wr, ws = _cs(j, n)
    wstep = jnp.concatenate([jnp.broadcast_to(wr[:, None], (n2, LANES)),
                             jnp.broadcast_to(-ws[:, None], (n2, LANES))], axis=0)
    tall = jnp.arange(n1, dtype=jnp.int32)
    c2, s2 = _cs(kk[:, None] * tall[None, :], n1)
    f2 = jnp.stack([c2 * vm, -s2 * vm], axis=1).reshape(2 * k1p, n1)
    rev = ((j[:, None] + j[None, :]) == n2).astype(BF16)
    tabs = dict(n1=n1, k1=k1, k1p=k1p, half=half, wstep=wstep, rev=rev)
    for name, m in (("f1", f1), ("f2", f2), ("g3", g3), ("gc", gc), ("gb", gb)):
        tabs[name + "h"], tabs[name + "l"] = _split_const(m)
    return tabs


def _filter_spectrum(seq, tabs, fw):
    filt = _hyena_filters(seq, *fw)
    half, k1, k1p = tabs["half"], tabs["k1"], tabs["k1p"]
    n2 = DFT_N2
    nct = HYENA_WIDTH // LANES
    consts = [tabs[n] for n in ("f2h", "f2l", "gch", "gcl", "wstep", "rev")]
    return pl.pallas_call(
        functools.partial(_hyena_spec_kernel, half=half, k1=k1, k1p=k1p),
        grid=(2, nct),
        in_specs=[pl.BlockSpec((seq, LANES), lambda o, j: (0, o * nct + j)),
                  pl.BlockSpec((seq, LANES), lambda o, j: (0, (2 + o) * nct + j))] + [_full_spec(a) for a in consts],
        out_specs=pl.BlockSpec((None, None, k1 * 2 * n2, LANES), lambda o, j: (o, j, 0, 0)),
        out_shape=jax.ShapeDtypeStruct((2, nct, k1 * 2 * n2, LANES), BF16),
        scratch_shapes=[pltpu.VMEM((k1p * 2 * PITCH, LANES), F32), pltpu.VMEM((2 * n2, LANES), F32),
                        pltpu.VMEM((2 * half * PITCH, LANES), F32)],
        compiler_params=_cparams(("parallel", "parallel")),
        name="hyena_filter_spectrum",
    )(filt, filt, *consts)


def _skip_first_ref(kernel_fn, *refs, **kwargs):
    return kernel_fn(*refs[1:], **kwargs)


def _merged_out(kernel_fn, args, in_specs, into, total_rows, width, dtype):
    out_shape = jax.ShapeDtypeStruct((total_rows, width), dtype)
    if into is None:
        return kernel_fn, args, in_specs, out_shape, {}
    return (functools.partial(_skip_first_ref, kernel_fn), [into] + args,
            [pl.BlockSpec(memory_space=pl.ANY)] + in_specs, out_shape, {0: 0})


def _hyena_conv(sig, sig_cols, gate_cols, proj, row0, nbatch, seq, tabs, kf, order, short_w, short_b, skip,
                total_rows=None, into=None):
    assert row0 % seq == 0
    b0 = row0 // seq
    half, k1, k1p = tabs["half"], tabs["k1"], tabs["k1p"]
    n2 = DFT_N2
    nct = HYENA_WIDTH // LANES
    sig_conv = sig is None
    if sig_conv:
        sig_arr = proj
        sig_spec = pl.BlockSpec((seq, LANES), lambda j, b: (b0 + b, sig_cols + j))
    else:
        sig_arr = sig
        sig_spec = pl.BlockSpec((seq, LANES), lambda j, b: (b, j))
    sw = short_w.astype(F32)
    sb = short_b.reshape(1, -1).astype(F32)
    consts = [tabs[n] for n in ("f1h", "f1l", "g3h", "g3l", "gch", "gcl", "gbh", "gbl", "wstep")]
    args = [sig_arr, proj, sw, sb, sw, sb, skip.reshape(1, -1).astype(F32), kf] + consts
    in_specs = [
        sig_spec,
        pl.BlockSpec((seq, LANES), lambda j, b: (b0 + b, gate_cols + j)),
        pl.BlockSpec((3, LANES), lambda j, b: (0, sig_cols + j)),
        pl.BlockSpec((1, LANES), lambda j, b: (0, sig_cols + j)),
        pl.BlockSpec((3, LANES), lambda j, b: (0, gate_cols + j)),
        pl.BlockSpec((1, LANES), lambda j, b: (0, gate_cols + j)),
        pl.BlockSpec((1, LANES), lambda j, b: (0, j)),
        pl.BlockSpec((None, None, k1 * 2 * n2, LANES), lambda j, b: (order, j, 0, 0)),
    ] + [_full_spec(a) for a in consts]
    kern = functools.partial(_hyena_conv_kernel, half=half, k1=k1, k1p=k1p, sig_conv=sig_conv)
    out_b0 = 0 if total_rows is None else b0
    kern, args, in_specs, out_shape, aliases = _merged_out(
        kern, args, in_specs, into, total_rows or nbatch * seq, HYENA_WIDTH, F32)
    return pl.pallas_call(
        kern,
        grid=(nct, nbatch),
        in_specs=in_specs,
        out_specs=pl.BlockSpec((seq, LANES), lambda j, b: (out_b0 + b, j)),
        out_shape=out_shape,
        input_output_aliases=aliases,
        scratch_shapes=[pltpu.VMEM((k1p * 2 * PITCH, LANES), F32), pltpu.VMEM((2 * n2, LANES), F32),
                        pltpu.VMEM((half * PITCH, LANES), F32), pltpu.VMEM((half * PITCH, LANES), F32)],
        compiler_params=_cparams(("parallel", "arbitrary")),
        name="hyena_conv",
    )(*args)


def _hyena_batch(proj, row0, nbatch, seq, tabs, kf, short_w, short_b, skip, total_rows, into):
    nct = HYENA_WIDTH // LANES
    z = _hyena_conv(None, 2 * nct, 0, proj, row0, nbatch, seq, tabs, kf, 0, short_w, short_b, skip[0])
    return _hyena_conv(z, 2 * nct, nct, proj, row0, nbatch, seq, tabs, kf, 1, short_w, short_b, skip[1],
                       total_rows=total_rows, into=into)


def _fnet_kernel(u_ref, chan_ref, m1_ref, gri_ref, wstep_ref, o_ref, zr_ref, zi_ref, a_ref, tw_ref, *, n1):
    n2 = DFT_N2
    slabs = 4

    def chan(g, carry):
        x = u_ref[pl.ds(pl.multiple_of(g * slabs * n2, slabs * n2), slabs * n2), :].astype(BF16)
        z = _dot(x, chan_ref[...])
        for i in range(slabs):
            r = pl.multiple_of((g * slabs + i) * PITCH, 8)
            zr_ref[pl.ds(r, n2), :] = z[i * n2:(i + 1) * n2, :LANES]
            zi_ref[pl.ds(r, n2), :] = z[i * n2:(i + 1) * n2, LANES:]
        return carry

    lax.fori_loop(0, n1 // slabs, chan, 0)

    def outer(base):
        xs = jnp.concatenate(
            [jnp.concatenate([_gather(zr_ref, base + j, n1), _gather(zi_ref, base + j, n1)], axis=0)
             for j in range(T2_GROUP)], axis=1)
        y = _dot(m1_ref[...], xs.astype(BF16))
        for j in range(T2_GROUP):
            a_ref[pl.ds(base + j, 2 * n1, stride=PITCH), :] = y[:, j * LANES:(j + 1) * LANES]

    _for_t2_groups(outer)

    def real_part(ks, xcat, tws):
        y = _dot(gri_ref[...], xcat.astype(BF16))
        return [_lane_block(y, j) for j in range(len(ks))]

    def put(k, rr, ri, y):
        zr_ref[pl.ds(pl.multiple_of(k * PITCH, 8), n2), :] = y

    _inner_blocks(a_ref, tw_ref, wstep_ref, n1, real_part, put)

    def emit(k2, carry):
        o_ref[pl.ds(pl.multiple_of(k2 * n1, n1), n1), :] = _gather(zr_ref, k2, n1).astype(o_ref.dtype)
        return carry

    lax.fori_loop(0, n2, emit, 0)


def _fnet_tables(seq):
    n2 = DFT_N2
    n1 = seq // n2
    j = jnp.arange(HEAD_DIM, dtype=jnp.int32)
    c, s = _cs(j[:, None] * j[None, :], HEAD_DIM)
    scale = (seq * HEAD_DIM) ** -0.5
    chan = jnp.concatenate([c * scale, -s * scale], axis=1).astype(BF16)
    kk = jnp.arange(n1, dtype=jnp.int32)
    c1, s1 = _cs(kk[:, None] * kk[None, :], n1)
    m1 = jnp.stack([jnp.concatenate([c1, s1], axis=1), jnp.concatenate([-s1, c1], axis=1)], axis=1)
    m1 = m1.reshape(2 * n1, 2 * n1).astype(BF16)
    t2 = jnp.arange(n2, dtype=jnp.int32)
    cr, cs_ = _cs(t2[:, None] * t2[None, :], n2)
    gri = jnp.concatenate([cr, cs_], axis=1).astype(BF16)
    wr, ws = _cs(t2, seq)
    wstep = jnp.concatenate([jnp.broadcast_to(wr[:, None], (n2, LANES)),
                             jnp.broadcast_to(-ws[:, None], (n2, LANES))], axis=0)
    return dict(n1=n1, chan=chan, m1=m1, gri=gri, wstep=wstep)


def _fnet_batch(proj, row0, nbatch, seq, tabs, total_rows, into):
    assert row0 % seq == 0
    b0 = row0 // seq
    n1 = tabs["n1"]
    consts = [tabs[n] for n in ("chan", "m1", "gri", "wstep")]
    in_specs = ([pl.BlockSpec((seq, LANES), lambda j, b: (b0 + b, OFF_FNET // LANES + j))]
                + [_full_spec(a) for a in consts])
    kern, args, in_specs, out_shape, aliases = _merged_out(
        functools.partial(_fnet_kernel, n1=n1), [proj] + consts, in_specs, into, total_rows, FNET_WIDTH, BF16)
    return pl.pallas_call(
        kern,
        grid=(FNET_HEADS, nbatch),
        in_specs=in_specs,
        out_specs=pl.BlockSpec((seq, LANES), lambda j, b: (b0 + b, j)),
        out_shape=out_shape,
        input_output_aliases=aliases,
        scratch_shapes=[pltpu.VMEM((n1 * PITCH, LANES), F32), pltpu.VMEM((n1 * PITCH, LANES), F32),
                        pltpu.VMEM((2 * n1 * PITCH, LANES), F32), pltpu.VMEM((2 * DFT_N2, LANES), F32)],
        compiler_params=_cparams(("parallel", "parallel")),
        name="fnet_mixer",
    )(*args)


def _pick_tile(t, pref):
    while t % pref:
        pref //= 2
    return pref


def kernel(x_prompt, x_sample, ln0_g, ln0_b, w_in, short_w, short_b, filt_w1, filt_b1, filt_w2, filt_b2, filt_w3, filt_b3, filt_freq, filt_w4, hyena_skip, w_fnet, b_fnet, attn_sink, w_out, ln1_g, ln1_b, w_gate, w_up, w_down, ln2_g, ln2_b):
    bp, lp, _ = x_prompt.shape
    bs, ls, _ = x_sample.shape
    tp, ts = bp * lp, bs * ls
    batches = ((0, bp, lp), (tp, bs, ls))
    tm = _pick_tile(math.gcd(tp, ts), 1024)
    tln = _pick_tile(math.gcd(tp, ts), 256)

    conv_tabs = {seq: _conv_tables(seq) for seq in {lp, ls}}
    fnet_tabs = {seq: _fnet_tables(seq) for seq in {lp, ls}}

    resid, xb = _ln0(x_prompt.reshape(tp, D_MODEL), x_sample.reshape(ts, D_MODEL), ln0_g, ln0_b, tln)
    w_in_b, w_out_b, w_down_b = w_in.astype(BF16), w_out.astype(BF16), w_down.astype(BF16)
    for l in range(DEPTH):
        fw = (filt_w1[l], filt_b1[l], filt_w2[l], filt_b2[l], filt_w3[l], filt_b3[l], filt_freq[l], filt_w4[l])

        proj = _matmul([xb], w_in_b, None, F32, tm, 1024, "in_proj", layer=l)

        kf = {seq: _filter_spectrum(seq, conv_tabs[seq], fw) for seq in {lp, ls}}
        y_h = y_f = None
        for r0, nb, seq in batches:
            y_h = _hyena_batch(proj, r0, nb, seq, conv_tabs[seq], kf[seq], short_w[l], short_b[l], hyena_skip[l],
                               tp + ts, y_h)
            y_f = _fnet_batch(proj, r0, nb, seq, fnet_tabs[seq], tp + ts, y_f)
        y_f = _matmul([y_f], w_fnet, b_fnet[l], BF16, tm, 1024, "fnet_linear", layer=l)
        y_a = _attention(proj, attn_sink[l], tp, lp, ls)

        y = _matmul([y_h, y_f, y_a], w_out_b, None, F32, tm, 512, "out_proj", resid=resid, layer=l)
        xb, mu, rs = _ln(y, ln1_g[l], ln1_b[l], tln)
        resid = (y, mu, rs, ln1_g[l], ln1_b[l])

        hid = _gate_up(xb, w_gate, w_up, l, _pick_tile(math.gcd(tp, ts), 2048), FF_TILE)
        y = _matmul([hid], w_down_b, None, F32, _pick_tile(tm, 512), 512, "ffn_down", resid=resid, layer=l)
        if l + 1 < DEPTH:
            xb, mu, rs = _ln(y, ln2_g[l], ln2_b[l], tln)
            resid = (y, mu, rs, ln2_g[l], ln2_b[l])
        else:
            y_p, y_s = _ln_final(y, ln2_g[l], ln2_b[l], tln, tp)
    return y_p.reshape(bp, lp, D_MODEL), y_s.reshape(bs, ls, D_MODEL)
```

```python
import functools
import math

import jax
import jax.numpy as jnp
from jax import lax
from jax.experimental import pallas as pl
from jax.experimental.pallas import tpu as pltpu

F32 = jnp.float32
BF16 = jnp.bfloat16

D_MODEL = 4096
HEAD_DIM = 128
HYENA_WIDTH = 1024
FNET_WIDTH = 1024
ATTN_WIDTH = 2048
FNET_HEADS = 8
N_HEADS = 16
N_KV_HEADS = 4
KV_GROUP = 4
KV_WIDTH = 512
BLOCK = 128
HYENA_IN = 3 * HYENA_WIDTH
FILTER_EMB = 33
FILTER_HIDDEN = 64
FILTER_OUT = 4 * HYENA_WIDTH
OFF_FNET = HYENA_IN
OFF_Q = OFF_FNET + FNET_WIDTH
OFF_K = OFF_Q + ATTN_WIDTH
OFF_V = OFF_K + KV_WIDTH
IN_WIDTH = OFF_V + KV_WIDTH
D_FF = 11008
FF_TILE = 256
GATE_UP_PARTS = 2
DEPTH = 2
ALPHA = (2 * DEPTH) ** 0.25
LN_EPS = 1e-5
LOG2E = math.log2(math.e)
DECAY_FAST_PCT = 0.3
DECAY_SLOW_PCT = 1.5
DECAY_TARGET = 1e-2

DFT_N2 = 128
LANES = 128
T2_GROUP = 8
PITCH = 136
CONV_PASSES = 1
SPEC_PASSES = 1
INNER_GROUP = 8
ATTN_TQ = 512
VMEM_LIMIT = 56 * 1024 * 1024


def _cparams(sem, vmem=VMEM_LIMIT):
    return pltpu.CompilerParams(dimension_semantics=sem, vmem_limit_bytes=vmem)


def _dot(a, b):
    return jnp.dot(a, b, preferred_element_type=F32)


def _split(x):
    hi = x.astype(BF16)
    lo = (x - hi.astype(F32)).astype(BF16)
    return hi, lo


def _dot3_left(m_hi, m_lo, x):
    x_hi, x_lo = _split(x)
    return _dot(m_hi, x_hi) + (_dot(m_hi, x_lo) + _dot(m_lo, x_hi))


def _split_const(m):
    m = m.astype(F32)
    hi = m.astype(BF16)
    lo = (m - hi.astype(F32)).astype(BF16)
    return hi, lo


def _cs(num, den):
    ang = (2.0 * math.pi / den) * (num % den).astype(F32)
    return jnp.cos(ang), jnp.sin(ang)


def _full_spec(a):
    return pl.BlockSpec(a.shape, lambda *_: (0,) * a.ndim)


def _ln_math(x, g, b):
    mu = jnp.mean(x, axis=-1, keepdims=True)
    xc = x - mu
    var = jnp.mean(xc * xc, axis=-1, keepdims=True)
    return xc * lax.rsqrt(var + LN_EPS) * g + b


def _ln0_kernel(xp_ref, xs_ref, g_ref, b_ref, of_ref, ob_ref, *, n_p):
    i = pl.program_id(0)

    def emit(x):
        y = _ln_math(x, g_ref[...], b_ref[...])
        of_ref[...] = y
        ob_ref[...] = y.astype(BF16)

    @pl.when(i < n_p)
    def _():
        emit(xp_ref[...])

    @pl.when(i >= n_p)
    def _():
        emit(xs_ref[...])


def _ln0(xp, xs, g, b, tm):
    tp, ts = xp.shape[0], xs.shape[0]
    n_p, n_s = tp // tm, ts // tm
    t = tp + ts
    return pl.pallas_call(
        functools.partial(_ln0_kernel, n_p=n_p),
        grid=(n_p + n_s,),
        in_specs=[
            pl.BlockSpec((tm, D_MODEL), lambda i: (jnp.minimum(i, n_p - 1), 0)),
            pl.BlockSpec((tm, D_MODEL), lambda i: (jnp.maximum(i - n_p, 0), 0)),
            pl.BlockSpec((1, D_MODEL), lambda i: (0, 0)),
            pl.BlockSpec((1, D_MODEL), lambda i: (0, 0)),
        ],
        out_specs=[
            pl.BlockSpec((tm, D_MODEL), lambda i: (i, 0)),
            pl.BlockSpec((tm, D_MODEL), lambda i: (i, 0)),
        ],
        out_shape=[jax.ShapeDtypeStruct((t, D_MODEL), F32), jax.ShapeDtypeStruct((t, D_MODEL), BF16)],
        compiler_params=_cparams(("parallel",)),
        name="ln0",
    )(xp, xs, g.reshape(1, -1), b.reshape(1, -1))


def _ln_kernel(y_ref, g_ref, b_ref, ob_ref, mu_ref, rs_ref):
    y = y_ref[...]
    mu = jnp.mean(y, axis=-1, keepdims=True)
    xc = y - mu
    rs = lax.rsqrt(jnp.mean(xc * xc, axis=-1, keepdims=True) + LN_EPS)
    ob_ref[...] = (xc * rs * g_ref[...] + b_ref[...]).astype(BF16)
    mu_ref[...] = jnp.broadcast_to(mu, mu_ref.shape)
    rs_ref[...] = jnp.broadcast_to(rs, rs_ref.shape)


def _ln(y, g, b, tm):
    t = y.shape[0]
    row = pl.BlockSpec((tm, D_MODEL), lambda i: (i, 0))
    vec = pl.BlockSpec((1, D_MODEL), lambda i: (0, 0))
    stat = pl.BlockSpec((tm, LANES), lambda i: (i, 0))
    return pl.pallas_call(
        _ln_kernel,
        grid=(t // tm,),
        in_specs=[row, vec, vec],
        out_specs=[row, stat, stat],
        out_shape=[jax.ShapeDtypeStruct((t, D_MODEL), BF16), jax.ShapeDtypeStruct((t, LANES), F32),
                   jax.ShapeDtypeStruct((t, LANES), F32)],
        compiler_params=_cparams(("parallel",)),
        name="layer_norm",
    )(y, g.reshape(1, -1), b.reshape(1, -1))


def _ln_final_kernel(y_ref, g_ref, b_ref, op_ref, os_ref, *, n_p):
    i = pl.program_id(0)
    y = _ln_math(y_ref[...], g_ref[...], b_ref[...])

    @pl.when(i < n_p)
    def _():
        op_ref[...] = y

    @pl.when(i >= n_p)
    def _():
        os_ref[...] = y


def _ln_final(y, g, b, tm, tp):
    t = y.shape[0]
    n_p = tp // tm
    row = pl.BlockSpec((tm, D_MODEL), lambda i: (i, 0))
    vec = pl.BlockSpec((1, D_MODEL), lambda i: (0, 0))
    return pl.pallas_call(
        functools.partial(_ln_final_kernel, n_p=n_p),
        grid=(t // tm,),
        in_specs=[row, vec, vec],
        out_specs=[
            pl.BlockSpec((tm, D_MODEL), lambda i: (jnp.minimum(i, n_p - 1), 0)),
            pl.BlockSpec((tm, D_MODEL), lambda i: (jnp.maximum(i - n_p, 0), 0)),
        ],
        out_shape=[jax.ShapeDtypeStruct((tp, D_MODEL), F32), jax.ShapeDtypeStruct((t - tp, D_MODEL), F32)],
        compiler_params=_cparams(("arbitrary",)),
        name="layer_norm_final",
    )(y, g.reshape(1, -1), b.reshape(1, -1))


def _mm_kernel(*refs, widths, has_bias, resid_mode):
    n_a = len(widths)
    a_refs = refs[:n_a]
    w_ref = refs[n_a]
    o_ref = refs[-1]
    nxt = n_a + 1
    acc = None
    off = 0
    for a_ref, wd in zip(a_refs, widths):
        part = _dot(a_ref[...].astype(BF16), w_ref[off:off + wd, :].astype(BF16))
        acc = part if acc is None else acc + part
        off += wd
    if has_bias:
        acc = acc + refs[nxt][...]
        nxt += 1
    if resid_mode == "plain":
        acc = ALPHA * refs[nxt][...] + acc
    elif resid_mode == "normalise":
        y_ref, mu_ref, rs_ref, g_ref, b_ref = refs[nxt:nxt + 5]
        reps = o_ref.shape[1] // LANES
        mu = jnp.concatenate([mu_ref[...]] * reps, axis=1)
        rs = jnp.concatenate([rs_ref[...]] * reps, axis=1)
        acc = ALPHA * ((y_ref[...] - mu) * rs * g_ref[...] + b_ref[...]) + acc
    o_ref[...] = acc.astype(o_ref.dtype)


def _matmul(a_list, w, bias, out_dtype, tm, tn, name, resid=None, layer=None):
    t = a_list[0].shape[0]
    k, n = w.shape[-2:]
    widths = tuple(a.shape[1] for a in a_list)
    assert sum(widths) == k and t % tm == 0 and n % tn == 0
    in_specs = [pl.BlockSpec((tm, wd), lambda i, j: (i, 0)) for wd in widths]
    if layer is None:
        in_specs.append(pl.BlockSpec((k, tn), lambda i, j: (0, j)))
    else:
        in_specs.append(pl.BlockSpec((None, k, tn), lambda i, j: (layer, 0, j)))
    args = list(a_list) + [w]
    if bias is not None:
        in_specs.append(pl.BlockSpec((1, tn), lambda i, j: (0, j)))
        args.append(bias.reshape(1, n).astype(F32))
    resid_mode = None
    if isinstance(resid, tuple):
        resid_mode = "normalise"
        y, mu, rs, g, b = resid
        stat = pl.BlockSpec((tm, LANES), lambda i, j: (i, 0))
        vec = pl.BlockSpec((1, tn), lambda i, j: (0, j))
        in_specs += [pl.BlockSpec((tm, tn), lambda i, j: (i, j)), stat, stat, vec, vec]
        args += [y, mu, rs, g.reshape(1, n).astype(F32), b.reshape(1, n).astype(F32)]
    elif resid is not None:
        resid_mode = "plain"
        in_specs.append(pl.BlockSpec((tm, tn), lambda i, j: (i, j)))
        args.append(resid)
    return pl.pallas_call(
        functools.partial(_mm_kernel, widths=widths, has_bias=bias is not None, resid_mode=resid_mode),
        grid=(t // tm, n // tn),
        in_specs=in_specs,
        out_specs=pl.BlockSpec((tm, tn), lambda i, j: (i, j)),
        out_shape=jax.ShapeDtypeStruct((t, n), out_dtype),
        compiler_params=_cparams(("parallel", "parallel")),
        name=name,
    )(*args)


def _gate_up_kernel(x_ref, wg_ref, wu_ref, o_ref):
    wg = wg_ref[...].astype(BF16)
    wu = wu_ref[...].astype(BF16)
    rows = x_ref.shape[0] // GATE_UP_PARTS
    for r0 in range(0, x_ref.shape[0], rows):
        x = x_ref[r0:r0 + rows, :]
        g = _dot(x, wg)
        u = _dot(x, wu)
        o_ref[r0:r0 + rows, :] = (g * (1.0 / (1.0 + jnp.exp(-g))) * u).astype(o_ref.dtype)


def _gate_up(x, wg, wu, layer, tm, tn):
    t, k = x.shape
    n = wg.shape[-1]
    return pl.pallas_call(
        _gate_up_kernel,
        grid=(t // tm, n // tn),
        in_specs=[
            pl.BlockSpec((tm, k), lambda i, j: (i, 0)),
            pl.BlockSpec((None, k, tn), lambda i, j: (layer, 0, j)),
            pl.BlockSpec((None, k, tn), lambda i, j: (layer, 0, j)),
        ],
        out_specs=pl.BlockSpec((tm, tn), lambda i, j: (i, j)),
        out_shape=jax.ShapeDtypeStruct((t, n), BF16),
        compiler_params=_cparams(("parallel", "parallel")),
        name="ffn_gate_up",
    )(x, wg, wu)


def _attn_kernel(q_ref, kp_ref, kc_ref, kn_ref, vp_ref, vc_ref, vn_ref, bias_ref, sink_ref, o_ref, kbuf, vbuf,
                 *, tq, tiles_p, per_p, per_s):
    g = pl.program_id(0)
    h = pl.program_id(1)
    in_p = g < tiles_p
    n_loc = jnp.where(in_p, g % per_p, (g - tiles_p) % per_s)
    n_seq = jnp.where(in_p, per_p, per_s)
    pen_prev = jnp.where(n_loc > 0, 0.0, -jnp.inf)
    pen_next = jnp.where(n_loc < n_seq - 1, 0.0, -jnp.inf)

    kbuf[0:BLOCK, :] = kp_ref[...].astype(BF16)
    kbuf[BLOCK:BLOCK + tq, :] = kc_ref[...].astype(BF16)
    kbuf[BLOCK + tq:, :] = kn_ref[...].astype(BF16)
    vbuf[:, 0:BLOCK] = vp_ref[...].T.astype(BF16)
    for c0 in range(0, tq, BLOCK):
        vbuf[:, BLOCK + c0:2 * BLOCK + c0] = vc_ref[c0:c0 + BLOCK, :].T.astype(BF16)
    vbuf[:, BLOCK + tq:] = vn_ref[...].T.astype(BF16)

    key = lax.broadcasted_iota(jnp.int32, (3 * BLOCK, BLOCK), 0)
    col_prev = jnp.where(key < BLOCK, pen_prev, 0.0)
    col_next = jnp.where(key >= 2 * BLOCK, pen_next, 0.0)
    scale = HEAD_DIM ** -0.5
    dn = (((1,), (1,)), ((), ()))
    nsb = tq // BLOCK
    for sb in range(nsb):
        r0 = sb * BLOCK
        k3 = kbuf[r0:r0 + 3 * BLOCK, :]
        v3 = vbuf[:, r0:r0 + 3 * BLOCK]
        for gi in range(KV_GROUP):
            sink = sink_ref[h * KV_GROUP + gi] * LOG2E
            q = q_ref[r0:r0 + BLOCK, gi * HEAD_DIM:(gi + 1) * HEAD_DIM].astype(BF16)
            s = lax.dot_general(k3, q, dn, preferred_element_type=F32) * (scale * LOG2E)
            s = s + bias_ref[gi * 3 * BLOCK:(gi + 1) * 3 * BLOCK, :]
            if sb == 0:
                s = s + col_prev
            if sb == nsb - 1:
                s = s + col_next
            m = jnp.maximum(jnp.max(s, axis=0, keepdims=True), sink)
            p = jnp.exp2(s - m)
            denom = jnp.sum(p, axis=0, keepdims=True) + jnp.exp2(sink - m)
            o_t = _dot(v3, p.astype(BF16)) * (1.0 / denom)
            o_ref[r0:r0 + BLOCK, gi * HEAD_DIM:(gi + 1) * HEAD_DIM] = o_t.T.astype(o_ref.dtype)


def _attention(proj, sink, tp, lp, ls):
    t = proj.shape[0]
    tq = math.gcd(ATTN_TQ, math.gcd(lp, ls))
    nblk = t // BLOCK
    bpt = tq // BLOCK
    slopes = 2.0 ** (-8.0 * jnp.arange(1, N_HEADS + 1, dtype=F32) / N_HEADS)
    qi = jnp.arange(BLOCK)[:, None]
    ki = jnp.arange(3 * BLOCK)[None, :]
    dist = jnp.abs(qi + BLOCK - ki)
    bias = jnp.where(dist[None] <= BLOCK, -(slopes * LOG2E)[:, None, None] * dist[None].astype(F32), -jnp.inf)
    bias = jnp.swapaxes(bias, 1, 2).reshape(N_HEADS * 3 * BLOCK, BLOCK)
    qc = OFF_Q // (KV_GROUP * HEAD_DIM)
    kc = OFF_K // HEAD_DIM
    vc = OFF_V // HEAD_DIM
    prev = lambda g: jnp.maximum(g * bpt - 1, 0)
    nxt = lambda g: jnp.minimum((g + 1) * bpt, nblk - 1)
    halo = (BLOCK, HEAD_DIM)
    cur = (tq, HEAD_DIM)
    return pl.pallas_call(
        functools.partial(_attn_kernel, tq=tq, tiles_p=tp // tq, per_p=lp // tq, per_s=ls // tq),
        grid=(t // tq, N_KV_HEADS),
        in_specs=[
            pl.BlockSpec((tq, KV_GROUP * HEAD_DIM), lambda g, h: (g, qc + h)),
            pl.BlockSpec(halo, lambda g, h: (prev(g), kc + h)),
            pl.BlockSpec(cur, lambda g, h: (g, kc + h)),
            pl.BlockSpec(halo, lambda g, h: (nxt(g), kc + h)),
            pl.BlockSpec(halo, lambda g, h: (prev(g), vc + h)),
            pl.BlockSpec(cur, lambda g, h: (g, vc + h)),
            pl.BlockSpec(halo, lambda g, h: (nxt(g), vc + h)),
            pl.BlockSpec((KV_GROUP * 3 * BLOCK, BLOCK), lambda g, h: (h, 0)),
            pl.BlockSpec(memory_space=pltpu.SMEM),
        ],
        out_specs=pl.BlockSpec((tq, KV_GROUP * HEAD_DIM), lambda g, h: (g, h)),
        out_shape=jax.ShapeDtypeStruct((t, ATTN_WIDTH), BF16),
        scratch_shapes=[pltpu.VMEM((tq + 2 * BLOCK, HEAD_DIM), BF16), pltpu.VMEM((HEAD_DIM, tq + 2 * BLOCK), BF16)],
        compiler_params=_cparams(("parallel", "parallel")),
        name="band_attention",
    )(proj, proj, proj, proj, proj, proj, proj, bias, sink.astype(F32))


def _filter_kernel(fvec_ref, w1_ref, b1_ref, w2_ref, b2_ref, w3_ref, b3_ref, fr_ref, w4_ref, delta_ref, o_ref,
                   *, seq, tl):
    r0 = pl.program_id(0) * tl
    row = (lax.broadcasted_iota(jnp.int32, (tl, 128), 0) + r0).astype(F32)
    lane = lax.broadcasted_iota(jnp.int32, (tl, 128), 1)
    t = row * (1.0 / (seq - 1))
    ang = (row * (2.0 * math.pi / seq)) * fvec_ref[...]
    z = jnp.where(lane == 0, t,
                  jnp.where(lane <= 16, jnp.cos(ang), jnp.where(lane <= 32, -jnp.sin(ang), 0.0)))
    def dot3(x, w_ref, cols=slice(None)):
        x_hi, x_lo = _split(x)
        w_hi, w_lo = w_ref[0, :, cols], w_ref[1, :, cols]
        return _dot(x_hi, w_hi) + (_dot(x_lo, w_hi) + _dot(x_hi, w_lo))

    h = jnp.sin(fr_ref[0:1, :] * (dot3(z, w1_ref) + b1_ref[...]))
    h = jnp.sin(fr_ref[1:2, :] * (dot3(h, w2_ref) + b2_ref[...]))
    h = jnp.sin(fr_ref[2:3, :] * (dot3(h, w3_ref) + b3_ref[...]))
    trow = (lax.broadcasted_iota(jnp.int32, (tl, HYENA_WIDTH), 0) + r0)
    decay = jnp.exp(-(trow.astype(F32) * (1.0 / (seq - 1))) * delta_ref[...])
    first = trow == 0
    for part in range(4):
        sl = slice(part * HYENA_WIDTH, (part + 1) * HYENA_WIDTH)
        v = dot3(h, w4_ref, sl) * decay
        if part >= 2:
            v = jnp.where(first, 0.0, v)
        o_ref[:, sl] = v


def _hyena_filters(seq, w1, b1, w2, b2, w3, b3, freq, w4):
    tl = min(seq, 512)
    bands = (FILTER_EMB - 1) // 2
    f = jnp.linspace(1e-4, bands - 1, bands, dtype=F32)
    fvec = jnp.zeros((1, 128), F32).at[0, 1:1 + bands].set(f).at[0, 1 + bands:1 + 2 * bands].set(f)
    w1p = jnp.zeros((128, FILTER_HIDDEN), F32).at[:FILTER_EMB].set(w1.astype(F32))
    max_decay = math.log(DECAY_TARGET) / DECAY_FAST_PCT
    min_decay = math.log(DECAY_TARGET) / DECAY_SLOW_PCT
    deltas = jnp.abs(jnp.linspace(min_decay, max_decay, HYENA_WIDTH, dtype=F32)).reshape(1, -1)
    planes = lambda w: jnp.stack(_split_const(w))
    args = [fvec, planes(w1p), b1.reshape(1, -1).astype(F32), planes(w2), b2.reshape(1, -1).astype(F32),
            planes(w3), b3.reshape(1, -1).astype(F32), freq.astype(F32), planes(w4), deltas]
    return pl.pallas_call(
        functools.partial(_filter_kernel, seq=seq, tl=tl),
        grid=(seq // tl,),
        in_specs=[_full_spec(a) for a in args],
        out_specs=pl.BlockSpec((tl, FILTER_OUT), lambda i: (i, 0)),
        out_shape=jax.ShapeDtypeStruct((seq, FILTER_OUT), F32),
        compiler_params=_cparams(("parallel",)),
        name="hyena_filter",
    )(*args)


def _dotp(mh_ref, ml_ref, x, passes):
    if passes == 1:
        return _dot(mh_ref[...], x.astype(BF16))
    return _dot3_left(mh_ref[...], ml_ref[...], x)


def _gather(ref, t2, n):
    return ref[pl.ds(t2, n, stride=PITCH), :]


def _fill_pitched(u_ref, p_ref, half, w_ref=None, b_ref=None):
    n2 = DFT_N2
    seq = half * n2
    rid = lax.broadcasted_iota(jnp.int32, (n2, LANES), 0)

    def step(t1, carry):
        r0 = pl.multiple_of(t1 * n2, n2)
        cur = u_ref[pl.ds(r0, n2), :]
        if w_ref is not None:
            before = u_ref[pl.ds(pl.multiple_of(jnp.maximum(r0 - 8, 0), 8), 8), :][7:8, :]
            after = u_ref[pl.ds(pl.multiple_of(jnp.minimum(r0 + n2, seq - 8), 8), 8), :][0:1, :]
            before = jnp.where(t1 > 0, before, 0.0)
            after = jnp.where(t1 < half - 1, after, 0.0)
            prev = jnp.where(rid == 0, before, pltpu.roll(cur, 1, 0))
            nxt = jnp.where(rid == n2 - 1, after, pltpu.roll(cur, n2 - 1, 0))
            cur = prev * w_ref[0:1, :] + cur * w_ref[1:2, :] + nxt * w_ref[2:3, :] + b_ref[...]
        p_ref[pl.ds(pl.multiple_of(t1 * PITCH, 8), n2), :] = cur
        return carry

    lax.fori_loop(0, half, step, 0)


def _for_t2_groups(body):
    def step(g, carry):
        body(g * T2_GROUP)
        return carry

    lax.fori_loop(0, DFT_N2 // T2_GROUP, step, 0, unroll=2)


def _outer_stage(p_ref, half, mh_ref, ml_ref, a_ref, nrows, passes):
    def body(base):
        xs = jnp.concatenate([_gather(p_ref, base + j, half) for j in range(T2_GROUP)], axis=1)
        y = _dotp(mh_ref, ml_ref, xs, passes)
        for j in range(T2_GROUP):
            a_ref[pl.ds(base + j, nrows, stride=PITCH), :] = y[:, j * LANES:(j + 1) * LANES]

    _for_t2_groups(body)


def _lane_block(x, j):
    return x[:, j * LANES:(j + 1) * LANES]


def _inner_blocks(a_ref, tw_ref, wstep_ref, nblocks, group_fn, store_fn):
    n2 = DFT_N2
    tw_ref[0:n2, :] = jnp.ones((n2, LANES), F32)
    tw_ref[n2:, :] = jnp.zeros((n2, LANES), F32)

    def run(k0, count):
        twr = tw_ref[0:n2, :]
        twi = tw_ref[n2:, :]
        wr = wstep_ref[0:n2, :]
        wi = wstep_ref[n2:, :]
        ks, rows, tws, blocks = [], [], [], []
        for j in range(count):
            k = k0 + j
            rr = pl.multiple_of(k * 2 * PITCH, 8)
            ri = pl.multiple_of(k * 2 * PITCH + PITCH, 8)
            ar = a_ref[pl.ds(rr, n2), :]
            ai = a_ref[pl.ds(ri, n2), :]
            ks.append(k)
            rows.append((rr, ri))
            tws.append((twr, twi))
            blocks.append(jnp.concatenate([ar * twr - ai * twi, ar * twi + ai * twr], axis=0))
            twr, twi = twr * wr - twi * wi, twr * wi + twi * wr
        vals = group_fn(ks, jnp.concatenate(blocks, axis=1), tws)
        for k, (rr, ri), val in zip(ks, rows, vals):
            store_fn(k, rr, ri, val)
        tw_ref[0:n2, :] = twr
        tw_ref[n2:, :] = twi

    def step(g, carry):
        run(g * INNER_GROUP, INNER_GROUP)
        return carry

    lax.fori_loop(0, nblocks // INNER_GROUP, step, 0)
    if nblocks % INNER_GROUP:
        run(nblocks - nblocks % INNER_GROUP, nblocks % INNER_GROUP)


def _hyena_conv_kernel(sig_ref, gate_ref, wsig_ref, bsig_ref, wgate_ref, bgate_ref, skip_ref, kf_ref,
                       f1h_ref, f1l_ref, g3h_ref, g3l_ref, gch_ref, gcl_ref, gbh_ref, gbl_ref, wstep_ref,
                       o_ref, a_ref, tw_ref, pv_ref, pg_ref, *, half, k1, k1p, sig_conv):
    n2 = DFT_N2
    if sig_conv:
        _fill_pitched(sig_ref, pv_ref, half, wsig_ref, bsig_ref)
    else:
        _fill_pitched(sig_ref, pv_ref, half)
    _fill_pitched(gate_ref, pg_ref, half, wgate_ref, bgate_ref)
    _outer_stage(pv_ref, half, f1h_ref, f1l_ref, a_ref, 2 * k1p, CONV_PASSES)

    def block(ks, xcat, tws):
        x = _dotp(gch_ref, gcl_ref, xcat, CONV_PASSES)
        prods = []
        for j, k in enumerate(ks):
            xr, xi = _lane_block(x[:n2], j), _lane_block(x[n2:], j)
            k0 = pl.multiple_of(k * 2 * n2, 2 * n2)
            kr = kf_ref[pl.ds(k0, n2), :].astype(F32)
            ki = kf_ref[pl.ds(k0 + n2, n2), :].astype(F32)
            prods.append(jnp.concatenate([xr * kr - xi * ki, xr * ki + xi * kr], axis=0))
        bb = _dotp(gbh_ref, gbl_ref, jnp.concatenate(prods, axis=1), CONV_PASSES)
        out = []
        for j, (twr, twi) in enumerate(tws):
            br, bi = _lane_block(bb[:n2], j), _lane_block(bb[n2:], j)
            out.append((br * twr + bi * twi, bi * twr - br * twi))
        return out

    def put(k, rr, ri, vals):
        a_ref[pl.ds(rr, n2), :] = vals[0]
        a_ref[pl.ds(ri, n2), :] = vals[1]

    _inner_blocks(a_ref, tw_ref, wstep_ref, k1, block, put)

    def finish(base):
        bcat = jnp.concatenate([_gather(a_ref, base + j, 2 * k1p) for j in range(T2_GROUP)], axis=1)
        y = _dotp(g3h_ref, g3l_ref, bcat, CONV_PASSES)
        for j in range(T2_GROUP):
            v = _gather(pv_ref, base + j, half)
            gate = _gather(pg_ref, base + j, half)
            pg_ref[pl.ds(base + j, half, stride=PITCH), :] = gate * (y[:, j * LANES:(j + 1) * LANES]
                                                                    + v * skip_ref[...])

    _for_t2_groups(finish)

    def emit(t1, carry):
        o_ref[pl.ds(pl.multiple_of(t1 * n2, n2), n2), :] = pg_ref[pl.ds(pl.multiple_of(t1 * PITCH, 8), n2), :]
        return carry

    lax.fori_loop(0, half, emit, 0)


def _hyena_spec_kernel(ff_ref, fb_ref, f2h_ref, f2l_ref, gch_ref, gcl_ref, wstep_ref, rev_ref, o_ref, a_ref, tw_ref,
                       pf_ref, *, half, k1, k1p):
    n2 = DFT_N2
    seq = half * n2
    _fill_pitched(ff_ref, pf_ref, half)
    rid = lax.broadcasted_iota(jnp.int32, (n2, LANES), 0)

    def reversed_slab(a, carry):
        s = half - 1 - a
        src = fb_ref[pl.ds(pl.multiple_of(s * n2, n2), n2), :]
        if SPEC_PASSES == 1:
            flipped = _dot(rev_ref[...], src.astype(BF16))
        else:
            src_hi, src_lo = _split(src)
            flipped = _dot(rev_ref[...], src_hi) + _dot(rev_ref[...], src_lo)
        head = fb_ref[pl.ds(pl.multiple_of(jnp.minimum((s + 1) * n2, seq - 8), 8), 8), :][0:1, :]
        head = jnp.where(a > 0, head, 0.0)
        pf_ref[pl.ds(pl.multiple_of((half + a) * PITCH, 8), n2), :] = jnp.where(rid == 0, head, flipped)
        return carry

    lax.fori_loop(0, half, reversed_slab, 0, unroll=8)
    _outer_stage(pf_ref, 2 * half, f2h_ref, f2l_ref, a_ref, 2 * k1p, SPEC_PASSES)

    def spectrum(ks, xcat, tws):
        x = _dotp(gch_ref, gcl_ref, xcat, SPEC_PASSES)
        return [_lane_block(x, j) for j in range(len(ks))]

    def put(k, rr, ri, x):
        o_ref[pl.ds(pl.multiple_of(k * 2 * n2, 2 * n2), 2 * n2), :] = x.astype(o_ref.dtype)

    _inner_blocks(a_ref, tw_ref, wstep_ref, k1, spectrum, put)


def _conv_tables(seq):
    n = 2 * seq
    n2 = DFT_N2
    n1 = n // n2
    k1 = n1 // 2 + 1
    k1p = -(-k1 // 8) * 8
    half = n1 // 2
    kk = jnp.arange(k1p, dtype=jnp.int32)
    valid = (kk < k1)
    t1 = jnp.arange(half, dtype=jnp.int32)
    c, s = _cs(kk[:, None] * t1[None, :], n1)
    vm = valid[:, None].astype(F32)
    f1 = jnp.stack([c * vm, -s * vm], axis=1).reshape(2 * k1p, half)
    wgt = jnp.where((kk == 0) | (kk == n1 // 2), 1.0, 2.0) * valid.astype(F32) / n
    g3 = jnp.stack([c * vm * wgt[:, None], -s * vm * wgt[:, None]], axis=1).reshape(2 * k1p, half).T
    j = jnp.arange(n2, dtype=jnp.int32)
    cr, cs_ = _cs(j[:, None] * j[None, :], n2)
    gc = jnp.concatenate([jnp.concatenate([cr, cs_], axis=1), jnp.concatenate([-cs_, cr], axis=1)], axis=0)
    gb = jnp.concatenate([jnp.concatenate([cr, -cs_], axis=1), jnp.concatenate([cs_, cr], axis=1)], axis=0)
    wr, ws = _cs(j, n)
    wstep = jnp.concatenate([jnp.broadcast_to(wr[:, None], (n2, LANES)),
                             jnp.broadcast_to(-ws[:, None], (n2, LANES))], axis=0)
    tall = jnp.arange(n1, dtype=jnp.int32)
    c2, s2 = _cs(kk[:, None] * tall[None, :], n1)
    f2 = jnp.stack([c2 * vm, -s2 * vm], axis=1).reshape(2 * k1p, n1)
    rev = ((j[:, None] + j[None, :]) == n2).astype(BF16)
    tabs = dict(n1=n1, k1=k1, k1p=k1p, half=half, wstep=wstep, rev=rev)
    for name, m in (("f1", f1), ("f2", f2), ("g3", g3), ("gc", gc), ("gb", gb)):
        tabs[name + "h"], tabs[name + "l"] = _split_const(m)
    return tabs


def _filter_spectrum(seq, tabs, fw):
    filt = _hyena_filters(seq, *fw)
    half, k1, k1p = tabs["half"], tabs["k1"], tabs["k1p"]
    n2 = DFT_N2
    nct = HYENA_WIDTH // LANES
    consts = [tabs[n] for n in ("f2h", "f2l", "gch", "gcl", "wstep", "rev")]
    return pl.pallas_call(
        functools.partial(_hyena_spec_kernel, half=half, k1=k1, k1p=k1p),
        grid=(2, nct),
        in_specs=[pl.BlockSpec((seq, LANES), lambda o, j: (0, o * nct + j)),
                  pl.BlockSpec((seq, LANES), lambda o, j: (0, (2 + o) * nct + j))] + [_full_spec(a) for a in consts],
        out_specs=pl.BlockSpec((None, None, k1 * 2 * n2, LANES), lambda o, j: (o, j, 0, 0)),
        out_shape=jax.ShapeDtypeStruct((2, nct, k1 * 2 * n2, LANES), BF16),
        scratch_shapes=[pltpu.VMEM((k1p * 2 * PITCH, LANES), F32), pltpu.VMEM((2 * n2, LANES), F32),
                        pltpu.VMEM((2 * half * PITCH, LANES), F32)],
        compiler_params=_cparams(("parallel", "parallel")),
        name="hyena_filter_spectrum",
    )(filt, filt, *consts)


def _skip_first_ref(kernel_fn, *refs, **kwargs):
    return kernel_fn(*refs[1:], **kwargs)


def _merged_out(kernel_fn, args, in_specs, into, total_rows, width, dtype):
    out_shape = jax.ShapeDtypeStruct((total_rows, width), dtype)
    if into is None:
        return kernel_fn, args, in_specs, out_shape, {}
    return (functools.partial(_skip_first_ref, kernel_fn), [into] + args,
            [pl.BlockSpec(memory_space=pl.ANY)] + in_specs, out_shape, {0: 0})


def _hyena_conv(sig, sig_cols, gate_cols, proj, row0, nbatch, seq, tabs, kf, order, short_w, short_b, skip,
                total_rows=None, into=None):
    assert row0 % seq == 0
    b0 = row0 // seq
    half, k1, k1p = tabs["half"], tabs["k1"], tabs["k1p"]
    n2 = DFT_N2
    nct = HYENA_WIDTH // LANES
    sig_conv = sig is None
    if sig_conv:
        sig_arr = proj
        sig_spec = pl.BlockSpec((seq, LANES), lambda j, b: (b0 + b, sig_cols + j))
    else:
        sig_arr = sig
        sig_spec = pl.BlockSpec((seq, LANES), lambda j, b: (b, j))
    sw = short_w.astype(F32)
    sb = short_b.reshape(1, -1).astype(F32)
    consts = [tabs[n] for n in ("f1h", "f1l", "g3h", "g3l", "gch", "gcl", "gbh", "gbl", "wstep")]
    args = [sig_arr, proj, sw, sb, sw, sb, skip.reshape(1, -1).astype(F32), kf] + consts
    in_specs = [
        sig_spec,
        pl.BlockSpec((seq, LANES), lambda j, b: (b0 + b, gate_cols + j)),
        pl.BlockSpec((3, LANES), lambda j, b: (0, sig_cols + j)),
        pl.BlockSpec((1, LANES), lambda j, b: (0, sig_cols + j)),
        pl.BlockSpec((3, LANES), lambda j, b: (0, gate_cols + j)),
        pl.BlockSpec((1, LANES), lambda j, b: (0, gate_cols + j)),
        pl.BlockSpec((1, LANES), lambda j, b: (0, j)),
        pl.BlockSpec((None, None, k1 * 2 * n2, LANES), lambda j, b: (order, j, 0, 0)),
    ] + [_full_spec(a) for a in consts]
    kern = functools.partial(_hyena_conv_kernel, half=half, k1=k1, k1p=k1p, sig_conv=sig_conv)
    out_b0 = 0 if total_rows is None else b0
    kern, args, in_specs, out_shape, aliases = _merged_out(
        kern, args, in_specs, into, total_rows or nbatch * seq, HYENA_WIDTH, F32)
    return pl.pallas_call(
        kern,
        grid=(nct, nbatch),
        in_specs=in_specs,
        out_specs=pl.BlockSpec((seq, LANES), lambda j, b: (out_b0 + b, j)),
        out_shape=out_shape,
        input_output_aliases=aliases,
        scratch_shapes=[pltpu.VMEM((k1p * 2 * PITCH, LANES), F32), pltpu.VMEM((2 * n2, LANES), F32),
                        pltpu.VMEM((half * PITCH, LANES), F32), pltpu.VMEM((half * PITCH, LANES), F32)],
        compiler_params=_cparams(("parallel", "arbitrary")),
        name="hyena_conv",
    )(*args)


def _hyena_batch(proj, row0, nbatch, seq, tabs, kf, short_w, short_b, skip, total_rows, into):
    nct = HYENA_WIDTH // LANES
    z = _hyena_conv(None, 2 * nct, 0, proj, row0, nbatch, seq, tabs, kf, 0, short_w, short_b, skip[0])
    return _hyena_conv(z, 2 * nct, nct, proj, row0, nbatch, seq, tabs, kf, 1, short_w, short_b, skip[1],
                       total_rows=total_rows, into=into)


def _fnet_kernel(u_ref, chan_ref, m1_ref, gri_ref, wstep_ref, o_ref, zr_ref, zi_ref, a_ref, tw_ref, *, n1):
    n2 = DFT_N2
    slabs = 4

    def chan(g, carry):
        x = u_ref[pl.ds(pl.multiple_of(g * slabs * n2, slabs * n2), slabs * n2), :].astype(BF16)
        z = _dot(x, chan_ref[...])
        for i in range(slabs):
            r = pl.multiple_of((g * slabs + i) * PITCH, 8)
            zr_ref[pl.ds(r, n2), :] = z[i * n2:(i + 1) * n2, :LANES]
            zi_ref[pl.ds(r, n2), :] = z[i * n2:(i + 1) * n2, LANES:]
        return carry

    lax.fori_loop(0, n1 // slabs, chan, 0, unroll=2)

    def outer(base):
        xs = jnp.concatenate(
            [jnp.concatenate([_gather(zr_ref, base + j, n1), _gather(zi_ref, base + j, n1)], axis=0)
             for j in range(T2_GROUP)], axis=1)
        y = _dot(m1_ref[...], xs.astype(BF16))
        for j in range(T2_GROUP):
            a_ref[pl.ds(base + j, 2 * n1, stride=PITCH), :] = y[:, j * LANES:(j + 1) * LANES]

    _for_t2_groups(outer)

    def real_part(ks, xcat, tws):
        y = _dot(gri_ref[...], xcat.astype(BF16))
        return [_lane_block(y, j) for j in range(len(ks))]

    def put(k, rr, ri, y):
        zr_ref[pl.ds(pl.multiple_of(k * PITCH, 8), n2), :] = y

    _inner_blocks(a_ref, tw_ref, wstep_ref, n1, real_part, put)

    def emit(k2, carry):
        o_ref[pl.ds(pl.multiple_of(k2 * n1, n1), n1), :] = _gather(zr_ref, k2, n1).astype(o_ref.dtype)
        return carry

    lax.fori_loop(0, n2, emit, 0, unroll=4)


def _fnet_tables(seq):
    n2 = DFT_N2
    n1 = seq // n2
    j = jnp.arange(HEAD_DIM, dtype=jnp.int32)
    c, s = _cs(j[:, None] * j[None, :], HEAD_DIM)
    scale = (seq * HEAD_DIM) ** -0.5
    chan = jnp.concatenate([c * scale, -s * scale], axis=1).astype(BF16)
    kk = jnp.arange(n1, dtype=jnp.int32)
    c1, s1 = _cs(kk[:, None] * kk[None, :], n1)
    m1 = jnp.stack([jnp.concatenate([c1, s1], axis=1), jnp.concatenate([-s1, c1], axis=1)], axis=1)
    m1 = m1.reshape(2 * n1, 2 * n1).astype(BF16)
    t2 = jnp.arange(n2, dtype=jnp.int32)
    cr, cs_ = _cs(t2[:, None] * t2[None, :], n2)
    gri = jnp.concatenate([cr, cs_], axis=1).astype(BF16)
    wr, ws = _cs(t2, seq)
    wstep = jnp.concatenate([jnp.broadcast_to(wr[:, None], (n2, LANES)),
                             jnp.broadcast_to(-ws[:, None], (n2, LANES))], axis=0)
    return dict(n1=n1, chan=chan, m1=m1, gri=gri, wstep=wstep)


def _fnet_batch(proj, row0, nbatch, seq, tabs, total_rows, into):
    assert row0 % seq == 0
    b0 = row0 // seq
    n1 = tabs["n1"]
    consts = [tabs[n] for n in ("chan", "m1", "gri", "wstep")]
    in_specs = ([pl.BlockSpec((seq, LANES), lambda j, b: (b0 + b, OFF_FNET // LANES + j))]
                + [_full_spec(a) for a in consts])
    kern, args, in_specs, out_shape, aliases = _merged_out(
        functools.partial(_fnet_kernel, n1=n1), [proj] + consts, in_specs, into, total_rows, FNET_WIDTH, BF16)
    return pl.pallas_call(
        kern,
        grid=(FNET_HEADS, nbatch),
        in_specs=in_specs,
        out_specs=pl.BlockSpec((seq, LANES), lambda j, b: (b0 + b, j)),
        out_shape=out_shape,
        input_output_aliases=aliases,
        scratch_shapes=[pltpu.VMEM((n1 * PITCH, LANES), F32), pltpu.VMEM((n1 * PITCH, LANES), F32),
                        pltpu.VMEM((2 * n1 * PITCH, LANES), F32), pltpu.VMEM((2 * DFT_N2, LANES), F32)],
        compiler_params=_cparams(("parallel", "parallel")),
        name="fnet_mixer",
    )(*args)


def _pick_tile(t, pref):
    while t % pref:
        pref //= 2
    return pref


def kernel(x_prompt, x_sample, ln0_g, ln0_b, w_in, short_w, short_b, filt_w1, filt_b1, filt_w2, filt_b2, filt_w3, filt_b3, filt_freq, filt_w4, hyena_skip, w_fnet, b_fnet, attn_sink, w_out, ln1_g, ln1_b, w_gate, w_up, w_down, ln2_g, ln2_b):
    bp, lp, _ = x_prompt.shape
    bs, ls, _ = x_sample.shape
    tp, ts = bp * lp, bs * ls
    batches = ((0, bp, lp), (tp, bs, ls))
    tm = _pick_tile(math.gcd(tp, ts), 1024)
    tln = _pick_tile(math.gcd(tp, ts), 512)

    conv_tabs = {seq: _conv_tables(seq) for seq in {lp, ls}}
    fnet_tabs = {seq: _fnet_tables(seq) for seq in {lp, ls}}

    resid, xb = _ln0(x_prompt.reshape(tp, D_MODEL), x_sample.reshape(ts, D_MODEL), ln0_g, ln0_b, tln // 2)
    w_in_b, w_out_b, w_down_b = w_in.astype(BF16), w_out.astype(BF16), w_down.astype(BF16)
    for l in range(DEPTH):
        fw = (filt_w1[l], filt_b1[l], filt_w2[l], filt_b2[l], filt_w3[l], filt_b3[l], filt_freq[l], filt_w4[l])

        proj = _matmul([xb], w_in_b, None, F32, tm, 1024, "in_proj", layer=l)

        kf = {seq: _filter_spectrum(seq, conv_tabs[seq], fw) for seq in {lp, ls}}
        y_h = y_f = None
        for r0, nb, seq in batches:
            y_h = _hyena_batch(proj, r0, nb, seq, conv_tabs[seq], kf[seq], short_w[l], short_b[l], hyena_skip[l],
                               tp + ts, y_h)
            y_f = _fnet_batch(proj, r0, nb, seq, fnet_tabs[seq], tp + ts, y_f)
        y_f = _matmul([y_f], w_fnet, b_fnet[l], BF16, tm, 1024, "fnet_linear", layer=l)
        y_a = _attention(proj, attn_sink[l], tp, lp, ls)

        y = _matmul([y_h, y_f, y_a], w_out_b, None, F32, tm, 512, "out_proj", resid=resid, layer=l)
        xb, mu, rs = _ln(y, ln1_g[l], ln1_b[l], tln)
        resid = (y, mu, rs, ln1_g[l], ln1_b[l])

        hid = _gate_up(xb, w_gate, w_up, l, _pick_tile(math.gcd(tp, ts), 2048), FF_TILE)
        y = _matmul([hid], w_down_b, None, F32, _pick_tile(tm, 512), 512, "ffn_down", resid=resid, layer=l)
        if l + 1 < DEPTH:
            xb, mu, rs = _ln(y, ln2_g[l], ln2_b[l], tln)
            resid = (y, mu, rs, ln2_g[l], ln2_b[l])
        else:
            y_p, y_s = _ln_final(y, ln2_g[l], ln2_b[l], tln // 2, tp)
    return y_p.reshape(bp, lp, D_MODEL), y_s.reshape(bs, ls, D_MODEL)
```

```python
import functools
import math

import jax
import jax.numpy as jnp
from jax import lax
from jax.experimental import pallas as pl
from jax.experimental.pallas import tpu as pltpu

F32 = jnp.float32
BF16 = jnp.bfloat16

D_MODEL = 4096
HEAD_DIM = 128
HYENA_WIDTH = 1024
FNET_WIDTH = 1024
ATTN_WIDTH = 2048
FNET_HEADS = 8
N_HEADS = 16
N_KV_HEADS = 4
KV_GROUP = 4
KV_WIDTH = 512
BLOCK = 128
HYENA_IN = 3 * HYENA_WIDTH
FILTER_EMB = 33
FILTER_HIDDEN = 64
FILTER_OUT = 4 * HYENA_WIDTH
OFF_FNET = HYENA_IN
OFF_Q = OFF_FNET + FNET_WIDTH
OFF_K = OFF_Q + ATTN_WIDTH
OFF_V = OFF_K + KV_WIDTH
IN_WIDTH = OFF_V + KV_WIDTH
D_FF = 11008
FF_TILE = 256
GATE_UP_PARTS = 2
DEPTH = 2
ALPHA = (2 * DEPTH) ** 0.25
LN_EPS = 1e-5
LOG2E = math.log2(math.e)
DECAY_FAST_PCT = 0.3
DECAY_SLOW_PCT = 1.5
DECAY_TARGET = 1e-2

DFT_N2 = 128
LANES = 128
T2_GROUP = 8
PITCH = 136
CONV_PASSES = 1
SPEC_PASSES = 1
INNER_GROUP = 8
ATTN_TQ = 2048
VMEM_LIMIT = 56 * 1024 * 1024


def _cparams(sem, vmem=VMEM_LIMIT):
    return pltpu.CompilerParams(dimension_semantics=sem, vmem_limit_bytes=vmem)


def _dot(a, b):
    return jnp.dot(a, b, preferred_element_type=F32)


def _split(x):
    hi = x.astype(BF16)
    lo = (x - hi.astype(F32)).astype(BF16)
    return hi, lo


def _dot3_left(m_hi, m_lo, x):
    x_hi, x_lo = _split(x)
    return _dot(m_hi, x_hi) + (_dot(m_hi, x_lo) + _dot(m_lo, x_hi))


def _split_const(m):
    m = m.astype(F32)
    hi = m.astype(BF16)
    lo = (m - hi.astype(F32)).astype(BF16)
    return hi, lo


def _cs(num, den):
    ang = (2.0 * math.pi / den) * (num % den).astype(F32)
    return jnp.cos(ang), jnp.sin(ang)


def _full_spec(a):
    return pl.BlockSpec(a.shape, lambda *_: (0,) * a.ndim)


def _ln_math(x, g, b):
    mu = jnp.mean(x, axis=-1, keepdims=True)
    xc = x - mu
    var = jnp.mean(xc * xc, axis=-1, keepdims=True)
    return xc * lax.rsqrt(var + LN_EPS) * g + b


def _ln0_kernel(xp_ref, xs_ref, g_ref, b_ref, of_ref, ob_ref, *, n_p):
    i = pl.program_id(0)

    def emit(x):
        y = _ln_math(x, g_ref[...], b_ref[...])
        of_ref[...] = y
        ob_ref[...] = y.astype(BF16)

    @pl.when(i < n_p)
    def _():
        emit(xp_ref[...])

    @pl.when(i >= n_p)
    def _():
        emit(xs_ref[...])


def _ln0(xp, xs, g, b, tm):
    tp, ts = xp.shape[0], xs.shape[0]
    n_p, n_s = tp // tm, ts // tm
    t = tp + ts
    return pl.pallas_call(
        functools.partial(_ln0_kernel, n_p=n_p),
        grid=(n_p + n_s,),
        in_specs=[
            pl.BlockSpec((tm, D_MODEL), lambda i: (jnp.minimum(i, n_p - 1), 0)),
            pl.BlockSpec((tm, D_MODEL), lambda i: (jnp.maximum(i - n_p, 0), 0)),
            pl.BlockSpec((1, D_MODEL), lambda i: (0, 0)),
            pl.BlockSpec((1, D_MODEL), lambda i: (0, 0)),
        ],
        out_specs=[
            pl.BlockSpec((tm, D_MODEL), lambda i: (i, 0)),
            pl.BlockSpec((tm, D_MODEL), lambda i: (i, 0)),
        ],
        out_shape=[jax.ShapeDtypeStruct((t, D_MODEL), F32), jax.ShapeDtypeStruct((t, D_MODEL), BF16)],
        compiler_params=_cparams(("parallel",)),
        name="ln0",
    )(xp, xs, g.reshape(1, -1), b.reshape(1, -1))


def _ln_kernel(y_ref, g_ref, b_ref, ob_ref, mu_ref, rs_ref):
    y = y_ref[...]
    mu = jnp.mean(y, axis=-1, keepdims=True)
    xc = y - mu
    rs = lax.rsqrt(jnp.mean(xc * xc, axis=-1, keepdims=True) + LN_EPS)
    ob_ref[...] = (xc * rs * g_ref[...] + b_ref[...]).astype(BF16)
    mu_ref[...] = jnp.broadcast_to(mu, mu_ref.shape)
    rs_ref[...] = jnp.broadcast_to(rs, rs_ref.shape)


def _ln(y, g, b, tm):
    t = y.shape[0]
    row = pl.BlockSpec((tm, D_MODEL), lambda i: (i, 0))
    vec = pl.BlockSpec((1, D_MODEL), lambda i: (0, 0))
    stat = pl.BlockSpec((tm, LANES), lambda i: (i, 0))
    return pl.pallas_call(
        _ln_kernel,
        grid=(t // tm,),
        in_specs=[row, vec, vec],
        out_specs=[row, stat, stat],
        out_shape=[jax.ShapeDtypeStruct((t, D_MODEL), BF16), jax.ShapeDtypeStruct((t, LANES), F32),
                   jax.ShapeDtypeStruct((t, LANES), F32)],
        compiler_params=_cparams(("parallel",)),
        name="layer_norm",
    )(y, g.reshape(1, -1), b.reshape(1, -1))


def _ln_final_kernel(y_ref, g_ref, b_ref, op_ref, os_ref, *, n_p):
    i = pl.program_id(0)
    y = _ln_math(y_ref[...], g_ref[...], b_ref[...])

    @pl.when(i < n_p)
    def _():
        op_ref[...] = y

    @pl.when(i >= n_p)
    def _():
        os_ref[...] = y


def _ln_final(y, g, b, tm, tp):
    t = y.shape[0]
    n_p = tp // tm
    row = pl.BlockSpec((tm, D_MODEL), lambda i: (i, 0))
    vec = pl.BlockSpec((1, D_MODEL), lambda i: (0, 0))
    return pl.pallas_call(
        functools.partial(_ln_final_kernel, n_p=n_p),
        grid=(t // tm,),
        in_specs=[row, vec, vec],
        out_specs=[
            pl.BlockSpec((tm, D_MODEL), lambda i: (jnp.minimum(i, n_p - 1), 0)),
            pl.BlockSpec((tm, D_MODEL), lambda i: (jnp.maximum(i - n_p, 0), 0)),
        ],
        out_shape=[jax.ShapeDtypeStruct((tp, D_MODEL), F32), jax.ShapeDtypeStruct((t - tp, D_MODEL), F32)],
        compiler_params=_cparams(("arbitrary",)),
        name="layer_norm_final",
    )(y, g.reshape(1, -1), b.reshape(1, -1))


def _mm_kernel(*refs, widths, has_bias, resid_mode):
    n_a = len(widths)
    a_refs = refs[:n_a]
    w_ref = refs[n_a]
    o_ref = refs[-1]
    nxt = n_a + 1
    acc = None
    off = 0
    for a_ref, wd in zip(a_refs, widths):
        part = _dot(a_ref[...].astype(BF16), w_ref[off:off + wd, :].astype(BF16))
        acc = part if acc is None else acc + part
        off += wd
    if has_bias:
        acc = acc + refs[nxt][...]
        nxt += 1
    if resid_mode == "plain":
        acc = ALPHA * refs[nxt][...] + acc
    elif resid_mode == "normalise":
        y_ref, mu_ref, rs_ref, g_ref, b_ref = refs[nxt:nxt + 5]
        reps = o_ref.shape[1] // LANES
        mu = jnp.concatenate([mu_ref[...]] * reps, axis=1)
        rs = jnp.concatenate([rs_ref[...]] * reps, axis=1)
        acc = ALPHA * ((y_ref[...] - mu) * rs * g_ref[...] + b_ref[...]) + acc
    o_ref[...] = acc.astype(o_ref.dtype)


def _matmul(a_list, w, bias, out_dtype, tm, tn, name, resid=None, layer=None):
    t = a_list[0].shape[0]
    k, n = w.shape[-2:]
    widths = tuple(a.shape[1] for a in a_list)
    assert sum(widths) == k and t % tm == 0 and n % tn == 0
    in_specs = [pl.BlockSpec((tm, wd), lambda i, j: (i, 0)) for wd in widths]
    if layer is None:
        in_specs.append(pl.BlockSpec((k, tn), lambda i, j: (0, j)))
    else:
        in_specs.append(pl.BlockSpec((None, k, tn), lambda i, j: (layer, 0, j)))
    args = list(a_list) + [w]
    if bias is not None:
        in_specs.append(pl.BlockSpec((1, tn), lambda i, j: (0, j)))
        args.append(bias.reshape(1, n).astype(F32))
    resid_mode = None
    if isinstance(resid, tuple):
        resid_mode = "normalise"
        y, mu, rs, g, b = resid
        stat = pl.BlockSpec((tm, LANES), lambda i, j: (i, 0))
        vec = pl.BlockSpec((1, tn), lambda i, j: (0, j))
        in_specs += [pl.BlockSpec((tm, tn), lambda i, j: (i, j)), stat, stat, vec, vec]
        args += [y, mu, rs, g.reshape(1, n).astype(F32), b.reshape(1, n).astype(F32)]
    elif resid is not None:
        resid_mode = "plain"
        in_specs.append(pl.BlockSpec((tm, tn), lambda i, j: (i, j)))
        args.append(resid)
    return pl.pallas_call(
        functools.partial(_mm_kernel, widths=widths, has_bias=bias is not None, resid_mode=resid_mode),
        grid=(t // tm, n // tn),
        in_specs=in_specs,
        out_specs=pl.BlockSpec((tm, tn), lambda i, j: (i, j)),
        out_shape=jax.ShapeDtypeStruct((t, n), out_dtype),
        compiler_params=_cparams(("parallel", "parallel")),
        name=name,
    )(*args)


def _gate_up_kernel(x_ref, wg_ref, wu_ref, o_ref):
    wg = wg_ref[...].astype(BF16)
    wu = wu_ref[...].astype(BF16)
    rows = x_ref.shape[0] // GATE_UP_PARTS
    for r0 in range(0, x_ref.shape[0], rows):
        x = x_ref[r0:r0 + rows, :]
        g = _dot(x, wg)
        u = _dot(x, wu)
        o_ref[r0:r0 + rows, :] = (g * (1.0 / (1.0 + jnp.exp(-g))) * u).astype(o_ref.dtype)


def _gate_up(x, wg, wu, layer, tm, tn):
    t, k = x.shape
    n = wg.shape[-1]
    return pl.pallas_call(
        _gate_up_kernel,
        grid=(t // tm, n // tn),
        in_specs=[
            pl.BlockSpec((tm, k), lambda i, j: (i, 0)),
            pl.BlockSpec((None, k, tn), lambda i, j: (layer, 0, j)),
            pl.BlockSpec((None, k, tn), lambda i, j: (layer, 0, j)),
        ],
        out_specs=pl.BlockSpec((tm, tn), lambda i, j: (i, j)),
        out_shape=jax.ShapeDtypeStruct((t, n), BF16),
        compiler_params=_cparams(("parallel", "parallel")),
        name="ffn_gate_up",
    )(x, wg, wu)


def _attn_kernel(q_ref, kp_ref, kc_ref, kn_ref, vp_ref, vc_ref, vn_ref, bias_ref, sink_ref, o_ref, kbuf, vbuf,
                 *, tq, tiles_p, per_p, per_s):
    g = pl.program_id(0)
    h = pl.program_id(1)
    in_p = g < tiles_p
    n_loc = jnp.where(in_p, g % per_p, (g - tiles_p) % per_s)
    n_seq = jnp.where(in_p, per_p, per_s)
    pen_prev = jnp.where(n_loc > 0, 0.0, -jnp.inf)
    pen_next = jnp.where(n_loc < n_seq - 1, 0.0, -jnp.inf)

    kbuf[0:BLOCK, :] = kp_ref[...].astype(BF16)
    kbuf[BLOCK:BLOCK + tq, :] = kc_ref[...].astype(BF16)
    kbuf[BLOCK + tq:, :] = kn_ref[...].astype(BF16)
    vbuf[:, 0:BLOCK] = vp_ref[...].T.astype(BF16)
    for c0 in range(0, tq, BLOCK):
        vbuf[:, BLOCK + c0:2 * BLOCK + c0] = vc_ref[c0:c0 + BLOCK, :].T.astype(BF16)
    vbuf[:, BLOCK + tq:] = vn_ref[...].T.astype(BF16)

    key = lax.broadcasted_iota(jnp.int32, (3 * BLOCK, BLOCK), 0)
    col_prev = jnp.where(key < BLOCK, pen_prev, 0.0)
    col_next = jnp.where(key >= 2 * BLOCK, pen_next, 0.0)
    scale = HEAD_DIM ** -0.5
    dn = (((1,), (1,)), ((), ()))
    nsb = tq // BLOCK
    for sb in range(nsb):
        r0 = sb * BLOCK
        k3 = kbuf[r0:r0 + 3 * BLOCK, :]
        v3 = vbuf[:, r0:r0 + 3 * BLOCK]
        for gi in range(KV_GROUP):
            sink = sink_ref[h * KV_GROUP + gi] * LOG2E
            q = q_ref[r0:r0 + BLOCK, gi * HEAD_DIM:(gi + 1) * HEAD_DIM].astype(BF16)
            s = lax.dot_general(k3, q, dn, preferred_element_type=F32) * (scale * LOG2E)
            s = s + bias_ref[gi * 3 * BLOCK:(gi + 1) * 3 * BLOCK, :]
            if sb == 0:
                s = s + col_prev
            if sb == nsb - 1:
                s = s + col_next
            m = jnp.maximum(jnp.max(s, axis=0, keepdims=True), sink)
            p = jnp.exp2(s - m)
            denom = jnp.sum(p, axis=0, keepdims=True) + jnp.exp2(sink - m)
            o_t = _dot(v3, p.astype(BF16)) * (1.0 / denom)
            o_ref[r0:r0 + BLOCK, gi * HEAD_DIM:(gi + 1) * HEAD_DIM] = o_t.T.astype(o_ref.dtype)


def _attention(proj, sink, tp, lp, ls):
    t = proj.shape[0]
    tq = math.gcd(ATTN_TQ, math.gcd(lp, ls))
    nblk = t // BLOCK
    bpt = tq // BLOCK
    slopes = 2.0 ** (-8.0 * jnp.arange(1, N_HEADS + 1, dtype=F32) / N_HEADS)
    qi = jnp.arange(BLOCK)[:, None]
    ki = jnp.arange(3 * BLOCK)[None, :]
    dist = jnp.abs(qi + BLOCK - ki)
    bias = jnp.where(dist[None] <= BLOCK, -(slopes * LOG2E)[:, None, None] * dist[None].astype(F32), -jnp.inf)
    bias = jnp.swapaxes(bias, 1, 2).reshape(N_HEADS * 3 * BLOCK, BLOCK)
    qc = OFF_Q // (KV_GROUP * HEAD_DIM)
    kc = OFF_K // HEAD_DIM
    vc = OFF_V // HEAD_DIM
    prev = lambda g: jnp.maximum(g * bpt - 1, 0)
    nxt = lambda g: jnp.minimum((g + 1) * bpt, nblk - 1)
    halo = (BLOCK, HEAD_DIM)
    cur = (tq, HEAD_DIM)
    return pl.pallas_call(
        functools.partial(_attn_kernel, tq=tq, tiles_p=tp // tq, per_p=lp // tq, per_s=ls // tq),
        grid=(t // tq, N_KV_HEADS),
        in_specs=[
            pl.BlockSpec((tq, KV_GROUP * HEAD_DIM), lambda g, h: (g, qc + h)),
            pl.BlockSpec(halo, lambda g, h: (prev(g), kc + h)),
            pl.BlockSpec(cur, lambda g, h: (g, kc + h)),
            pl.BlockSpec(halo, lambda g, h: (nxt(g), kc + h)),
            pl.BlockSpec(halo, lambda g, h: (prev(g), vc + h)),
            pl.BlockSpec(cur, lambda g, h: (g, vc + h)),
            pl.BlockSpec(halo, lambda g, h: (nxt(g), vc + h)),
            pl.BlockSpec((KV_GROUP * 3 * BLOCK, BLOCK), lambda g, h: (h, 0)),
            pl.BlockSpec(memory_space=pltpu.SMEM),
        ],
        out_specs=pl.BlockSpec((tq, KV_GROUP * HEAD_DIM), lambda g, h: (g, h)),
        out_shape=jax.ShapeDtypeStruct((t, ATTN_WIDTH), BF16),
        scratch_shapes=[pltpu.VMEM((tq + 2 * BLOCK, HEAD_DIM), BF16), pltpu.VMEM((HEAD_DIM, tq + 2 * BLOCK), BF16)],
        compiler_params=_cparams(("parallel", "parallel")),
        name="band_attention",
    )(proj, proj, proj, proj, proj, proj, proj, bias, sink.astype(F32))


def _filter_kernel(fvec_ref, w1_ref, b1_ref, w2_ref, b2_ref, w3_ref, b3_ref, fr_ref, w4_ref, delta_ref, o_ref,
                   *, seq, tl):
    r0 = pl.program_id(0) * tl
    row = (lax.broadcasted_iota(jnp.int32, (tl, 128), 0) + r0).astype(F32)
    lane = lax.broadcasted_iota(jnp.int32, (tl, 128), 1)
    t = row * (1.0 / (seq - 1))
    ang = (row * (2.0 * math.pi / seq)) * fvec_ref[...]
    z = jnp.where(lane == 0, t,
                  jnp.where(lane <= 16, jnp.cos(ang), jnp.where(lane <= 32, -jnp.sin(ang), 0.0)))
    def dot3(x, w_ref, cols=slice(None)):
        x_hi, x_lo = _split(x)
        w_hi, w_lo = w_ref[0, :, cols], w_ref[1, :, cols]
        return _dot(x_hi, w_hi) + (_dot(x_lo, w_hi) + _dot(x_hi, w_lo))

    h = jnp.sin(fr_ref[0:1, :] * (dot3(z, w1_ref) + b1_ref[...]))
    h = jnp.sin(fr_ref[1:2, :] * (dot3(h, w2_ref) + b2_ref[...]))
    h = jnp.sin(fr_ref[2:3, :] * (dot3(h, w3_ref) + b3_ref[...]))
    trow = (lax.broadcasted_iota(jnp.int32, (tl, HYENA_WIDTH), 0) + r0)
    decay = jnp.exp(-(trow.astype(F32) * (1.0 / (seq - 1))) * delta_ref[...])
    first = trow == 0
    for part in range(4):
        sl = slice(part * HYENA_WIDTH, (part + 1) * HYENA_WIDTH)
        v = dot3(h, w4_ref, sl) * decay
        if part >= 2:
            v = jnp.where(first, 0.0, v)
        o_ref[:, sl] = v


def _hyena_filters(seq, w1, b1, w2, b2, w3, b3, freq, w4):
    tl = min(seq, 512)
    bands = (FILTER_EMB - 1) // 2
    f = jnp.linspace(1e-4, bands - 1, bands, dtype=F32)
    fvec = jnp.zeros((1, 128), F32).at[0, 1:1 + bands].set(f).at[0, 1 + bands:1 + 2 * bands].set(f)
    w1p = jnp.zeros((128, FILTER_HIDDEN), F32).at[:FILTER_EMB].set(w1.astype(F32))
    max_decay = math.log(DECAY_TARGET) / DECAY_FAST_PCT
    min_decay = math.log(DECAY_TARGET) / DECAY_SLOW_PCT
    deltas = jnp.abs(jnp.linspace(min_decay, max_decay, HYENA_WIDTH, dtype=F32)).reshape(1, -1)
    planes = lambda w: jnp.stack(_split_const(w))
    args = [fvec, planes(w1p), b1.reshape(1, -1).astype(F32), planes(w2), b2.reshape(1, -1).astype(F32),
            planes(w3), b3.reshape(1, -1).astype(F32), freq.astype(F32), planes(w4), deltas]
    return pl.pallas_call(
        functools.partial(_filter_kernel, seq=seq, tl=tl),
        grid=(seq // tl,),
        in_specs=[_full_spec(a) for a in args],
        out_specs=pl.BlockSpec((tl, FILTER_OUT), lambda i: (i, 0)),
        out_shape=jax.ShapeDtypeStruct((seq, FILTER_OUT), F32),
        compiler_params=_cparams(("parallel",)),
        name="hyena_filter",
    )(*args)


def _dotp(mh_ref, ml_ref, x, passes):
    if passes == 1:
        return _dot(mh_ref[...], x.astype(BF16))
    return _dot3_left(mh_ref[...], ml_ref[...], x)


def _gather(ref, t2, n):
    return ref[pl.ds(t2, n, stride=PITCH), :]


def _fill_pitched(u_ref, p_ref, half, w_ref=None, b_ref=None):
    n2 = DFT_N2
    seq = half * n2
    rid = lax.broadcasted_iota(jnp.int32, (n2, LANES), 0)

    def step(t1, carry):
        r0 = pl.multiple_of(t1 * n2, n2)
        cur = u_ref[pl.ds(r0, n2), :]
        if w_ref is not None:
            before = u_ref[pl.ds(pl.multiple_of(jnp.maximum(r0 - 8, 0), 8), 8), :][7:8, :]
            after = u_ref[pl.ds(pl.multiple_of(jnp.minimum(r0 + n2, seq - 8), 8), 8), :][0:1, :]
            before = jnp.where(t1 > 0, before, 0.0)
            after = jnp.where(t1 < half - 1, after, 0.0)
            prev = jnp.where(rid == 0, before, pltpu.roll(cur, 1, 0))
            nxt = jnp.where(rid == n2 - 1, after, pltpu.roll(cur, n2 - 1, 0))
            cur = prev * w_ref[0:1, :] + cur * w_ref[1:2, :] + nxt * w_ref[2:3, :] + b_ref[...]
        p_ref[pl.ds(pl.multiple_of(t1 * PITCH, 8), n2), :] = cur
        return carry

    lax.fori_loop(0, half, step, 0)


def _for_t2_groups(body):
    def step(g, carry):
        body(g * T2_GROUP)
        return carry

    lax.fori_loop(0, DFT_N2 // T2_GROUP, step, 0, unroll=2)


def _outer_stage(p_ref, half, mh_ref, ml_ref, a_ref, nrows, passes):
    def body(base):
        xs = jnp.concatenate([_gather(p_ref, base + j, half) for j in range(T2_GROUP)], axis=1)
        y = _dotp(mh_ref, ml_ref, xs, passes)
        for j in range(T2_GROUP):
            a_ref[pl.ds(base + j, nrows, stride=PITCH), :] = y[:, j * LANES:(j + 1) * LANES]

    _for_t2_groups(body)


def _lane_block(x, j):
    return x[:, j * LANES:(j + 1) * LANES]


def _inner_blocks(a_ref, tw_ref, wstep_ref, nblocks, group_fn, store_fn):
    n2 = DFT_N2
    tw_ref[0:n2, :] = jnp.ones((n2, LANES), F32)
    tw_ref[n2:, :] = jnp.zeros((n2, LANES), F32)

    def run(k0, count):
        twr = tw_ref[0:n2, :]
        twi = tw_ref[n2:, :]
        wr = wstep_ref[0:n2, :]
        wi = wstep_ref[n2:, :]
        ks, rows, tws, blocks = [], [], [], []
        for j in range(count):
            k = k0 + j
            rr = pl.multiple_of(k * 2 * PITCH, 8)
            ri = pl.multiple_of(k * 2 * PITCH + PITCH, 8)
            ar = a_ref[pl.ds(rr, n2), :]
            ai = a_ref[pl.ds(ri, n2), :]
            ks.append(k)
            rows.append((rr, ri))
            tws.append((twr, twi))
            blocks.append(jnp.concatenate([ar * twr - ai * twi, ar * twi + ai * twr], axis=0))
            twr, twi = twr * wr - twi * wi, twr * wi + twi * wr
        vals = group_fn(ks, jnp.concatenate(blocks, axis=1), tws)
        for k, (rr, ri), val in zip(ks, rows, vals):
            store_fn(k, rr, ri, val)
        tw_ref[0:n2, :] = twr
        tw_ref[n2:, :] = twi

    def step(g, carry):
        run(g * INNER_GROUP, INNER_GROUP)
        return carry

    lax.fori_loop(0, nblocks // INNER_GROUP, step, 0)
    if nblocks % INNER_GROUP:
        run(nblocks - nblocks % INNER_GROUP, nblocks % INNER_GROUP)


def _hyena_conv_kernel(sig_ref, gate_ref, wsig_ref, bsig_ref, wgate_ref, bgate_ref, skip_ref, kf_ref,
                       f1h_ref, f1l_ref, g3h_ref, g3l_ref, gch_ref, gcl_ref, gbh_ref, gbl_ref, wstep_ref,
                       o_ref, a_ref, tw_ref, pv_ref, pg_ref, *, half, k1, k1p, sig_conv):
    n2 = DFT_N2
    if sig_conv:
        _fill_pitched(sig_ref, pv_ref, half, wsig_ref, bsig_ref)
    else:
        _fill_pitched(sig_ref, pv_ref, half)
    _fill_pitched(gate_ref, pg_ref, half, wgate_ref, bgate_ref)
    _outer_stage(pv_ref, half, f1h_ref, f1l_ref, a_ref, 2 * k1p, CONV_PASSES)

    def block(ks, xcat, tws):
        x = _dotp(gch_ref, gcl_ref, xcat, CONV_PASSES)
        prods = []
        for j, k in enumerate(ks):
            xr, xi = _lane_block(x[:n2], j), _lane_block(x[n2:], j)
            k0 = pl.multiple_of(k * 2 * n2, 2 * n2)
            kr = kf_ref[pl.ds(k0, n2), :].astype(F32)
            ki = kf_ref[pl.ds(k0 + n2, n2), :].astype(F32)
            prods.append(jnp.concatenate([xr * kr - xi * ki, xr * ki + xi * kr], axis=0))
        bb = _dotp(gbh_ref, gbl_ref, jnp.concatenate(prods, axis=1), CONV_PASSES)
        out = []
        for j, (twr, twi) in enumerate(tws):
            br, bi = _lane_block(bb[:n2], j), _lane_block(bb[n2:], j)
            out.append((br * twr + bi * twi, bi * twr - br * twi))
        return out

    def put(k, rr, ri, vals):
        a_ref[pl.ds(rr, n2), :] = vals[0]
        a_ref[pl.ds(ri, n2), :] = vals[1]

    _inner_blocks(a_ref, tw_ref, wstep_ref, k1, block, put)

    def finish(base):
        bcat = jnp.concatenate([_gather(a_ref, base + j, 2 * k1p) for j in range(T2_GROUP)], axis=1)
        y = _dotp(g3h_ref, g3l_ref, bcat, CONV_PASSES)
        for j in range(T2_GROUP):
            v = _gather(pv_ref, base + j, half)
            gate = _gather(pg_ref, base + j, half)
            pg_ref[pl.ds(base + j, half, stride=PITCH), :] = gate * (y[:, j * LANES:(j + 1) * LANES]
                                                                    + v * skip_ref[...])

    _for_t2_groups(finish)

    def emit(t1, carry):
        o_ref[pl.ds(pl.multiple_of(t1 * n2, n2), n2), :] = pg_ref[pl.ds(pl.multiple_of(t1 * PITCH, 8), n2), :]
        return carry

    lax.fori_loop(0, half, emit, 0)


def _hyena_spec_kernel(ff_ref, fb_ref, f2h_ref, f2l_ref, gch_ref, gcl_ref, wstep_ref, rev_ref, o_ref, a_ref, tw_ref,
                       pf_ref, *, half, k1, k1p):
    n2 = DFT_N2
    seq = half * n2
    _fill_pitched(ff_ref, pf_ref, half)
    rid = lax.broadcasted_iota(jnp.int32, (n2, LANES), 0)

    def reversed_slab(a, carry):
        s = half - 1 - a
        src = fb_ref[pl.ds(pl.multiple_of(s * n2, n2), n2), :]
        if SPEC_PASSES == 1:
            flipped = _dot(rev_ref[...], src.astype(BF16))
        else:
            src_hi, src_lo = _split(src)
            flipped = _dot(rev_ref[...], src_hi) + _dot(rev_ref[...], src_lo)
        head = fb_ref[pl.ds(pl.multiple_of(jnp.minimum((s + 1) * n2, seq - 8), 8), 8), :][0:1, :]
        head = jnp.where(a > 0, head, 0.0)
        pf_ref[pl.ds(pl.multiple_of((half + a) * PITCH, 8), n2), :] = jnp.where(rid == 0, head, flipped)
        return carry

    lax.fori_loop(0, half, reversed_slab, 0, unroll=8)
    _outer_stage(pf_ref, 2 * half, f2h_ref, f2l_ref, a_ref, 2 * k1p, SPEC_PASSES)

    def spectrum(ks, xcat, tws):
        x = _dotp(gch_ref, gcl_ref, xcat, SPEC_PASSES)
        return [_lane_block(x, j) for j in range(len(ks))]

    def put(k, rr, ri, x):
        o_ref[pl.ds(pl.multiple_of(k * 2 * n2, 2 * n2), 2 * n2), :] = x.astype(o_ref.dtype)

    _inner_blocks(a_ref, tw_ref, wstep_ref, k1, spectrum, put)


def _conv_tables(seq):
    n = 2 * seq
    n2 = DFT_N2
    n1 = n // n2
    k1 = n1 // 2 + 1
    k1p = -(-k1 // 8) * 8
    half = n1 // 2
    kk = jnp.arange(k1p, dtype=jnp.int32)
    valid = (kk < k1)
    t1 = jnp.arange(half, dtype=jnp.int32)
    c, s = _cs(kk[:, None] * t1[None, :], n1)
    vm = valid[:, None].astype(F32)
    f1 = jnp.stack([c * vm, -s * vm], axis=1).reshape(2 * k1p, half)
    wgt = jnp.where((kk == 0) | (kk == n1 // 2), 1.0, 2.0) * valid.astype(F32) / n
    g3 = jnp.stack([c * vm * wgt[:, None], -s * vm * wgt[:, None]], axis=1).reshape(2 * k1p, half).T
    j = jnp.arange(n2, dtype=jnp.int32)
    cr, cs_ = _cs(j[:, None] * j[None, :], n2)
    gc = jnp.concatenate([jnp.concatenate([cr, cs_], axis=1), jnp.concatenate([-cs_, cr], axis=1)], axis=0)
    gb = jnp.concatenate([jnp.concatenate([cr, -cs_], axis=1), jnp.concatenate([cs_, cr], axis=1)], axis=0)
    wr, ws = _cs(j, n)
    wstep = jnp.concatenate([jnp.broadcast_to(wr[:, None], (n2, LANES)),
                             jnp.broadcast_to(-ws[:, None], (n2, LANES))], axis=0)
    tall = jnp.arange(n1, dtype=jnp.int32)
    c2, s2 = _cs(kk[:, None] * tall[None, :], n1)
    f2 = jnp.stack([c2 * vm, -s2 * vm], axis=1).reshape(2 * k1p, n1)
    rev = ((j[:, None] + j[None, :]) == n2).astype(BF16)
    tabs = dict(n1=n1, k1=k1, k1p=k1p, half=half, wstep=wstep, rev=rev)
    for name, m in (("f1", f1), ("f2", f2), ("g3", g3), ("gc", gc), ("gb", gb)):
        tabs[name + "h"], tabs[name + "l"] = _split_const(m)
    return tabs


def _filter_spectrum(seq, tabs, fw):
    filt = _hyena_filters(seq, *fw)
    half, k1, k1p = tabs["half"], tabs["k1"], tabs["k1p"]
    n2 = DFT_N2
    nct = HYENA_WIDTH // LANES
    consts = [tabs[n] for n in ("f2h", "f2l", "gch", "gcl", "wstep", "rev")]
    return pl.pallas_call(
        functools.partial(_hyena_spec_kernel, half=half, k1=k1, k1p=k1p),
        grid=(2, nct),
        in_specs=[pl.BlockSpec((seq, LANES), lambda o, j: (0, o * nct + j)),
                  pl.BlockSpec((seq, LANES), lambda o, j: (0, (2 + o) * nct + j))] + [_full_spec(a) for a in consts],
        out_specs=pl.BlockSpec((None, None, k1 * 2 * n2, LANES), lambda o, j: (o, j, 0, 0)),
        out_shape=jax.ShapeDtypeStruct((2, nct, k1 * 2 * n2, LANES), BF16),
        scratch_shapes=[pltpu.VMEM((k1p * 2 * PITCH, LANES), F32), pltpu.VMEM((2 * n2, LANES), F32),
                        pltpu.VMEM((2 * half * PITCH, LANES), F32)],
        compiler_params=_cparams(("parallel", "parallel")),
        name="hyena_filter_spectrum",
    )(filt, filt, *consts)


def _skip_first_ref(kernel_fn, *refs, **kwargs):
    return kernel_fn(*refs[1:], **kwargs)


def _merged_out(kernel_fn, args, in_specs, into, total_rows, width, dtype):
    out_shape = jax.ShapeDtypeStruct((total_rows, width), dtype)
    if into is None:
        return kernel_fn, args, in_specs, out_shape, {}
    return (functools.partial(_skip_first_ref, kernel_fn), [into] + args,
            [pl.BlockSpec(memory_space=pl.ANY)] + in_specs, out_shape, {0: 0})


def _hyena_conv(sig, sig_cols, gate_cols, proj, row0, nbatch, seq, tabs, kf, order, short_w, short_b, skip,
                total_rows=None, into=None):
    assert row0 % seq == 0
    b0 = row0 // seq
    half, k1, k1p = tabs["half"], tabs["k1"], tabs["k1p"]
    n2 = DFT_N2
    nct = HYENA_WIDTH // LANES
    sig_conv = sig is None
    if sig_conv:
        sig_arr = proj
        sig_spec = pl.BlockSpec((seq, LANES), lambda j, b: (b0 + b, sig_cols + j))
    else:
        sig_arr = sig
        sig_spec = pl.BlockSpec((seq, LANES), lambda j, b: (b, j))
    sw = short_w.astype(F32)
    sb = short_b.reshape(1, -1).astype(F32)
    consts = [tabs[n] for n in ("f1h", "f1l", "g3h", "g3l", "gch", "gcl", "gbh", "gbl", "wstep")]
    args = [sig_arr, proj, sw, sb, sw, sb, skip.reshape(1, -1).astype(F32), kf] + consts
    in_specs = [
        sig_spec,
        pl.BlockSpec((seq, LANES), lambda j, b: (b0 + b, gate_cols + j)),
        pl.BlockSpec((3, LANES), lambda j, b: (0, sig_cols + j)),
        pl.BlockSpec((1, LANES), lambda j, b: (0, sig_cols + j)),
        pl.BlockSpec((3, LANES), lambda j, b: (0, gate_cols + j)),
        pl.BlockSpec((1, LANES), lambda j, b: (0, gate_cols + j)),
        pl.BlockSpec((1, LANES), lambda j, b: (0, j)),
        pl.BlockSpec((None, None, k1 * 2 * n2, LANES), lambda j, b: (order, j, 0, 0)),
    ] + [_full_spec(a) for a in consts]
    kern = functools.partial(_hyena_conv_kernel, half=half, k1=k1, k1p=k1p, sig_conv=sig_conv)
    out_b0 = 0 if total_rows is None else b0
    kern, args, in_specs, out_shape, aliases = _merged_out(
        kern, args, in_specs, into, total_rows or nbatch * seq, HYENA_WIDTH, F32)
    return pl.pallas_call(
        kern,
        grid=(nct, nbatch),
        in_specs=in_specs,
        out_specs=pl.BlockSpec((seq, LANES), lambda j, b: (out_b0 + b, j)),
        out_shape=out_shape,
        input_output_aliases=aliases,
        scratch_shapes=[pltpu.VMEM((k1p * 2 * PITCH, LANES), F32), pltpu.VMEM((2 * n2, LANES), F32),
                        pltpu.VMEM((half * PITCH, LANES), F32), pltpu.VMEM((half * PITCH, LANES), F32)],
        compiler_params=_cparams(("parallel", "arbitrary")),
        name="hyena_conv",
    )(*args)


def _hyena_batch(proj, row0, nbatch, seq, tabs, kf, short_w, short_b, skip, total_rows, into):
    nct = HYENA_WIDTH // LANES
    z = _hyena_conv(None, 2 * nct, 0, proj, row0, nbatch, seq, tabs, kf, 0, short_w, short_b, skip[0])
    return _hyena_conv(z, 2 * nct, nct, proj, row0, nbatch, seq, tabs, kf, 1, short_w, short_b, skip[1],
                       total_rows=total_rows, into=into)


def _fnet_kernel(u_ref, chan_ref, m1_ref, gri_ref, wstep_ref, o_ref, zr_ref, zi_ref, a_ref, tw_ref, *, n1):
    n2 = DFT_N2
    slabs = 4

    def chan(g, carry):
        x = u_ref[pl.ds(pl.multiple_of(g * slabs * n2, slabs * n2), slabs * n2), :].astype(BF16)
        z = _dot(x, chan_ref[...])
        for i in range(slabs):
            r = pl.multiple_of((g * slabs + i) * PITCH, 8)
            zr_ref[pl.ds(r, n2), :] = z[i * n2:(i + 1) * n2, :LANES]
            zi_ref[pl.ds(r, n2), :] = z[i * n2:(i + 1) * n2, LANES:]
        return carry

    lax.fori_loop(0, n1 // slabs, chan, 0, unroll=2)

    def outer(base):
        xs = jnp.concatenate(
            [jnp.concatenate([_gather(zr_ref, base + j, n1), _gather(zi_ref, base + j, n1)], axis=0)
             for j in range(T2_GROUP)], axis=1)
        y = _dot(m1_ref[...], xs.astype(BF16))
        for j in range(T2_GROUP):
            a_ref[pl.ds(base + j, 2 * n1, stride=PITCH), :] = y[:, j * LANES:(j + 1) * LANES]

    _for_t2_groups(outer)

    def real_part(ks, xcat, tws):
        y = _dot(gri_ref[...], xcat.astype(BF16))
        return [_lane_block(y, j) for j in range(len(ks))]

    def put(k, rr, ri, y):
        zr_ref[pl.ds(pl.multiple_of(k * PITCH, 8), n2), :] = y

    _inner_blocks(a_ref, tw_ref, wstep_ref, n1, real_part, put)

    def emit(k2, carry):
        o_ref[pl.ds(pl.multiple_of(k2 * n1, n1), n1), :] = _gather(zr_ref, k2, n1).astype(o_ref.dtype)
        return carry

    lax.fori_loop(0, n2, emit, 0, unroll=4)


def _fnet_tables(seq):
    n2 = DFT_N2
    n1 = seq // n2
    j = jnp.arange(HEAD_DIM, dtype=jnp.int32)
    c, s = _cs(j[:, None] * j[None, :], HEAD_DIM)
    scale = (seq * HEAD_DIM) ** -0.5
    chan = jnp.concatenate([c * scale, -s * scale], axis=1).astype(BF16)
    kk = jnp.arange(n1, dtype=jnp.int32)
    c1, s1 = _cs(kk[:, None] * kk[None, :], n1)
    m1 = jnp.stack([jnp.concatenate([c1, s1], axis=1), jnp.concatenate([-s1, c1], axis=1)], axis=1)
    m1 = m1.reshape(2 * n1, 2 * n1).astype(BF16)
    t2 = jnp.arange(n2, dtype=jnp.int32)
    cr, cs_ = _cs(t2[:, None] * t2[None, :], n2)
    gri = jnp.concatenate([cr, cs_], axis=1).astype(BF16)
    wr, ws = _cs(t2, seq)
    wstep = jnp.concatenate([jnp.broadcast_to(wr[:, None], (n2, LANES)),
                             jnp.broadcast_to(-ws[:, None], (n2, LANES))], axis=0)
    return dict(n1=n1, chan=chan, m1=m1, gri=gri, wstep=wstep)


def _fnet_batch(proj, row0, nbatch, seq, tabs, total_rows, into):
    assert row0 % seq == 0
    b0 = row0 // seq
    n1 = tabs["n1"]
    consts = [tabs[n] for n in ("chan", "m1", "gri", "wstep")]
    in_specs = ([pl.BlockSpec((seq, LANES), lambda j, b: (b0 + b, OFF_FNET // LANES + j))]
                + [_full_spec(a) for a in consts])
    kern, args, in_specs, out_shape, aliases = _merged_out(
        functools.partial(_fnet_kernel, n1=n1), [proj] + consts, in_specs, into, total_rows, FNET_WIDTH, BF16)
    return pl.pallas_call(
        kern,
        grid=(FNET_HEADS, nbatch),
        in_specs=in_specs,
        out_specs=pl.BlockSpec((seq, LANES), lambda j, b: (b0 + b, j)),
        out_shape=out_shape,
        input_output_aliases=aliases,
        scratch_shapes=[pltpu.VMEM((n1 * PITCH, LANES), F32), pltpu.VMEM((n1 * PITCH, LANES), F32),
                        pltpu.VMEM((2 * n1 * PITCH, LANES), F32), pltpu.VMEM((2 * DFT_N2, LANES), F32)],
        compiler_params=_cparams(("parallel", "parallel")),
        name="fnet_mixer",
    )(*args)


def _pick_tile(t, pref):
    while t % pref:
        pref //= 2
    return pref


def kernel(x_prompt, x_sample, ln0_g, ln0_b, w_in, short_w, short_b, filt_w1, filt_b1, filt_w2, filt_b2, filt_w3, filt_b3, filt_freq, filt_w4, hyena_skip, w_fnet, b_fnet, attn_sink, w_out, ln1_g, ln1_b, w_gate, w_up, w_down, ln2_g, ln2_b):
    bp, lp, _ = x_prompt.shape
    bs, ls, _ = x_sample.shape
    tp, ts = bp * lp, bs * ls
    batches = ((0, bp, lp), (tp, bs, ls))
    tm = _pick_tile(math.gcd(tp, ts), 1024)
    tln = _pick_tile(math.gcd(tp, ts), 512)

    conv_tabs = {seq: _conv_tables(seq) for seq in {lp, ls}}
    fnet_tabs = {seq: _fnet_tables(seq) for seq in {lp, ls}}

    resid, xb = _ln0(x_prompt.reshape(tp, D_MODEL), x_sample.reshape(ts, D_MODEL), ln0_g, ln0_b, tln // 2)
    w_in_b, w_out_b, w_down_b = w_in.astype(BF16), w_out.astype(BF16), w_down.astype(BF16)
    for l in range(DEPTH):
        fw = (filt_w1[l], filt_b1[l], filt_w2[l], filt_b2[l], filt_w3[l], filt_b3[l], filt_freq[l], filt_w4[l])

        proj = _matmul([xb], w_in_b, None, F32, tm, 1024, "in_proj", layer=l)

        kf = {seq: _filter_spectrum(seq, conv_tabs[seq], fw) for seq in {lp, ls}}
        y_h = y_f = None
        for r0, nb, seq in batches:
            y_h = _hyena_batch(proj, r0, nb, seq, conv_tabs[seq], kf[seq], short_w[l], short_b[l], hyena_skip[l],
                               tp + ts, y_h)
            y_f = _fnet_batch(proj, r0, nb, seq, fnet_tabs[seq], tp + ts, y_f)
        y_f = _matmul([y_f], w_fnet, b_fnet[l], BF16, tm, 1024, "fnet_linear", layer=l)
        y_a = _attention(proj, attn_sink[l], tp, lp, ls)

        y = _matmul([y_h, y_f, y_a], w_out_b, None, F32, tm, 512, "out_proj", resid=resid, layer=l)
        xb, mu, rs = _ln(y, ln1_g[l], ln1_b[l], tln)
        resid = (y, mu, rs, ln1_g[l], ln1_b[l])

        hid = _gate_up(xb, w_gate, w_up, l, _pick_tile(math.gcd(tp, ts), 2048), FF_TILE)
        y = _matmul([hid], w_down_b, None, F32, _pick_tile(tm, 512), 512, "ffn_down", resid=resid, layer=l)
        if l + 1 < DEPTH:
            xb, mu, rs = _ln(y, ln2_g[l], ln2_b[l], tln)
            resid = (y, mu, rs, ln2_g[l], ln2_b[l])
        else:
            y_p, y_s = _ln_final(y, ln2_g[l], ln2_b[l], tln // 2, tp)
    return y_p.reshape(bp, lp, D_MODEL), y_s.reshape(bs, ls, D_MODEL)
```

```python
import functools
import math

import jax
import jax.numpy as jnp
from jax import lax
from jax.experimental import pallas as pl
from jax.experimental.pallas import tpu as pltpu

F32 = jnp.float32
BF16 = jnp.bfloat16

D_MODEL = 4096
HEAD_DIM = 128
HYENA_WIDTH = 1024
FNET_WIDTH = 1024
ATTN_WIDTH = 2048
FNET_HEADS = 8
N_HEADS = 16
N_KV_HEADS = 4
KV_GROUP = 4
KV_WIDTH = 512
BLOCK = 128
HYENA_IN = 3 * HYENA_WIDTH
FILTER_EMB = 33
FILTER_HIDDEN = 64
FILTER_OUT = 4 * HYENA_WIDTH
OFF_FNET = HYENA_IN
OFF_Q = OFF_FNET + FNET_WIDTH
OFF_K = OFF_Q + ATTN_WIDTH
OFF_V = OFF_K + KV_WIDTH
IN_WIDTH = OFF_V + KV_WIDTH
D_FF = 11008
FF_TILE = 256
GATE_UP_PARTS = 2
DEPTH = 2
ALPHA = (2 * DEPTH) ** 0.25
LN_EPS = 1e-5
LOG2E = math.log2(math.e)
DECAY_FAST_PCT = 0.3
DECAY_SLOW_PCT = 1.5
DECAY_TARGET = 1e-2

DFT_N2 = 128
LANES = 128
T2_GROUP = 8
PITCH = 136
CONV_PASSES = 1
SPEC_PASSES = 1
INNER_GROUP = 8
ATTN_TQ = 2048
VMEM_LIMIT = 56 * 1024 * 1024


def _cparams(sem, vmem=VMEM_LIMIT):
    return pltpu.CompilerParams(dimension_semantics=sem, vmem_limit_bytes=vmem)


def _dot(a, b):
    return jnp.dot(a, b, preferred_element_type=F32)


def _split(x):
    hi = x.astype(BF16)
    lo = (x - hi.astype(F32)).astype(BF16)
    return hi, lo


def _dot3_left(m_hi, m_lo, x):
    x_hi, x_lo = _split(x)
    return _dot(m_hi, x_hi) + (_dot(m_hi, x_lo) + _dot(m_lo, x_hi))


def _split_const(m):
    m = m.astype(F32)
    hi = m.astype(BF16)
    lo = (m - hi.astype(F32)).astype(BF16)
    return hi, lo


def _cs(num, den):
    ang = (2.0 * math.pi / den) * (num % den).astype(F32)
    return jnp.cos(ang), jnp.sin(ang)


def _full_spec(a):
    return pl.BlockSpec(a.shape, lambda *_: (0,) * a.ndim)


def _ln_math(x, g, b):
    mu = jnp.mean(x, axis=-1, keepdims=True)
    xc = x - mu
    var = jnp.mean(xc * xc, axis=-1, keepdims=True)
    return xc * lax.rsqrt(var + LN_EPS) * g + b


def _ln0_kernel(xp_ref, xs_ref, g_ref, b_ref, of_ref, ob_ref, *, n_p):
    i = pl.program_id(0)

    def emit(x):
        y = _ln_math(x, g_ref[...], b_ref[...])
        of_ref[...] = y
        ob_ref[...] = y.astype(BF16)

    @pl.when(i < n_p)
    def _():
        emit(xp_ref[...])

    @pl.when(i >= n_p)
    def _():
        emit(xs_ref[...])


def _ln0(xp, xs, g, b, tm):
    tp, ts = xp.shape[0], xs.shape[0]
    n_p, n_s = tp // tm, ts // tm
    t = tp + ts
    return pl.pallas_call(
        functools.partial(_ln0_kernel, n_p=n_p),
        grid=(n_p + n_s,),
        in_specs=[
            pl.BlockSpec((tm, D_MODEL), lambda i: (jnp.minimum(i, n_p - 1), 0)),
            pl.BlockSpec((tm, D_MODEL), lambda i: (jnp.maximum(i - n_p, 0), 0)),
            pl.BlockSpec((1, D_MODEL), lambda i: (0, 0)),
            pl.BlockSpec((1, D_MODEL), lambda i: (0, 0)),
        ],
        out_specs=[
            pl.BlockSpec((tm, D_MODEL), lambda i: (i, 0)),
            pl.BlockSpec((tm, D_MODEL), lambda i: (i, 0)),
        ],
        out_shape=[jax.ShapeDtypeStruct((t, D_MODEL), F32), jax.ShapeDtypeStruct((t, D_MODEL), BF16)],
        compiler_params=_cparams(("parallel",)),
        name="ln0",
    )(xp, xs, g.reshape(1, -1), b.reshape(1, -1))


def _ln_kernel(y_ref, g_ref, b_ref, ob_ref, mu_ref, rs_ref):
    y = y_ref[...]
    mu = jnp.mean(y, axis=-1, keepdims=True)
    xc = y - mu
    rs = lax.rsqrt(jnp.mean(xc * xc, axis=-1, keepdims=True) + LN_EPS)
    ob_ref[...] = (xc * rs * g_ref[...] + b_ref[...]).astype(BF16)
    mu_ref[...] = jnp.broadcast_to(mu, mu_ref.shape)
    rs_ref[...] = jnp.broadcast_to(rs, rs_ref.shape)


def _ln(y, g, b, tm):
    t = y.shape[0]
    row = pl.BlockSpec((tm, D_MODEL), lambda i: (i, 0))
    vec = pl.BlockSpec((1, D_MODEL), lambda i: (0, 0))
    stat = pl.BlockSpec((tm, LANES), lambda i: (i, 0))
    return pl.pallas_call(
        _ln_kernel,
        grid=(t // tm,),
        in_specs=[row, vec, vec],
        out_specs=[row, stat, stat],
        out_shape=[jax.ShapeDtypeStruct((t, D_MODEL), BF16), jax.ShapeDtypeStruct((t, LANES), F32),
                   jax.ShapeDtypeStruct((t, LANES), F32)],
        compiler_params=_cparams(("parallel",)),
        name="layer_norm",
    )(y, g.reshape(1, -1), b.reshape(1, -1))


def _ln_final_kernel(y_ref, g_ref, b_ref, op_ref, os_ref, *, n_p):
    i = pl.program_id(0)
    y = _ln_math(y_ref[...], g_ref[...], b_ref[...])

    @pl.when(i < n_p)
    def _():
        op_ref[...] = y

    @pl.when(i >= n_p)
    def _():
        os_ref[...] = y


def _ln_final(y, g, b, tm, tp):
    t = y.shape[0]
    n_p = tp // tm
    row = pl.BlockSpec((tm, D_MODEL), lambda i: (i, 0))
    vec = pl.BlockSpec((1, D_MODEL), lambda i: (0, 0))
    return pl.pallas_call(
        functools.partial(_ln_final_kernel, n_p=n_p),
        grid=(t // tm,),
        in_specs=[row, vec, vec],
        out_specs=[
            pl.BlockSpec((tm, D_MODEL), lambda i: (jnp.minimum(i, n_p - 1), 0)),
            pl.BlockSpec((tm, D_MODEL), lambda i: (jnp.maximum(i - n_p, 0), 0)),
        ],
        out_shape=[jax.ShapeDtypeStruct((tp, D_MODEL), F32), jax.ShapeDtypeStruct((t - tp, D_MODEL), F32)],
        compiler_params=_cparams(("arbitrary",)),
        name="layer_norm_final",
    )(y, g.reshape(1, -1), b.reshape(1, -1))


def _mm_kernel(*refs, widths, has_bias, resid_mode):
    n_a = len(widths)
    a_refs = refs[:n_a]
    w_ref = refs[n_a]
    o_ref = refs[-1]
    nxt = n_a + 1
    acc = None
    off = 0
    for a_ref, wd in zip(a_refs, widths):
        part = _dot(a_ref[...].astype(BF16), w_ref[off:off + wd, :].astype(BF16))
        acc = part if acc is None else acc + part
        off += wd
    if has_bias:
        acc = acc + refs[nxt][...]
        nxt += 1
    if resid_mode == "plain":
        acc = ALPHA * refs[nxt][...] + acc
    elif resid_mode == "normalise":
        y_ref, mu_ref, rs_ref, g_ref, b_ref = refs[nxt:nxt + 5]
        reps = o_ref.shape[1] // LANES
        mu = jnp.concatenate([mu_ref[...]] * reps, axis=1)
        rs = jnp.concatenate([rs_ref[...]] * reps, axis=1)
        acc = ALPHA * ((y_ref[...] - mu) * rs * g_ref[...] + b_ref[...]) + acc
    o_ref[...] = acc.astype(o_ref.dtype)


def _matmul(a_list, w, bias, out_dtype, tm, tn, name, resid=None, layer=None, rows_inner=False):
    t = a_list[0].shape[0]
    k, n = w.shape[-2:]
    widths = tuple(a.shape[1] for a in a_list)
    assert sum(widths) == k and t % tm == 0 and n % tn == 0

    def spec(shape, index_map):
        if rows_inner:
            return pl.BlockSpec(shape, lambda j, i: index_map(i, j))
        return pl.BlockSpec(shape, index_map)

    in_specs = [spec((tm, wd), lambda i, j: (i, 0)) for wd in widths]
    if layer is None:
        in_specs.append(spec((k, tn), lambda i, j: (0, j)))
    else:
        in_specs.append(spec((None, k, tn), lambda i, j: (layer, 0, j)))
    args = list(a_list) + [w]
    if bias is not None:
        in_specs.append(spec((1, tn), lambda i, j: (0, j)))
        args.append(bias.reshape(1, n).astype(F32))
    resid_mode = None
    if isinstance(resid, tuple):
        resid_mode = "normalise"
        y, mu, rs, g, b = resid
        stat = spec((tm, LANES), lambda i, j: (i, 0))
        vec = spec((1, tn), lambda i, j: (0, j))
        in_specs += [spec((tm, tn), lambda i, j: (i, j)), stat, stat, vec, vec]
        args += [y, mu, rs, g.reshape(1, n).astype(F32), b.reshape(1, n).astype(F32)]
    elif resid is not None:
        resid_mode = "plain"
        in_specs.append(spec((tm, tn), lambda i, j: (i, j)))
        args.append(resid)
    return pl.pallas_call(
        functools.partial(_mm_kernel, widths=widths, has_bias=bias is not None, resid_mode=resid_mode),
        grid=(n // tn, t // tm) if rows_inner else (t // tm, n // tn),
        in_specs=in_specs,
        out_specs=spec((tm, tn), lambda i, j: (i, j)),
        out_shape=jax.ShapeDtypeStruct((t, n), out_dtype),
        compiler_params=_cparams(("parallel", "parallel")),
        name=name,
    )(*args)


def _gate_up_kernel(x_ref, wg_ref, wu_ref, o_ref):
    wg = wg_ref[...].astype(BF16)
    wu = wu_ref[...].astype(BF16)
    rows = x_ref.shape[0] // GATE_UP_PARTS
    for r0 in range(0, x_ref.shape[0], rows):
        x = x_ref[r0:r0 + rows, :]
        g = _dot(x, wg)
        u = _dot(x, wu)
        o_ref[r0:r0 + rows, :] = (g * (1.0 / (1.0 + jnp.exp(-g))) * u).astype(o_ref.dtype)


def _gate_up(x, wg, wu, layer, tm, tn):
    t, k = x.shape
    n = wg.shape[-1]
    return pl.pallas_call(
        _gate_up_kernel,
        grid=(t // tm, n // tn),
        in_specs=[
            pl.BlockSpec((tm, k), lambda i, j: (i, 0)),
            pl.BlockSpec((None, k, tn), lambda i, j: (layer, 0, j)),
            pl.BlockSpec((None, k, tn), lambda i, j: (layer, 0, j)),
        ],
        out_specs=pl.BlockSpec((tm, tn), lambda i, j: (i, j)),
        out_shape=jax.ShapeDtypeStruct((t, n), BF16),
        compiler_params=_cparams(("parallel", "parallel")),
        name="ffn_gate_up",
    )(x, wg, wu)


def _attn_kernel(q_ref, kp_ref, kc_ref, kn_ref, vp_ref, vc_ref, vn_ref, bias_ref, sink_ref, o_ref, kbuf, vbuf,
                 *, tq, tiles_p, per_p, per_s):
    g = pl.program_id(0)
    h = pl.program_id(1)
    in_p = g < tiles_p
    n_loc = jnp.where(in_p, g % per_p, (g - tiles_p) % per_s)
    n_seq = jnp.where(in_p, per_p, per_s)
    pen_prev = jnp.where(n_loc > 0, 0.0, -jnp.inf)
    pen_next = jnp.where(n_loc < n_seq - 1, 0.0, -jnp.inf)

    kbuf[0:BLOCK, :] = kp_ref[...].astype(BF16)
    kbuf[BLOCK:BLOCK + tq, :] = kc_ref[...].astype(BF16)
    kbuf[BLOCK + tq:, :] = kn_ref[...].astype(BF16)
    vbuf[:, 0:BLOCK] = vp_ref[...].T.astype(BF16)
    for c0 in range(0, tq, BLOCK):
        vbuf[:, BLOCK + c0:2 * BLOCK + c0] = vc_ref[c0:c0 + BLOCK, :].T.astype(BF16)
    vbuf[:, BLOCK + tq:] = vn_ref[...].T.astype(BF16)

    key = lax.broadcasted_iota(jnp.int32, (3 * BLOCK, BLOCK), 0)
    col_prev = jnp.where(key < BLOCK, pen_prev, 0.0)
    col_next = jnp.where(key >= 2 * BLOCK, pen_next, 0.0)
    scale = HEAD_DIM ** -0.5
    dn = (((1,), (1,)), ((), ()))
    nsb = tq // BLOCK
    for sb in range(nsb):
        r0 = sb * BLOCK
        k3 = kbuf[r0:r0 + 3 * BLOCK, :]
        v3 = vbuf[:, r0:r0 + 3 * BLOCK]
        for gi in range(KV_GROUP):
            sink = sink_ref[h * KV_GROUP + gi] * LOG2E
            q = q_ref[r0:r0 + BLOCK, gi * HEAD_DIM:(gi + 1) * HEAD_DIM].astype(BF16)
            s = lax.dot_general(k3, q, dn, preferred_element_type=F32) * (scale * LOG2E)
            s = s + bias_ref[gi * 3 * BLOCK:(gi + 1) * 3 * BLOCK, :]
            if sb == 0:
                s = s + col_prev
            if sb == nsb - 1:
                s = s + col_next
            m = jnp.maximum(jnp.max(s, axis=0, keepdims=True), sink)
            p = jnp.exp2(s - m)
            denom = jnp.sum(p, axis=0, keepdims=True) + jnp.exp2(sink - m)
            o_t = _dot(v3, p.astype(BF16)) * (1.0 / denom)
            o_ref[r0:r0 + BLOCK, gi * HEAD_DIM:(gi + 1) * HEAD_DIM] = o_t.T.astype(o_ref.dtype)


def _attention(proj, sink, tp, lp, ls):
    t = proj.shape[0]
    tq = math.gcd(ATTN_TQ, math.gcd(lp, ls))
    nblk = t // BLOCK
    bpt = tq // BLOCK
    slopes = 2.0 ** (-8.0 * jnp.arange(1, N_HEADS + 1, dtype=F32) / N_HEADS)
    qi = jnp.arange(BLOCK)[:, None]
    ki = jnp.arange(3 * BLOCK)[None, :]
    dist = jnp.abs(qi + BLOCK - ki)
    bias = jnp.where(dist[None] <= BLOCK, -(slopes * LOG2E)[:, None, None] * dist[None].astype(F32), -jnp.inf)
    bias = jnp.swapaxes(bias, 1, 2).reshape(N_HEADS * 3 * BLOCK, BLOCK)
    qc = OFF_Q // (KV_GROUP * HEAD_DIM)
    kc = OFF_K // HEAD_DIM
    vc = OFF_V // HEAD_DIM
    prev = lambda g: jnp.maximum(g * bpt - 1, 0)
    nxt = lambda g: jnp.minimum((g + 1) * bpt, nblk - 1)
    halo = (BLOCK, HEAD_DIM)
    cur = (tq, HEAD_DIM)
    return pl.pallas_call(
        functools.partial(_attn_kernel, tq=tq, tiles_p=tp // tq, per_p=lp // tq, per_s=ls // tq),
        grid=(t // tq, N_KV_HEADS),
        in_specs=[
            pl.BlockSpec((tq, KV_GROUP * HEAD_DIM), lambda g, h: (g, qc + h)),
            pl.BlockSpec(halo, lambda g, h: (prev(g), kc + h)),
            pl.BlockSpec(cur, lambda g, h: (g, kc + h)),
            pl.BlockSpec(halo, lambda g, h: (nxt(g), kc + h)),
            pl.BlockSpec(halo, lambda g, h: (prev(g), vc + h)),
            pl.BlockSpec(cur, lambda g, h: (g, vc + h)),
            pl.BlockSpec(halo, lambda g, h: (nxt(g), vc + h)),
            pl.BlockSpec((KV_GROUP * 3 * BLOCK, BLOCK), lambda g, h: (h, 0)),
            pl.BlockSpec(memory_space=pltpu.SMEM),
        ],
        out_specs=pl.BlockSpec((tq, KV_GROUP * HEAD_DIM), lambda g, h: (g, h)),
        out_shape=jax.ShapeDtypeStruct((t, ATTN_WIDTH), BF16),
        scratch_shapes=[pltpu.VMEM((tq + 2 * BLOCK, HEAD_DIM), BF16), pltpu.VMEM((HEAD_DIM, tq + 2 * BLOCK), BF16)],
        compiler_params=_cparams(("parallel", "parallel")),
        name="band_attention",
    )(proj, proj, proj, proj, proj, proj, proj, bias, sink.astype(F32))


def _filter_kernel(fvec_ref, w1_ref, b1_ref, w2_ref, b2_ref, w3_ref, b3_ref, fr_ref, w4_ref, delta_ref, o_ref,
                   *, seq, tl):
    r0 = pl.program_id(0) * tl
    row = (lax.broadcasted_iota(jnp.int32, (tl, 128), 0) + r0).astype(F32)
    lane = lax.broadcasted_iota(jnp.int32, (tl, 128), 1)
    t = row * (1.0 / (seq - 1))
    ang = (row * (2.0 * math.pi / seq)) * fvec_ref[...]
    z = jnp.where(lane == 0, t,
                  jnp.where(lane <= 16, jnp.cos(ang), jnp.where(lane <= 32, -jnp.sin(ang), 0.0)))
    def dot3(x, w_ref, cols=slice(None)):
        x_hi, x_lo = _split(x)
        w_hi, w_lo = w_ref[0, :, cols], w_ref[1, :, cols]
        return _dot(x_hi, w_hi) + (_dot(x_lo, w_hi) + _dot(x_hi, w_lo))

    h = jnp.sin(fr_ref[0:1, :] * (dot3(z, w1_ref) + b1_ref[...]))
    h = jnp.sin(fr_ref[1:2, :] * (dot3(h, w2_ref) + b2_ref[...]))
    h = jnp.sin(fr_ref[2:3, :] * (dot3(h, w3_ref) + b3_ref[...]))
    trow = (lax.broadcasted_iota(jnp.int32, (tl, HYENA_WIDTH), 0) + r0)
    decay = jnp.exp(-(trow.astype(F32) * (1.0 / (seq - 1))) * delta_ref[...])
    first = trow == 0
    for part in range(4):
        sl = slice(part * HYENA_WIDTH, (part + 1) * HYENA_WIDTH)
        v = dot3(h, w4_ref, sl) * decay
        if part >= 2:
            v = jnp.where(first, 0.0, v)
        o_ref[:, sl] = v


def _hyena_filters(seq, w1, b1, w2, b2, w3, b3, freq, w4):
    tl = min(seq, 512)
    bands = (FILTER_EMB - 1) // 2
    f = jnp.linspace(1e-4, bands - 1, bands, dtype=F32)
    fvec = jnp.zeros((1, 128), F32).at[0, 1:1 + bands].set(f).at[0, 1 + bands:1 + 2 * bands].set(f)
    w1p = jnp.zeros((128, FILTER_HIDDEN), F32).at[:FILTER_EMB].set(w1.astype(F32))
    max_decay = math.log(DECAY_TARGET) / DECAY_FAST_PCT
    min_decay = math.log(DECAY_TARGET) / DECAY_SLOW_PCT
    deltas = jnp.abs(jnp.linspace(min_decay, max_decay, HYENA_WIDTH, dtype=F32)).reshape(1, -1)
    planes = lambda w: jnp.stack(_split_const(w))
    args = [fvec, planes(w1p), b1.reshape(1, -1).astype(F32), planes(w2), b2.reshape(1, -1).astype(F32),
            planes(w3), b3.reshape(1, -1).astype(F32), freq.astype(F32), planes(w4), deltas]
    return pl.pallas_call(
        functools.partial(_filter_kernel, seq=seq, tl=tl),
        grid=(seq // tl,),
        in_specs=[_full_spec(a) for a in args],
        out_specs=pl.BlockSpec((tl, FILTER_OUT), lambda i: (i, 0)),
        out_shape=jax.ShapeDtypeStruct((seq, FILTER_OUT), F32),
        compiler_params=_cparams(("parallel",)),
        name="hyena_filter",
    )(*args)


def _dotp(mh_ref, ml_ref, x, passes):
    if passes == 1:
        return _dot(mh_ref[...], x.astype(BF16))
    return _dot3_left(mh_ref[...], ml_ref[...], x)


def _gather(ref, t2, n):
    return ref[pl.ds(t2, n, stride=PITCH), :]


def _fill_pitched(u_ref, p_ref, half, w_ref=None, b_ref=None):
    n2 = DFT_N2
    seq = half * n2
    rid = lax.broadcasted_iota(jnp.int32, (n2, LANES), 0)

    def step(t1, carry):
        r0 = pl.multiple_of(t1 * n2, n2)
        cur = u_ref[pl.ds(r0, n2), :]
        if w_ref is not None:
            before = u_ref[pl.ds(pl.multiple_of(jnp.maximum(r0 - 8, 0), 8), 8), :][7:8, :]
            after = u_ref[pl.ds(pl.multiple_of(jnp.minimum(r0 + n2, seq - 8), 8), 8), :][0:1, :]
            before = jnp.where(t1 > 0, before, 0.0)
            after = jnp.where(t1 < half - 1, after, 0.0)
            prev = jnp.where(rid == 0, before, pltpu.roll(cur, 1, 0))
            nxt = jnp.where(rid == n2 - 1, after, pltpu.roll(cur, n2 - 1, 0))
            cur = prev * w_ref[0:1, :] + cur * w_ref[1:2, :] + nxt * w_ref[2:3, :] + b_ref[...]
        p_ref[pl.ds(pl.multiple_of(t1 * PITCH, 8), n2), :] = cur
        return carry

    lax.fori_loop(0, half, step, 0)


def _for_t2_groups(body):
    def step(g, carry):
        body(g * T2_GROUP)
        return carry

    lax.fori_loop(0, DFT_N2 // T2_GROUP, step, 0, unroll=2)


def _outer_stage(p_ref, half, mh_ref, ml_ref, a_ref, nrows, passes):
    def body(base):
        xs = jnp.concatenate([_gather(p_ref, base + j, half) for j in range(T2_GROUP)], axis=1)
        y = _dotp(mh_ref, ml_ref, xs, passes)
        for j in range(T2_GROUP):
            a_ref[pl.ds(base + j, nrows, stride=PITCH), :] = y[:, j * LANES:(j + 1) * LANES]

    _for_t2_groups(body)


def _lane_block(x, j):
    return x[:, j * LANES:(j + 1) * LANES]


def _inner_blocks(a_ref, tw_ref, wstep_ref, nblocks, group_fn, store_fn):
    n2 = DFT_N2
    tw_ref[0:n2, :] = jnp.ones((n2, LANES), F32)
    tw_ref[n2:, :] = jnp.zeros((n2, LANES), F32)

    def run(k0, count):
        twr = tw_ref[0:n2, :]
        twi = tw_ref[n2:, :]
        wr = wstep_ref[0:n2, :]
        wi = wstep_ref[n2:, :]
        ks, rows, tws, blocks = [], [], [], []
        for j in range(count):
            k = k0 + j
            rr = pl.multiple_of(k * 2 * PITCH, 8)
            ri = pl.multiple_of(k * 2 * PITCH + PITCH, 8)
            ar = a_ref[pl.ds(rr, n2), :]
            ai = a_ref[pl.ds(ri, n2), :]
            ks.append(k)
            rows.append((rr, ri))
            tws.append((twr, twi))
            blocks.append(jnp.concatenate([ar * twr - ai * twi, ar * twi + ai * twr], axis=0))
            twr, twi = twr * wr - twi * wi, twr * wi + twi * wr
        vals = group_fn(ks, jnp.concatenate(blocks, axis=1), tws)
        for k, (rr, ri), val in zip(ks, rows, vals):
            store_fn(k, rr, ri, val)
        tw_ref[0:n2, :] = twr
        tw_ref[n2:, :] = twi

    def step(g, carry):
        run(g * INNER_GROUP, INNER_GROUP)
        return carry

    lax.fori_loop(0, nblocks // INNER_GROUP, step, 0)
    if nblocks % INNER_GROUP:
        run(nblocks - nblocks % INNER_GROUP, nblocks % INNER_GROUP)


def _hyena_conv_kernel(sig_ref, gate_ref, wsig_ref, bsig_ref, wgate_ref, bgate_ref, skip_ref, kf_ref,
                       f1h_ref, f1l_ref, g3h_ref, g3l_ref, gch_ref, gcl_ref, gbh_ref, gbl_ref, wstep_ref,
                       o_ref, a_ref, tw_ref, pv_ref, pg_ref, *, half, k1, k1p, sig_conv):
    n2 = DFT_N2
    if sig_conv:
        _fill_pitched(sig_ref, pv_ref, half, wsig_ref, bsig_ref)
    else:
        _fill_pitched(sig_ref, pv_ref, half)
    _fill_pitched(gate_ref, pg_ref, half, wgate_ref, bgate_ref)
    _outer_stage(pv_ref, half, f1h_ref, f1l_ref, a_ref, 2 * k1p, CONV_PASSES)

    def block(ks, xcat, tws):
        x = _dotp(gch_ref, gcl_ref, xcat, CONV_PASSES)
        prods = []
        for j, k in enumerate(ks):
            xr, xi = _lane_block(x[:n2], j), _lane_block(x[n2:], j)
            k0 = pl.multiple_of(k * 2 * n2, 2 * n2)
            kr = kf_ref[pl.ds(k0, n2), :].astype(F32)
            ki = kf_ref[pl.ds(k0 + n2, n2), :].astype(F32)
            prods.append(jnp.concatenate([xr * kr - xi * ki, xr * ki + xi * kr], axis=0))
        bb = _dotp(gbh_ref, gbl_ref, jnp.concatenate(prods, axis=1), CONV_PASSES)
        out = []
        for j, (twr, twi) in enumerate(tws):
            br, bi = _lane_block(bb[:n2], j), _lane_block(bb[n2:], j)
            out.append((br * twr + bi * twi, bi * twr - br * twi))
        return out

    def put(k, rr, ri, vals):
        a_ref[pl.ds(rr, n2), :] = vals[0]
        a_ref[pl.ds(ri, n2), :] = vals[1]

    _inner_blocks(a_ref, tw_ref, wstep_ref, k1, block, put)

    def finish(base):
        bcat = jnp.concatenate([_gather(a_ref, base + j, 2 * k1p) for j in range(T2_GROUP)], axis=1)
        y = _dotp(g3h_ref, g3l_ref, bcat, CONV_PASSES)
        for j in range(T2_GROUP):
            v = _gather(pv_ref, base + j, half)
            gate = _gather(pg_ref, base + j, half)
            pg_ref[pl.ds(base + j, half, stride=PITCH), :] = gate * (y[:, j * LANES:(j + 1) * LANES]
                                                                    + v * skip_ref[...])

    _for_t2_groups(finish)

    def emit(t1, carry):
        o_ref[pl.ds(pl.multiple_of(t1 * n2, n2), n2), :] = pg_ref[pl.ds(pl.multiple_of(t1 * PITCH, 8), n2), :]
        return carry

    lax.fori_loop(0, half, emit, 0)


def _hyena_spec_kernel(ff_ref, fb_ref, f2h_ref, f2l_ref, gch_ref, gcl_ref, wstep_ref, rev_ref, o_ref, a_ref, tw_ref,
                       pf_ref, *, half, k1, k1p):
    n2 = DFT_N2
    seq = half * n2
    _fill_pitched(ff_ref, pf_ref, half)
    rid = lax.broadcasted_iota(jnp.int32, (n2, LANES), 0)

    def reversed_slab(a, carry):
        s = half - 1 - a
        src = fb_ref[pl.ds(pl.multiple_of(s * n2, n2), n2), :]
        if SPEC_PASSES == 1:
            flipped = _dot(rev_ref[...], src.astype(BF16))
        else:
            src_hi, src_lo = _split(src)
            flipped = _dot(rev_ref[...], src_hi) + _dot(rev_ref[...], src_lo)
        head = fb_ref[pl.ds(pl.multiple_of(jnp.minimum((s + 1) * n2, seq - 8), 8), 8), :][0:1, :]
        head = jnp.where(a > 0, head, 0.0)
        pf_ref[pl.ds(pl.multiple_of((half + a) * PITCH, 8), n2), :] = jnp.where(rid == 0, head, flipped)
        return carry

    lax.fori_loop(0, half, reversed_slab, 0, unroll=8)
    _outer_stage(pf_ref, 2 * half, f2h_ref, f2l_ref, a_ref, 2 * k1p, SPEC_PASSES)

    def spectrum(ks, xcat, tws):
        x = _dotp(gch_ref, gcl_ref, xcat, SPEC_PASSES)
        return [_lane_block(x, j) for j in range(len(ks))]

    def put(k, rr, ri, x):
        o_ref[pl.ds(pl.multiple_of(k * 2 * n2, 2 * n2), 2 * n2), :] = x.astype(o_ref.dtype)

    _inner_blocks(a_ref, tw_ref, wstep_ref, k1, spectrum, put)


def _conv_tables(seq):
    n = 2 * seq
    n2 = DFT_N2
    n1 = n // n2
    k1 = n1 // 2 + 1
    k1p = -(-k1 // 8) * 8
    half = n1 // 2
    kk = jnp.arange(k1p, dtype=jnp.int32)
    valid = (kk < k1)
    t1 = jnp.arange(half, dtype=jnp.int32)
    c, s = _cs(kk[:, None] * t1[None, :], n1)
    vm = valid[:, None].astype(F32)
    f1 = jnp.stack([c * vm, -s * vm], axis=1).reshape(2 * k1p, half)
    wgt = jnp.where((kk == 0) | (kk == n1 // 2), 1.0, 2.0) * valid.astype(F32) / n
    g3 = jnp.stack([c * vm * wgt[:, None], -s * vm * wgt[:, None]], axis=1).reshape(2 * k1p, half).T
    j = jnp.arange(n2, dtype=jnp.int32)
    cr, cs_ = _cs(j[:, None] * j[None, :], n2)
    gc = jnp.concatenate([jnp.concatenate([cr, cs_], axis=1), jnp.concatenate([-cs_, cr], axis=1)], axis=0)
    gb = jnp.concatenate([jnp.concatenate([cr, -cs_], axis=1), jnp.concatenate([cs_, cr], axis=1)], axis=0)
    wr, ws = _cs(j, n)
    wstep = jnp.concatenate([jnp.broadcast_to(wr[:, None], (n2, LANES)),
                             jnp.broadcast_to(-ws[:, None], (n2, LANES))], axis=0)
    tall = jnp.arange(n1, dtype=jnp.int32)
    c2, s2 = _cs(kk[:, None] * tall[None, :], n1)
    f2 = jnp.stack([c2 * vm, -s2 * vm], axis=1).reshape(2 * k1p, n1)
    rev = ((j[:, None] + j[None, :]) == n2).astype(BF16)
    tabs = dict(n1=n1, k1=k1, k1p=k1p, half=half, wstep=wstep, rev=rev)
    for name, m in (("f1", f1), ("f2", f2), ("g3", g3), ("gc", gc), ("gb", gb)):
        tabs[name + "h"], tabs[name + "l"] = _split_const(m)
    return tabs


def _filter_spectrum(seq, tabs, fw):
    filt = _hyena_filters(seq, *fw)
    half, k1, k1p = tabs["half"], tabs["k1"], tabs["k1p"]
    n2 = DFT_N2
    nct = HYENA_WIDTH // LANES
    consts = [tabs[n] for n in ("f2h", "f2l", "gch", "gcl", "wstep", "rev")]
    return pl.pallas_call(
        functools.partial(_hyena_spec_kernel, half=half, k1=k1, k1p=k1p),
        grid=(2, nct),
        in_specs=[pl.BlockSpec((seq, LANES), lambda o, j: (0, o * nct + j)),
                  pl.BlockSpec((seq, LANES), lambda o, j: (0, (2 + o) * nct + j))] + [_full_spec(a) for a in consts],
        out_specs=pl.BlockSpec((None, None, k1 * 2 * n2, LANES), lambda o, j: (o, j, 0, 0)),
        out_shape=jax.ShapeDtypeStruct((2, nct, k1 * 2 * n2, LANES), BF16),
        scratch_shapes=[pltpu.VMEM((k1p * 2 * PITCH, LANES), F32), pltpu.VMEM((2 * n2, LANES), F32),
                        pltpu.VMEM((2 * half * PITCH, LANES), F32)],
        compiler_params=_cparams(("parallel", "parallel")),
        name="hyena_filter_spectrum",
    )(filt, filt, *consts)


def _skip_first_ref(kernel_fn, *refs, **kwargs):
    return kernel_fn(*refs[1:], **kwargs)


def _merged_out(kernel_fn, args, in_specs, into, total_rows, width, dtype):
    out_shape = jax.ShapeDtypeStruct((total_rows, width), dtype)
    if into is None:
        return kernel_fn, args, in_specs, out_shape, {}
    return (functools.partial(_skip_first_ref, kernel_fn), [into] + args,
            [pl.BlockSpec(memory_space=pl.ANY)] + in_specs, out_shape, {0: 0})


def _hyena_conv(sig, sig_cols, gate_cols, proj, row0, nbatch, seq, tabs, kf, order, short_w, short_b, skip,
                total_rows=None, into=None):
    assert row0 % seq == 0
    b0 = row0 // seq
    half, k1, k1p = tabs["half"], tabs["k1"], tabs["k1p"]
    n2 = DFT_N2
    nct = HYENA_WIDTH // LANES
    sig_conv = sig is None
    if sig_conv:
        sig_arr = proj
        sig_spec = pl.BlockSpec((seq, LANES), lambda j, b: (b0 + b, sig_cols + j))
    else:
        sig_arr = sig
        sig_spec = pl.BlockSpec((seq, LANES), lambda j, b: (b, j))
    sw = short_w.astype(F32)
    sb = short_b.reshape(1, -1).astype(F32)
    consts = [tabs[n] for n in ("f1h", "f1l", "g3h", "g3l", "gch", "gcl", "gbh", "gbl", "wstep")]
    args = [sig_arr, proj, sw, sb, sw, sb, skip.reshape(1, -1).astype(F32), kf] + consts
    in_specs = [
        sig_spec,
        pl.BlockSpec((seq, LANES), lambda j, b: (b0 + b, gate_cols + j)),
        pl.BlockSpec((3, LANES), lambda j, b: (0, sig_cols + j)),
        pl.BlockSpec((1, LANES), lambda j, b: (0, sig_cols + j)),
        pl.BlockSpec((3, LANES), lambda j, b: (0, gate_cols + j)),
        pl.BlockSpec((1, LANES), lambda j, b: (0, gate_cols + j)),
        pl.BlockSpec((1, LANES), lambda j, b: (0, j)),
        pl.BlockSpec((None, None, k1 * 2 * n2, LANES), lambda j, b: (order, j, 0, 0)),
    ] + [_full_spec(a) for a in consts]
    kern = functools.partial(_hyena_conv_kernel, half=half, k1=k1, k1p=k1p, sig_conv=sig_conv)
    out_b0 = 0 if total_rows is None else b0
    kern, args, in_specs, out_shape, aliases = _merged_out(
        kern, args, in_specs, into, total_rows or nbatch * seq, HYENA_WIDTH, F32)
    return pl.pallas_call(
        kern,
        grid=(nct, nbatch),
        in_specs=in_specs,
        out_specs=pl.BlockSpec((seq, LANES), lambda j, b: (out_b0 + b, j)),
        out_shape=out_shape,
        input_output_aliases=aliases,
        scratch_shapes=[pltpu.VMEM((k1p * 2 * PITCH, LANES), F32), pltpu.VMEM((2 * n2, LANES), F32),
                        pltpu.VMEM((half * PITCH, LANES), F32), pltpu.VMEM((half * PITCH, LANES), F32)],
        compiler_params=_cparams(("parallel", "arbitrary")),
        name="hyena_conv",
    )(*args)


def _hyena_batch(proj, row0, nbatch, seq, tabs, kf, short_w, short_b, skip, total_rows, into):
    nct = HYENA_WIDTH // LANES
    z = _hyena_conv(None, 2 * nct, 0, proj, row0, nbatch, seq, tabs, kf, 0, short_w, short_b, skip[0])
    return _hyena_conv(z, 2 * nct, nct, proj, row0, nbatch, seq, tabs, kf, 1, short_w, short_b, skip[1],
                       total_rows=total_rows, into=into)


def _fnet_kernel(u_ref, chan_ref, m1_ref, gri_ref, wstep_ref, o_ref, zr_ref, zi_ref, a_ref, tw_ref, *, n1):
    n2 = DFT_N2
    slabs = 4

    def chan(g, carry):
        x = u_ref[pl.ds(pl.multiple_of(g * slabs * n2, slabs * n2), slabs * n2), :].astype(BF16)
        z = _dot(x, chan_ref[...])
        for i in range(slabs):
            r = pl.multiple_of((g * slabs + i) * PITCH, 8)
            zr_ref[pl.ds(r, n2), :] = z[i * n2:(i + 1) * n2, :LANES]
            zi_ref[pl.ds(r, n2), :] = z[i * n2:(i + 1) * n2, LANES:]
        return carry

    lax.fori_loop(0, n1 // slabs, chan, 0, unroll=2)

    def outer(base):
        xs = jnp.concatenate(
            [jnp.concatenate([_gather(zr_ref, base + j, n1), _gather(zi_ref, base + j, n1)], axis=0)
             for j in range(T2_GROUP)], axis=1)
        y = _dot(m1_ref[...], xs.astype(BF16))
        for j in range(T2_GROUP):
            a_ref[pl.ds(base + j, 2 * n1, stride=PITCH), :] = y[:, j * LANES:(j + 1) * LANES]

    _for_t2_groups(outer)

    def real_part(ks, xcat, tws):
        y = _dot(gri_ref[...], xcat.astype(BF16))
        return [_lane_block(y, j) for j in range(len(ks))]

    def put(k, rr, ri, y):
        zr_ref[pl.ds(pl.multiple_of(k * PITCH, 8), n2), :] = y

    _inner_blocks(a_ref, tw_ref, wstep_ref, n1, real_part, put)

    def emit(k2, carry):
        o_ref[pl.ds(pl.multiple_of(k2 * n1, n1), n1), :] = _gather(zr_ref, k2, n1).astype(o_ref.dtype)
        return carry

    lax.fori_loop(0, n2, emit, 0, unroll=4)


def _fnet_tables(seq):
    n2 = DFT_N2
    n1 = seq // n2
    j = jnp.arange(HEAD_DIM, dtype=jnp.int32)
    c, s = _cs(j[:, None] * j[None, :], HEAD_DIM)
    scale = (seq * HEAD_DIM) ** -0.5
    chan = jnp.concatenate([c * scale, -s * scale], axis=1).astype(BF16)
    kk = jnp.arange(n1, dtype=jnp.int32)
    c1, s1 = _cs(kk[:, None] * kk[None, :], n1)
    m1 = jnp.stack([jnp.concatenate([c1, s1], axis=1), jnp.concatenate([-s1, c1], axis=1)], axis=1)
    m1 = m1.reshape(2 * n1, 2 * n1).astype(BF16)
    t2 = jnp.arange(n2, dtype=jnp.int32)
    cr, cs_ = _cs(t2[:, None] * t2[None, :], n2)
    gri = jnp.concatenate([cr, cs_], axis=1).astype(BF16)
    wr, ws = _cs(t2, seq)
    wstep = jnp.concatenate([jnp.broadcast_to(wr[:, None], (n2, LANES)),
                             jnp.broadcast_to(-ws[:, None], (n2, LANES))], axis=0)
    return dict(n1=n1, chan=chan, m1=m1, gri=gri, wstep=wstep)


def _fnet_batch(proj, row0, nbatch, seq, tabs, total_rows, into):
    assert row0 % seq == 0
    b0 = row0 // seq
    n1 = tabs["n1"]
    consts = [tabs[n] for n in ("chan", "m1", "gri", "wstep")]
    in_specs = ([pl.BlockSpec((seq, LANES), lambda j, b: (b0 + b, OFF_FNET // LANES + j))]
                + [_full_spec(a) for a in consts])
    kern, args, in_specs, out_shape, aliases = _merged_out(
        functools.partial(_fnet_kernel, n1=n1), [proj] + consts, in_specs, into, total_rows, FNET_WIDTH, BF16)
    return pl.pallas_call(
        kern,
        grid=(FNET_HEADS, nbatch),
        in_specs=in_specs,
        out_specs=pl.BlockSpec((seq, LANES), lambda j, b: (b0 + b, j)),
        out_shape=out_shape,
        input_output_aliases=aliases,
        scratch_shapes=[pltpu.VMEM((n1 * PITCH, LANES), F32), pltpu.VMEM((n1 * PITCH, LANES), F32),
                        pltpu.VMEM((2 * n1 * PITCH, LANES), F32), pltpu.VMEM((2 * DFT_N2, LANES), F32)],
        compiler_params=_cparams(("parallel", "parallel")),
        name="fnet_mixer",
    )(*args)


def _pick_tile(t, pref):
    while t % pref:
        pref //= 2
    return pref


def kernel(x_prompt, x_sample, ln0_g, ln0_b, w_in, short_w, short_b, filt_w1, filt_b1, filt_w2, filt_b2, filt_w3, filt_b3, filt_freq, filt_w4, hyena_skip, w_fnet, b_fnet, attn_sink, w_out, ln1_g, ln1_b, w_gate, w_up, w_down, ln2_g, ln2_b):
    bp, lp, _ = x_prompt.shape
    bs, ls, _ = x_sample.shape
    tp, ts = bp * lp, bs * ls
    batches = ((0, bp, lp), (tp, bs, ls))
    tm = _pick_tile(math.gcd(tp, ts), 1024)
    tln = _pick_tile(math.gcd(tp, ts), 512)

    conv_tabs = {seq: _conv_tables(seq) for seq in {lp, ls}}
    fnet_tabs = {seq: _fnet_tables(seq) for seq in {lp, ls}}

    resid, xb = _ln0(x_prompt.reshape(tp, D_MODEL), x_sample.reshape(ts, D_MODEL), ln0_g, ln0_b, tln // 2)
    w_in_b, w_out_b, w_down_b = w_in.astype(BF16), w_out.astype(BF16), w_down.astype(BF16)
    for l in range(DEPTH):
        fw = (filt_w1[l], filt_b1[l], filt_w2[l], filt_b2[l], filt_w3[l], filt_b3[l], filt_freq[l], filt_w4[l])

        proj = _matmul([xb], w_in_b, None, F32, tm, 1024, "in_proj", layer=l)

        kf = {seq: _filter_spectrum(seq, conv_tabs[seq], fw) for seq in {lp, ls}}
        y_h = y_f = None
        for r0, nb, seq in batches:
            y_h = _hyena_batch(proj, r0, nb, seq, conv_tabs[seq], kf[seq], short_w[l], short_b[l], hyena_skip[l],
                               tp + ts, y_h)
            y_f = _fnet_batch(proj, r0, nb, seq, fnet_tabs[seq], tp + ts, y_f)
        y_f = _matmul([y_f], w_fnet, b_fnet[l], BF16, tm, 1024, "fnet_linear", layer=l)
        y_a = _attention(proj, attn_sink[l], tp, lp, ls)

        y = _matmul([y_h, y_f, y_a], w_out_b, None, F32, tm, 512, "out_proj", resid=resid, layer=l)
        xb, mu, rs = _ln(y, ln1_g[l], ln1_b[l], tln)
        resid = (y, mu, rs, ln1_g[l], ln1_b[l])

        hid = _gate_up(xb, w_gate, w_up, l, _pick_tile(math.gcd(tp, ts), 2048), FF_TILE)
        y = _matmul([hid], w_down_b, None, F32, _pick_tile(tm, 512), 512, "ffn_down", resid=resid, layer=l,
                    rows_inner=True)
        if l + 1 < DEPTH:
            xb, mu, rs = _ln(y, ln2_g[l], ln2_b[l], tln)
            resid = (y, mu, rs, ln2_g[l], ln2_b[l])
        else:
            y_p, y_s = _ln_final(y, ln2_g[l], ln2_b[l], tln // 2, tp)
    return y_p.reshape(bp, lp, D_MODEL), y_s.reshape(bs, ls, D_MODEL)
```
